```python
import math
import jax, jax.numpy as jnp
from jax import lax
import numpy as np

D_MODEL = 1024
BATCH = 16
SEQ = 2048
DEPTH = 4
DEC_BATCH = 32
DEC_SEQ = 32
PAST_LEN = 4096

CHUNK = 64
Q_BLOCK = 128
GMLP_CHUNK = 128
D_A = D_MODEL // 4
N_A_HEADS = 4
A_HEAD = D_A // N_A_HEADS
D_B = D_MODEL // 2
N_B_HEADS = 8
V_DIM = D_B // N_B_HEADS
NOPE_DIM = 64
ROPE_DIM = 32
Q_LORA = 384
KV_LORA = 256
ROPE_THETA = 10000.0
ATTN_SCALE = (NOPE_DIM + ROPE_DIM) ** -0.5
D_C = D_MODEL // 4
N_C_BLOCKS = 4
C_HEAD = D_C // N_C_BLOCKS
LRU_CONV = 4
LRU_C = 8.0
D_MIX = D_A + D_B + D_C
IN_SPLITS = (D_A, D_A, Q_LORA, KV_LORA, ROPE_DIM, D_C, D_C)
D_IN = sum(IN_SPLITS)
D_FF = 2816
FFN_CONV = 3
ALPHA = (2.0 * DEPTH) ** 0.25
BETA = (8.0 * DEPTH) ** -0.25
LN_EPS = 1e-5
RMS_EPS = 1e-6

kernel_name = "hymba_style_gmlp_mla_rglru_streaming_step"


def _split_points(sizes):
    pts, acc = [], 0
    for s in sizes[:-1]:
        acc += s
        pts.append(acc)
    return pts


def layer_norm(x, g, b):
    xf = x.astype(jnp.float32)
    mu = jnp.mean(xf, axis=-1, keepdims=True)
    var = jnp.mean(jnp.square(xf - mu), axis=-1, keepdims=True)
    return ((xf - mu) * lax.rsqrt(var + LN_EPS) * g + b).astype(x.dtype)


def rms_norm(x, g):
    xf = x.astype(jnp.float32)
    ms = jnp.mean(jnp.square(xf), axis=-1, keepdims=True)
    return (xf * lax.rsqrt(ms + RMS_EPS) * g).astype(x.dtype)


def apply_rope(x, pos):
    r = x.shape[-1]
    inv = 1.0 / (ROPE_THETA ** (jnp.arange(0, r, 2, dtype=jnp.float32) / r))
    ang = pos.astype(jnp.float32)[:, None] * inv[None, :]
    cos = jnp.cos(ang)[None, :, None, :]
    sin = jnp.sin(ang)[None, :, None, :]
    xf = x.astype(jnp.float32)
    x1, x2 = xf[..., : r // 2], xf[..., r // 2:]
    return jnp.concatenate([x1 * cos - x2 * sin, x1 * sin + x2 * cos], axis=-1).astype(x.dtype)


def causal_dwconv(x, prev, w, b):
    k = w.shape[0]
    t = x.shape[1]
    xp = jnp.concatenate([prev.astype(x.dtype), x], axis=1)
    y = b + xp[:, 0:t] * w[0]
    for j in range(1, k):
        y = y + xp[:, j:j + t] * w[j]
    return y, xp[:, t:]


def gmlp_spatial_gate(u, v, w_s, b_s):
    bsz, t, _ = v.shape
    ln = min(t, GMLP_CHUNK)
    n = t // ln
    w = jnp.tril(w_s[:, :ln, :ln])
    vb = v.reshape(bsz, n, ln, N_A_HEADS, A_HEAD)
    s = jnp.einsum('hij,bnjhd->bnihd', w, vb) + b_s[:, :ln].T[None, None, :, :, None]
    return u * s.reshape(bsz, t, D_A)


def mla_prompt_attention(q_nope, q_rope, k_nope, k_rope, v):
    bsz, t, h, _ = q_nope.shape
    nb = t // Q_BLOCK
    kchunk = jnp.arange(t) // CHUNK
    neg = jnp.finfo(jnp.float32).min

    def block(args):
        qn, qr, qc = args
        s = (jnp.einsum('bqhd,bkhd->bhqk', qn, k_nope)
             + jnp.einsum('bqhr,bkr->bhqk', qr, k_rope)).astype(jnp.float32) * ATTN_SCALE
        s = jnp.where(kchunk[None, :] <= qc[:, None], s, neg)
        p = jax.nn.softmax(s, axis=-1).astype(v.dtype)
        return jnp.einsum('bhqk,bkhd->bqhd', p, v)

    qn_b = q_nope.reshape(bsz, nb, Q_BLOCK, h, NOPE_DIM).swapaxes(0, 1)
    qr_b = q_rope.reshape(bsz, nb, Q_BLOCK, h, ROPE_DIM).swapaxes(0, 1)
    qc_b = (jnp.arange(t) // CHUNK).reshape(nb, Q_BLOCK)
    out = lax.map(block, (qn_b, qr_b, qc_b))
    return out.swapaxes(0, 1).reshape(bsz, t, h, V_DIM)


def mla_sample_attention(q_nope, q_rope, c_all, kr_all, w_uk, w_uv):
    q_lat = jnp.einsum('bqhd,chd->bqhc', q_nope, w_uk)
    s = (jnp.einsum('bqhc,bkc->bhqk', q_lat, c_all)
         + jnp.einsum('bqhr,bkr->bhqk', q_rope, kr_all)).astype(jnp.float32) * ATTN_SCALE
    p = jax.nn.softmax(s, axis=-1).astype(c_all.dtype)
    o_lat = jnp.einsum('bhqk,bkc->bqhc', p, c_all)
    return jnp.einsum('bqhc,chd->bqhd', o_lat, w_uv)


def linear_recurrence(a, b, h0):
    def combine(left, right):
        a1, b1 = left
        a2, b2 = right
        return a1 * a2, a2 * b1 + b2
    a_cum, b_cum = lax.associative_scan(combine, (a, b), axis=1)
    h = a_cum * h0[:, None, :] + b_cum
    return h, h[:, -1]


def trunk_layer(x, pos, lp, lat_cache, kr_cache, lru_conv_prev, lru_h0, ffn_conv_prev):
    bsz, t, _ = x.shape
    z = x @ lp['w_in']
    u, v, cq, ckv, kr, xc_in, gate_c = jnp.split(z, _split_points(IN_SPLITS), axis=-1)

    u = jax.nn.gelu(u)
    v = jax.nn.gelu(v)
    a_out = gmlp_spatial_gate(u, v, lp['gmlp_w_s'], lp['gmlp_b_s'])

    cq_n = rms_norm(cq, lp['mla_q_norm_g'])
    q = (cq_n @ lp['mla_w_uq']).reshape(bsz, t, N_B_HEADS, NOPE_DIM + ROPE_DIM)
    q_nope = q[..., :NOPE_DIM]
    q_rope = apply_rope(q[..., NOPE_DIM:], pos)
    ckv_n = rms_norm(ckv, lp['mla_kv_norm_g'])
    kr_rot = apply_rope(kr[:, :, None, :], pos)[:, :, 0, :]
    if lat_cache is None:
        k_nope = jnp.einsum('btc,chd->bthd', ckv_n, lp['mla_w_uk'])
        v_b = jnp.einsum('btc,chd->bthd', ckv_n, lp['mla_w_uv'])
        b_out = mla_prompt_attention(q_nope, q_rope, k_nope, kr_rot, v_b)
    else:
        c_all = jnp.concatenate([lat_cache.astype(ckv_n.dtype), ckv_n], axis=1)
        kr_all = jnp.concatenate([kr_cache.astype(kr_rot.dtype), kr_rot], axis=1)
        b_out = mla_sample_attention(q_nope, q_rope, c_all, kr_all, lp['mla_w_uk'], lp['mla_w_uv'])

    xc, lru_conv_new = causal_dwconv(xc_in, lru_conv_prev, lp['lru_conv_w'], lp['lru_conv_b'])
    xb = xc.reshape(bsz, t, N_C_BLOCKS, C_HEAD)
    r = jax.nn.sigmoid(jnp.einsum('btnd,nde->btne', xb, lp['lru_w_r']).reshape(bsz, t, D_C) + lp['lru_b_r'])
    ig = jax.nn.sigmoid(jnp.einsum('btnd,nde->btne', xb, lp['lru_w_i']).reshape(bsz, t, D_C) + lp['lru_b_i'])
    log_a = -LRU_C * r.astype(jnp.float32) * jax.nn.softplus(-lp['lru_lam'].astype(jnp.float32))
    a = jnp.exp(log_a)
    b_in = jnp.sqrt(-jnp.expm1(2.0 * log_a)) * (ig * xc).astype(jnp.float32)
    h, h_last = linear_recurrence(a, b_in, lru_h0.astype(jnp.float32))
    c_out = h.astype(x.dtype) * jax.nn.gelu(gate_c)

    mix = jnp.concatenate([a_out, b_out.reshape(bsz, t, D_B), c_out], axis=-1) @ lp['w_o']
    x = layer_norm(ALPHA * x + mix, lp['ln1_g'], lp['ln1_b'])

    up = x @ lp['ffn_w_up']
    upc, ffn_conv_new = causal_dwconv(up, ffn_conv_prev, lp['ffn_conv_w'], lp['ffn_conv_b'])
    g_ff, val = upc[..., :D_FF], upc[..., D_FF:]
    f = (jax.nn.gelu(g_ff) * val) @ lp['ffn_w_down']
    x = layer_norm(ALPHA * x + f, lp['ln2_g'], lp['ln2_b'])
    return x, v, ckv_n, kr_rot, lru_conv_new, h_last.astype(x.dtype), ffn_conv_new


def setup_inputs(seed: int = 0) -> dict:
    key = jax.random.key(seed)
    ks = iter(jax.random.split(key, 40))
    f32 = jnp.float32

    def nrm(shape, scale):
        return jax.random.normal(next(ks), shape, f32) * scale

    u = jax.random.uniform(next(ks), (DEPTH, D_C), f32, minval=0.9, maxval=0.999)
    a_base = u ** (1.0 / LRU_C)
    lam = jnp.log(a_base) - jnp.log1p(-a_base)
    return {
        'x_prompt': nrm((BATCH, SEQ, D_MODEL), 1.0),
        'x_sample': nrm((DEC_BATCH, DEC_SEQ, D_MODEL), 1.0),
        'cache_kv_latent': nrm((DEPTH, DEC_BATCH, PAST_LEN, KV_LORA), 1.0),
        'cache_k_rope': nrm((DEPTH, DEC_BATCH, PAST_LEN, ROPE_DIM), 1.0),
        'state_lru_h': nrm((DEPTH, DEC_BATCH, D_C), 0.5),
        'state_lru_conv': nrm((DEPTH, DEC_BATCH, LRU_CONV - 1, D_C), 1.0),
        'state_ffn_conv': nrm((DEPTH, DEC_BATCH, FFN_CONV - 1, 2 * D_FF), 1.0),
        'ln1_g': 1.0 + nrm((DEPTH, D_MODEL), 0.02),
        'ln1_b': nrm((DEPTH, D_MODEL), 0.02),
        'ln2_g': 1.0 + nrm((DEPTH, D_MODEL), 0.02),
        'ln2_b': nrm((DEPTH, D_MODEL), 0.02),
        'w_in': nrm((DEPTH, D_MODEL, D_IN), D_MODEL ** -0.5),
        'w_o': nrm((DEPTH, D_MIX, D_MODEL), BETA * D_MIX ** -0.5),
        'gmlp_w_s': nrm((DEPTH, N_A_HEADS, GMLP_CHUNK, GMLP_CHUNK), GMLP_CHUNK ** -0.5),
        'gmlp_b_s': 1.0 + nrm((DEPTH, N_A_HEADS, GMLP_CHUNK), 0.02),
        'mla_q_norm_g': 1.0 + nrm((DEPTH, Q_LORA), 0.02),
        'mla_w_uq': nrm((DEPTH, Q_LORA, N_B_HEADS * (NOPE_DIM + ROPE_DIM)), Q_LORA ** -0.5),
        'mla_kv_norm_g': 1.0 + nrm((DEPTH, KV_LORA), 0.02),
        'mla_w_uk': nrm((DEPTH, KV_LORA, N_B_HEADS, NOPE_DIM), KV_LORA ** -0.5),
        'mla_w_uv': nrm((DEPTH, KV_LORA, N_B_HEADS, V_DIM), KV_LORA ** -0.5),
        'lru_conv_w': nrm((DEPTH, LRU_CONV, D_C), LRU_CONV ** -0.5),
        'lru_conv_b': nrm((DEPTH, D_C), 0.01),
        'lru_w_r': nrm((DEPTH, N_C_BLOCKS, C_HEAD, C_HEAD), C_HEAD ** -0.5),
        'lru_b_r': nrm((DEPTH, D_C), 0.01),
        'lru_w_i': nrm((DEPTH, N_C_BLOCKS, C_HEAD, C_HEAD), C_HEAD ** -0.5),
        'lru_b_i': nrm((DEPTH, D_C), 0.01),
        'lru_lam': lam,
        'ffn_w_up': nrm((DEPTH, D_MODEL, 2 * D_FF), D_MODEL ** -0.5),
        'ffn_conv_w': nrm((DEPTH, FFN_CONV, 2 * D_FF), FFN_CONV ** -0.5),
        'ffn_conv_b': nrm((DEPTH, 2 * D_FF), 0.01),
        'ffn_w_down': nrm((DEPTH, D_FF, D_MODEL), BETA * D_FF ** -0.5),
    }


def reference(x_prompt, x_sample, cache_kv_latent, cache_k_rope, state_lru_h, state_lru_conv,
              state_ffn_conv, ln1_g, ln1_b, ln2_g, ln2_b, w_in, w_o, gmlp_w_s, gmlp_b_s,
              mla_q_norm_g, mla_w_uq, mla_kv_norm_g, mla_w_uk, mla_w_uv, lru_conv_w, lru_conv_b,
              lru_w_r, lru_b_r, lru_w_i, lru_b_i, lru_lam, ffn_w_up, ffn_conv_w, ffn_conv_b,
              ffn_w_down):
    bp, s_len, _ = x_prompt.shape
    t_len = x_sample.shape[1]
    past = cache_kv_latent.shape[2]
    pos_p = jnp.arange(s_len)
    pos_d = past + jnp.arange(t_len)
    zeros_lru_conv = jnp.zeros((bp, LRU_CONV - 1, D_C), x_prompt.dtype)
    zeros_lru_h = jnp.zeros((bp, D_C), x_prompt.dtype)
    zeros_ffn_conv = jnp.zeros((bp, FFN_CONV - 1, 2 * D_FF), x_prompt.dtype)

    xp, xd = x_prompt, x_sample
    p_lat, p_kr, p_h, p_lconv, p_fconv = [], [], [], [], []
    s_lat, s_kr, s_v, s_h, s_lconv, s_fconv = [], [], [], [], [], []
    for l in range(DEPTH):
        lp = {
            'w_in': w_in[l], 'w_o': w_o[l], 'ln1_g': ln1_g[l], 'ln1_b': ln1_b[l],
            'ln2_g': ln2_g[l], 'ln2_b': ln2_b[l], 'gmlp_w_s': gmlp_w_s[l], 'gmlp_b_s': gmlp_b_s[l],
            'mla_q_norm_g': mla_q_norm_g[l], 'mla_w_uq': mla_w_uq[l],
            'mla_kv_norm_g': mla_kv_norm_g[l], 'mla_w_uk': mla_w_uk[l], 'mla_w_uv': mla_w_uv[l],
            'lru_conv_w': lru_conv_w[l], 'lru_conv_b': lru_conv_b[l], 'lru_w_r': lru_w_r[l],
            'lru_b_r': lru_b_r[l], 'lru_w_i': lru_w_i[l], 'lru_b_i': lru_b_i[l], 'lru_lam': lru_lam[l],
            'ffn_w_up': ffn_w_up[l], 'ffn_conv_w': ffn_conv_w[l], 'ffn_conv_b': ffn_conv_b[l],
            'ffn_w_down': ffn_w_down[l],
        }
        xp, _, lat_p, kr_p, lconv_p, h_p, fconv_p = trunk_layer(
            xp, pos_p, lp, None, None, zeros_lru_conv, zeros_lru_h, zeros_ffn_conv)
        xd, v_d, lat_d, kr_d, lconv_d, h_d, fconv_d = trunk_layer(
            xd, pos_d, lp, cache_kv_latent[l], cache_k_rope[l], state_lru_conv[l],
            state_lru_h[l], state_ffn_conv[l])
        p_lat.append(lat_p); p_kr.append(kr_p); p_h.append(h_p)
        p_lconv.append(lconv_p); p_fconv.append(fconv_p)
        s_lat.append(lat_d); s_kr.append(kr_d); s_v.append(v_d); s_h.append(h_d)
        s_lconv.append(lconv_d); s_fconv.append(fconv_d)

    return (xp, xd,
            jnp.stack(p_lat), jnp.stack(p_kr), jnp.stack(p_h), jnp.stack(p_lconv), jnp.stack(p_fconv),
            jnp.stack(s_lat), jnp.stack(s_kr), jnp.stack(s_v), jnp.stack(s_h), jnp.stack(s_lconv),
            jnp.stack(s_fconv))
```

```python
import functools
import math

import jax
import jax.numpy as jnp
from jax import lax
from jax.experimental import pallas as pl
from jax.experimental.pallas import tpu as pltpu

F32 = jnp.float32
BF16 = jnp.bfloat16

D_MODEL = 1024
DEPTH = 4
CHUNK = 64
GMLP_CHUNK = 128
D_A = 256
N_A_HEADS = 4
A_HEAD = 64
D_B = 512
N_B_HEADS = 8
V_DIM = 64
NOPE_DIM = 64
ROPE_DIM = 32
Q_LORA = 384
KV_LORA = 256
ROPE_THETA = 10000.0
ATTN_SCALE = (NOPE_DIM + ROPE_DIM) ** -0.5
D_C = 256
N_C_BLOCKS = 4
C_HEAD = 64
LRU_CONV = 4
LRU_C = 8.0
D_FF = 2816
FFN_CONV = 3
ALPHA = (2.0 * DEPTH) ** 0.25
LN_EPS = 1e-5
RMS_EPS = 1e-6

LANE = 128
SUBLANE = 8
HEAD_BLK = 128
VMEM_LIMIT = 56 * 1024 * 1024

C_U, C_V, C_CQ, C_CKV, C_KR, C_XC, C_GATE, C_END = 0, 256, 512, 896, 1152, 1408, 1664, 1920

FF_CHUNK = 256
N_FF = D_FF // FF_CHUNK


def _dot(a, b):
    return jnp.dot(a, b, preferred_element_type=F32)


def _rms_norm(x, g):
    ms = jnp.mean(x * x, axis=-1, keepdims=True)
    return x * lax.rsqrt(ms + RMS_EPS) * g


def _layer_norm(x, g, b):
    mu = jnp.mean(x, axis=-1, keepdims=True)
    xc = x - mu
    var = jnp.mean(xc * xc, axis=-1, keepdims=True)
    return xc * lax.rsqrt(var + LN_EPS) * g + b


def _mixer_in_kernel(nb, tt, ln, prompt,
                     x_ref, win_ref, wqb_ref, wqs_ref, wkp_ref, wuv_ref, wr_ref, wi_ref,
                     ws_ref, bs_ref, qg_ref, kvg_ref, cw_ref, cb_ref, br_ref, bi_ref, lam_ref,
                     tqc_ref, tqs_ref, tkc_ref, tks_ref, h0_ref, conv0_ref,
                     *rest):
    if prompt:
        (a_ref, c_ref, q_ref, lat_ref, kr_ref, hst_ref, cst_ref, k_ref, v_ref,
         xb_ref, cv_ref, ac_ref, bc_ref, hh_ref, hc_ref) = rest
        vg_ref = None
    else:
        (a_ref, c_ref, q_ref, lat_ref, kr_ref, hst_ref, cst_ref, vg_ref,
         xb_ref, cv_ref, ac_ref, bc_ref, hh_ref, hc_ref) = rest
        k_ref = v_ref = None
    rows = nb * tt
    t_idx = pl.program_id(1)

    xb_ref[...] = x_ref[...].astype(BF16)

    def proj(c0, c1):
        return _dot(xb_ref[...], win_ref[:, c0:c1])

    u = jax.nn.gelu(proj(C_U, C_V))
    v = jax.nn.gelu(proj(C_V, C_CQ))
    if vg_ref is not None:
        vg_ref[...] = v
    vb = v.astype(BF16)
    row_i = lax.broadcasted_iota(jnp.int32, (ln, ln), 0)
    col_i = lax.broadcasted_iota(jnp.int32, (ln, ln), 1)
    w_tril = [jnp.where(row_i >= col_i, ws_ref[h], 0.0).astype(BF16) for h in range(N_A_HEADS)]
    head_of_lane = lax.broadcasted_iota(jnp.int32, (ln, D_A), 1) // A_HEAD
    for c in range(rows // ln):
        r0 = c * ln
        vc = vb[r0:r0 + ln]
        gate = bs_ref[...]
        for h in range(N_A_HEADS):
            gate = gate + _dot(w_tril[h], jnp.where(head_of_lane == h, vc, jnp.zeros_like(vc)))
        a_ref[r0:r0 + ln, :] = (u[r0:r0 + ln] * gate).astype(BF16)

    cqn = _rms_norm(proj(C_CQ, C_CKV), qg_ref[...]).astype(BF16)
    tqc = tqc_ref[...]
    tqs = tqs_ref[...]
    for j in range(N_B_HEADS // 2):
        c0, c1 = 2 * j * HEAD_BLK, 2 * (j + 1) * HEAD_BLK
        q_ref[:, c0:c1] = (_dot(cqn, wqb_ref[:, c0:c1]) * tqc
                           + _dot(cqn, wqs_ref[:, c0:c1]) * tqs).astype(BF16)
    ckvn = _rms_norm(proj(C_CKV, C_KR), kvg_ref[...])
    lat_ref[...] = ckvn
    kr2 = proj(C_KR, C_XC)
    kr_rot = kr2[:, :HEAD_BLK] * tkc_ref[...] + kr2[:, HEAD_BLK:] * tks_ref[...]
    kr_ref[...] = kr_rot[:, :ROPE_DIM]
    if prompt:
        ckvn_b = ckvn.astype(BF16)
        for j in range(N_B_HEADS // 2):
            c0 = 2 * j * HEAD_BLK
            kp = _dot(ckvn_b, wkp_ref[:, c0:c0 + 2 * HEAD_BLK])
            k_ref[:, c0:c0 + HEAD_BLK] = (kp[:, :HEAD_BLK] + kr_rot).astype(BF16)
            k_ref[:, c0 + HEAD_BLK:c0 + 2 * HEAD_BLK] = (kp[:, HEAD_BLK:] + kr_rot).astype(BF16)
        v_ref[...] = _dot(ckvn_b, wuv_ref[...]).astype(BF16)

    xc_in = proj(C_XC, C_GATE)
    gate_c = jax.nn.gelu(proj(C_GATE, C_END))

    @pl.when(t_idx == 0)
    def _():
        cv_ref[:, 0:SUBLANE, :] = conv0_ref[...]
        hc_ref[...] = h0_ref[...]

    xcs = []
    for s in range(nb):
        cv_ref[s, SUBLANE:SUBLANE + tt, :] = xc_in[s * tt:(s + 1) * tt]
        acc = cb_ref[...]
        for j in range(LRU_CONV):
            off = SUBLANE - (LRU_CONV - 1) + j
            acc = acc + cv_ref[s, off:off + tt, :] * cw_ref[j:j + 1, :]
        xcs.append(acc)
        tail = cv_ref[s, tt:tt + SUBLANE, :]
        cst_ref[s] = tail
        cv_ref[s, 0:SUBLANE, :] = tail
    xc = xcs[0] if nb == 1 else jnp.concatenate(xcs, axis=0)
    xcb = xc.astype(BF16)
    r = jax.nn.sigmoid(_dot(xcb, wr_ref[...]) + br_ref[...])
    ig = jax.nn.sigmoid(_dot(xcb, wi_ref[...]) + bi_ref[...])
    log_a = (-LRU_C) * r * jax.nn.softplus(-lam_ref[...])
    a = jnp.exp(log_a)
    b_in = jnp.sqrt(-jnp.tanh(log_a) * (a * a + 1.0)) * (ig * xc)

    groups = rows // SUBLANE
    a3 = a.reshape(groups, SUBLANE, D_C)
    b3 = b_in.reshape(groups, SUBLANE, D_C)
    sub = lax.broadcasted_iota(jnp.int32, (groups, SUBLANE, D_C), 1)
    for k in (1, 2, 4):
        keep = sub >= k
        a_sh = jnp.where(keep, pltpu.roll(a3, k, 1), 1.0)
        b_sh = jnp.where(keep, pltpu.roll(b3, k, 1), 0.0)
        b3 = a3 * b_sh + b3
        a3 = a3 * a_sh
    ac_ref[...] = a3
    bc_ref[...] = b3
    gs = tt // SUBLANE
    for s in range(nb):
        def body(g, hb):
            hr = ac_ref[g] * hb + bc_ref[g]
            hh_ref[g] = hr
            return jnp.broadcast_to(hr[SUBLANE - 1:SUBLANE, :], (SUBLANE, D_C))
        hb = lax.fori_loop(s * gs, (s + 1) * gs, body, hc_ref[s], unroll=4)
        hc_ref[s] = hb
        hst_ref[s] = hb
    h = hh_ref[...].reshape(rows, D_C)
    c_ref[...] = (h * gate_c).astype(BF16)


def _const_spec(shape):
    nd = len(shape)
    return pl.BlockSpec(shape, lambda *_: (0,) * nd)


def _mixer_in(x2d, lw, tabs, h0, conv0, *, n_seq, seq_len, nb, tt, ln, prompt):
    nt = seq_len // tt
    assert nb == 1 or nt == 1
    rows = nb * tt
    n_rows = n_seq * seq_len
    grid = (n_seq // nb, nt)
    row_map = lambda b, t: (b * nt + t, 0)
    tab_map = lambda b, t: (t, 0)
    seq_map = lambda b, t: (b, 0, 0)

    weights = [lw['w_in'], lw['wq_blk'], lw['wq_swp'], lw['wk_pad'], lw['w_uv'], lw['w_r'], lw['w_i'],
               lw['w_s'], lw['b_s'], lw['q_g'], lw['kv_g'], lw['conv_w'], lw['conv_b'], lw['b_r'],
               lw['b_i'], lw['lam']]
    in_specs = ([pl.BlockSpec((rows, D_MODEL), row_map)]
                + [_const_spec(w.shape) for w in weights]
                + [pl.BlockSpec((rows, 2 * HEAD_BLK), tab_map), pl.BlockSpec((rows, 2 * HEAD_BLK), tab_map),
                   pl.BlockSpec((rows, HEAD_BLK), tab_map), pl.BlockSpec((rows, HEAD_BLK), tab_map),
                   pl.BlockSpec((nb, SUBLANE, D_C), seq_map), pl.BlockSpec((nb, SUBLANE, D_C), seq_map)])
    out_shape = [jax.ShapeDtypeStruct((n_rows, D_A), BF16),
                 jax.ShapeDtypeStruct((n_rows, D_C), BF16),
                 jax.ShapeDtypeStruct((n_rows, N_B_HEADS * HEAD_BLK), BF16),
                 jax.ShapeDtypeStruct((n_rows, KV_LORA), F32),
                 jax.ShapeDtypeStruct((n_rows, ROPE_DIM), F32),
                 jax.ShapeDtypeStruct((n_seq, SUBLANE, D_C), F32),
                 jax.ShapeDtypeStruct((n_seq, SUBLANE, D_C), F32)]
    out_specs = [pl.BlockSpec((rows, D_A), row_map), pl.BlockSpec((rows, D_C), row_map),
                 pl.BlockSpec((rows, N_B_HEADS * HEAD_BLK), row_map), pl.BlockSpec((rows, KV_LORA), row_map),
                 pl.BlockSpec((rows, ROPE_DIM), row_map),
                 pl.BlockSpec((nb, SUBLANE, D_C), seq_map), pl.BlockSpec((nb, SUBLANE, D_C), seq_map)]
    if prompt:
        out_shape += [jax.ShapeDtypeStruct((n_rows, N_B_HEADS * HEAD_BLK), BF16),
                      jax.ShapeDtypeStruct((n_rows, D_B), BF16)]
        out_specs += [pl.BlockSpec((rows, N_B_HEADS * HEAD_BLK), row_map), pl.BlockSpec((rows, D_B), row_map)]
    else:
        out_shape += [jax.ShapeDtypeStruct((n_rows, D_A), F32)]
        out_specs += [pl.BlockSpec((rows, D_A), row_map)]
    groups = rows // SUBLANE
    scratch = [pltpu.VMEM((rows, D_MODEL), BF16),
               pltpu.VMEM((nb, tt + SUBLANE, D_C), F32),
               pltpu.VMEM((groups, SUBLANE, D_C), F32),
               pltpu.VMEM((groups, SUBLANE, D_C), F32),
               pltpu.VMEM((groups, SUBLANE, D_C), F32),
               pltpu.VMEM((nb, SUBLANE, D_C), F32)]
    return pl.pallas_call(
        functools.partial(_mixer_in_kernel, nb, tt, ln, prompt),
        grid=grid, in_specs=in_specs, out_specs=out_specs, out_shape=out_shape,
        scratch_shapes=scratch,
        compiler_params=pltpu.CompilerParams(dimension_semantics=("arbitrary", "arbitrary"),
                                             vmem_limit_bytes=VMEM_LIMIT),
        name="mixer_in_prompt" if prompt else "mixer_in_sample",
    )(x2d, *weights, *tabs, h0, conv0)


NEG_BIG = -1e30


def _last_kv_tile(qi, tq, tk):
    return ((qi + 1) * tq - 1) // tk


def _attn_prompt_kernel(tq, tk, q_ref, k_ref, v_ref, o_ref, m_ref, l_ref, acc_ref):
    qi = pl.program_id(1)
    ki = pl.program_id(2)
    last = _last_kv_tile(qi, tq, tk)
    n_pairs = N_B_HEADS // 2
    low_half = lax.broadcasted_iota(jnp.int32, (tq, LANE), 1) < V_DIM

    @pl.when(ki == 0)
    def _():
        m_ref[...] = jnp.full(m_ref.shape, NEG_BIG, F32)
        l_ref[...] = jnp.zeros(l_ref.shape, F32)
        acc_ref[...] = jnp.zeros(acc_ref.shape, F32)

    def step(masked):
        if masked:
            q_chunk = (qi * tq + lax.broadcasted_iota(jnp.int32, (tq, tk), 0)) // CHUNK
            k_chunk = (ki * tk + lax.broadcasted_iota(jnp.int32, (tq, tk), 1)) // CHUNK
            visible = k_chunk <= q_chunk
        for j in range(n_pairs):
            vp = v_ref[:, j * LANE:(j + 1) * LANE]
            upd = []
            for h in (2 * j, 2 * j + 1):
                q_h = q_ref[:, h * HEAD_BLK:(h + 1) * HEAD_BLK]
                k_h = k_ref[:, h * HEAD_BLK:(h + 1) * HEAD_BLK]
                s = lax.dot_general(q_h, k_h, (((1,), (1,)), ((), ())), preferred_element_type=F32)
                if masked:
                    s = jnp.where(visible, s, NEG_BIG)
                m_prev = m_ref[h]
                m_new = jnp.maximum(m_prev, jnp.max(s, axis=1, keepdims=True))
                alpha = jnp.exp(m_prev - m_new)
                p = jnp.exp(s - jnp.concatenate([m_new] * (tk // LANE), axis=1))
                l_ref[h] = alpha * l_ref[h] + jnp.sum(p, axis=1, keepdims=True)
                m_ref[h] = m_new
                upd.append((alpha, _dot(p.astype(BF16), vp)))
            acc = acc_ref[j]
            acc_ref[j] = jnp.where(low_half, upd[0][0] * acc + upd[0][1], upd[1][0] * acc + upd[1][1])

    needs_mask = (ki * tk + tk - 1) // CHUNK > (qi * tq) // CHUNK

    @pl.when(jnp.logical_and(ki <= last, needs_mask))
    def _():
        step(True)

    @pl.when(jnp.logical_and(ki <= last, jnp.logical_not(needs_mask)))
    def _():
        step(False)

    @pl.when(ki == last)
    def _():
        for j in range(n_pairs):
            l_pair = jnp.where(low_half, l_ref[2 * j], l_ref[2 * j + 1])
            o_ref[:, j * LANE:(j + 1) * LANE] = (acc_ref[j] / l_pair).astype(BF16)


def _attn_prompt(q, k, v, *, n_seq, seq_len, tq, tk):
    nq, nk = seq_len // tq, seq_len // tk
    kv_map = lambda b, i, j: (b * nk + jnp.minimum(j, _last_kv_tile(i, tq, tk)), 0)
    q_map = lambda b, i, j: (b * nq + i, 0)
    return pl.pallas_call(
        functools.partial(_attn_prompt_kernel, tq, tk),
        grid=(n_seq, nq, nk),
        in_specs=[pl.BlockSpec((tq, N_B_HEADS * HEAD_BLK), q_map),
                  pl.BlockSpec((tk, N_B_HEADS * HEAD_BLK), kv_map),
                  pl.BlockSpec((tk, D_B), kv_map)],
        out_specs=pl.BlockSpec((tq, D_B), q_map),
        out_shape=jax.ShapeDtypeStruct((n_seq * seq_len, D_B), BF16),
        scratch_shapes=[pltpu.VMEM((N_B_HEADS, tq, LANE), F32),
                        pltpu.VMEM((N_B_HEADS, tq, LANE), F32),
                        pltpu.VMEM((N_B_HEADS // 2, tq, LANE), F32)],
        compiler_params=pltpu.CompilerParams(dimension_semantics=("arbitrary", "arbitrary", "arbitrary"),
                                             vmem_limit_bytes=VMEM_LIMIT),
        name="attn_prompt",
    )(q, k, v)


def _attn_sample_kernel(t_new, layer_unused, q_ref, clat_ref, ckr_ref, nlat_ref, nkr_ref, wkt_ref, wv_ref,
                        o_ref, ql_ref, qr_ref, m_ref, l_ref, acc_ref):
    ki = pl.program_id(1)
    nt = (((1,), (1,)), ((), ()))
    rows = N_B_HEADS * t_new

    def attend(c_b, kr_b):
        width = c_b.shape[0]
        s = (lax.dot_general(ql_ref[...], c_b, nt, preferred_element_type=F32)
             + lax.dot_general(qr_ref[...], kr_b, nt, preferred_element_type=F32))
        m_prev = m_ref[...]
        m_new = jnp.maximum(m_prev, jnp.max(s, axis=1, keepdims=True))
        alpha = jnp.exp(m_prev - m_new)
        if width % LANE == 0:
            m_b = jnp.concatenate([m_new] * (width // LANE), axis=1)
        else:
            m_b = m_new[:, :width]
        p = jnp.exp(s - m_b)
        l_ref[...] = alpha * l_ref[...] + jnp.sum(p, axis=1, keepdims=True)
        m_ref[...] = m_new
        acc_ref[...] = (jnp.concatenate([alpha, alpha], axis=1) * acc_ref[...]
                        + _dot(p.astype(BF16), c_b))

    @pl.when(ki == 0)
    def _():
        for h in range(N_B_HEADS):
            q_h = q_ref[:, h * HEAD_BLK:(h + 1) * HEAD_BLK]
            ql_ref[h * t_new:(h + 1) * t_new, :] = _dot(q_h, wkt_ref[h]).astype(BF16)
            qr_ref[h * t_new:(h + 1) * t_new, :] = q_h[:, :ROPE_DIM]
        m_ref[...] = jnp.full(m_ref.shape, NEG_BIG, F32)
        l_ref[...] = jnp.zeros(l_ref.shape, F32)
        acc_ref[...] = jnp.zeros(acc_ref.shape, F32)
        attend(nlat_ref[...].astype(BF16), nkr_ref[...].astype(BF16))

    attend(clat_ref[...].astype(BF16), ckr_ref[...].astype(BF16))

    @pl.when(ki == pl.num_programs(1) - 1)
    def _():
        l = l_ref[...]
        o_lat = (acc_ref[...] / jnp.concatenate([l, l], axis=1)).astype(BF16)
        out = _dot(o_lat[0:t_new], wv_ref[0])
        for h in range(1, N_B_HEADS):
            out = out + _dot(o_lat[h * t_new:(h + 1) * t_new], wv_ref[h])
        o_ref[...] = out.astype(BF16)


def _attn_sample(q, cache_lat, cache_kr, new_lat, new_kr, wkt, wv_pad, *, layer, n_seq, t_new, tk):
    past = cache_lat.shape[2]
    nk = past // tk
    row_map = lambda b, j: (b, 0)
    rows = N_B_HEADS * t_new
    return pl.pallas_call(
        functools.partial(_attn_sample_kernel, t_new, layer),
        grid=(n_seq, nk),
        in_specs=[pl.BlockSpec((t_new, N_B_HEADS * HEAD_BLK), row_map),
                  pl.BlockSpec((None, None, tk, KV_LORA), lambda b, j: (layer, b, j, 0)),
                  pl.BlockSpec((None, None, tk, ROPE_DIM), lambda b, j: (layer, b, j, 0)),
                  pl.BlockSpec((t_new, KV_LORA), row_map),
                  pl.BlockSpec((t_new, ROPE_DIM), row_map),
                  _const_spec(wkt.shape), _const_spec(wv_pad.shape)],
        out_specs=pl.BlockSpec((t_new, D_B), row_map),
        out_shape=jax.ShapeDtypeStruct((n_seq * t_new, D_B), BF16),
        scratch_shapes=[pltpu.VMEM((rows, KV_LORA), BF16),
                        pltpu.VMEM((rows, ROPE_DIM), BF16),
                        pltpu.VMEM((rows, LANE), F32),
                        pltpu.VMEM((rows, LANE), F32),
                        pltpu.VMEM((rows, KV_LORA), F32)],
        compiler_params=pltpu.CompilerParams(dimension_semantics=("arbitrary", "arbitrary"),
                                             vmem_limit_bytes=VMEM_LIMIT),
        name="attn_sample",
    )(q, cache_lat, cache_kr, new_lat, new_kr, wkt, wv_pad)


def _mixer_out_kernel(nb, tt,
                      x_ref, a_ref, b_ref, c_ref, woa_ref, wob_ref, woc_ref, g1_ref, b1_ref,
                      wug_ref, wuv_ref, cwg_ref, cwv_ref, cbg_ref, cbv_ref, wd_ref, g2_ref, b2_ref,
                      sg_ref, sv_ref,
                      y_ref, og_ref, ov_ref,
                      x1_ref, x1b_ref, acc_ref, bufg_ref, bufv_ref, carg_ref, carv_ref):
    t_idx = pl.program_id(1)
    mix = _dot(a_ref[...], woa_ref[...]) + _dot(b_ref[...], wob_ref[...]) + _dot(c_ref[...], woc_ref[...])
    x1 = _layer_norm(ALPHA * x_ref[...] + mix, g1_ref[...], b1_ref[...])
    x1_ref[...] = x1
    x1b_ref[...] = x1.astype(BF16)
    acc_ref[...] = jnp.zeros(acc_ref.shape, F32)

    @pl.when(t_idx == 0)
    def _():
        carg_ref[...] = sg_ref[...]
        carv_ref[...] = sv_ref[...]

    def conv(up, buf_ref, car_ref, out_ref, w, bias, j):
        outs = []
        for s in range(nb):
            buf_ref[s, 0:SUBLANE, :] = car_ref[j, s]
            buf_ref[s, SUBLANE:SUBLANE + tt, :] = up[s * tt:(s + 1) * tt]
            acc = bias
            for i in range(FFN_CONV):
                off = SUBLANE - (FFN_CONV - 1) + i
                acc = acc + buf_ref[s, off:off + tt, :] * w[i:i + 1, :]
            outs.append(acc)
            tail = buf_ref[s, tt:tt + SUBLANE, :]
            car_ref[j, s] = tail
            out_ref[j, s] = tail
        return outs[0] if nb == 1 else jnp.concatenate(outs, axis=0)

    def chunk(j, carry):
        xb = x1b_ref[...]
        gate = conv(_dot(xb, wug_ref[j]), bufg_ref, carg_ref, og_ref, cwg_ref[j], cbg_ref[j], j)
        val = conv(_dot(xb, wuv_ref[j]), bufv_ref, carv_ref, ov_ref, cwv_ref[j], cbv_ref[j], j)
        act = (jax.nn.gelu(gate) * val).astype(BF16)
        acc_ref[...] += _dot(act, wd_ref[j])
        return carry

    lax.fori_loop(0, N_FF, chunk, 0)
    y_ref[...] = _layer_norm(ALPHA * x1_ref[...] + acc_ref[...], g2_ref[...], b2_ref[...])


def _mixer_out(x2d, a, b, c, lw, sg, sv, *, n_seq, seq_len, nb, tt):
    nt = seq_len // tt
    assert nb == 1 or nt == 1
    rows = nb * tt
    n_rows = n_seq * seq_len
    row_map = lambda bi, t: (bi * nt + t, 0)
    st_map = lambda bi, t: (0, bi, 0, 0)
    weights_a = [lw['w_oa'], lw['w_ob'], lw['w_oc'], lw['ln1_g'], lw['ln1_b'],
                 lw['w_up_g'], lw['w_up_v'], lw['fcw_g'], lw['fcw_v'], lw['fcb_g'], lw['fcb_v'],
                 lw['w_down'], lw['ln2_g'], lw['ln2_b']]
    st_spec = pl.BlockSpec((N_FF, nb, SUBLANE, FF_CHUNK), st_map)
    in_specs = ([pl.BlockSpec((rows, D_MODEL), row_map), pl.BlockSpec((rows, D_A), row_map),
                 pl.BlockSpec((rows, D_B), row_map), pl.BlockSpec((rows, D_C), row_map)]
                + [pl.BlockSpec(w.shape, functools.partial(lambda nd, *_: (0,) * nd, len(w.shape)),
                                pipeline_mode=pl.Buffered(1)) for w in weights_a]
                + [st_spec, st_spec])
    st_shape = jax.ShapeDtypeStruct((N_FF, n_seq, SUBLANE, FF_CHUNK), F32)
    return pl.pallas_call(
        functools.partial(_mixer_out_kernel, nb, tt),
        grid=(n_seq // nb, nt),
        in_specs=in_specs,
        out_specs=[pl.BlockSpec((rows, D_MODEL), row_map), st_spec, st_spec],
        out_shape=[jax.ShapeDtypeStruct((n_rows, D_MODEL), F32), st_shape, st_shape],
        scratch_shapes=[pltpu.VMEM((rows, D_MODEL), F32),
                        pltpu.VMEM((rows, D_MODEL), BF16),
                        pltpu.VMEM((rows, D_MODEL), F32),
                        pltpu.VMEM((nb, tt + SUBLANE, FF_CHUNK), F32),
                        pltpu.VMEM((nb, tt + SUBLANE, FF_CHUNK), F32),
                        pltpu.VMEM((N_FF, nb, SUBLANE, FF_CHUNK), F32),
                        pltpu.VMEM((N_FF, nb, SUBLANE, FF_CHUNK), F32)],
        compiler_params=pltpu.CompilerParams(dimension_semantics=("arbitrary", "arbitrary"),
                                             vmem_limit_bytes=VMEM_LIMIT),
        name="mixer_out",
    )(x2d, a, b, c, *weights_a, sg, sv)


def _rope_tables(pos, reps):
    half = ROPE_DIM // 2
    inv = 1.0 / (ROPE_THETA ** (jnp.arange(0, ROPE_DIM, 2, dtype=F32) / ROPE_DIM))
    ang = pos.astype(F32)[:, None] * inv[None, :]
    cos, sin = jnp.cos(ang), jnp.sin(ang)
    n = pos.shape[0]
    zeros = jnp.zeros((n, HEAD_BLK - ROPE_DIM), F32)
    tkc = jnp.concatenate([cos, cos, zeros], axis=1)
    tks = jnp.concatenate([sin, sin, zeros], axis=1)
    tqc = jnp.concatenate([cos, cos, jnp.ones((n, NOPE_DIM), F32),
                           jnp.zeros((n, HEAD_BLK - ROPE_DIM - NOPE_DIM), F32)], axis=1) * ATTN_SCALE
    tqs = tks * ATTN_SCALE
    tqc = jnp.concatenate([tqc, tqc], axis=1)
    tqs = jnp.concatenate([tqs, tqs], axis=1)
    return tuple(jnp.tile(t, (reps, 1)) for t in (tqc, tqs, tkc, tks))


def _block_diag(w):
    n, d, e = w.shape
    return jnp.einsum('nde,nm->ndme', w, jnp.eye(n, dtype=w.dtype)).reshape(n * d, n * e)


def _prep_layer(l, p):
    half = ROPE_DIM // 2
    w = p['w_in'][l]
    u, v, cq, ckv, kr, xc, gate = jnp.split(w, [256, 512, 896, 1152, 1184, 1440], axis=1)
    kr1, kr2 = kr[:, :half], kr[:, half:]
    z = jnp.zeros((D_MODEL, HEAD_BLK - ROPE_DIM), F32)
    w_in = jnp.concatenate([u, v, cq, ckv, kr1, kr2, z, -kr2, kr1, z, xc, gate], axis=1)

    wq = p['mla_w_uq'][l].reshape(Q_LORA, N_B_HEADS, NOPE_DIM + ROPE_DIM)
    nope, r1, r2 = wq[..., :NOPE_DIM], wq[..., NOPE_DIM:NOPE_DIM + half], wq[..., NOPE_DIM + half:]
    zq = lambda k: jnp.zeros((Q_LORA, N_B_HEADS, k), F32)
    wq_blk = jnp.concatenate([r1, r2, nope, zq(HEAD_BLK - ROPE_DIM - NOPE_DIM)], axis=-1)
    wq_swp = jnp.concatenate([-r2, r1, zq(HEAD_BLK - ROPE_DIM)], axis=-1)

    w_uk, w_uv = p['mla_w_uk'][l], p['mla_w_uv'][l]
    zk = jnp.zeros((KV_LORA, N_B_HEADS, ROPE_DIM), F32)
    wk_pad = jnp.concatenate([zk, w_uk, zk], axis=-1)
    zt = jnp.zeros((N_B_HEADS, ROPE_DIM, KV_LORA), F32)
    wkt = jnp.concatenate([zt, jnp.transpose(w_uk, (1, 2, 0)), zt], axis=1)
    wv_pad = jnp.einsum('chd,hg->hcgd', w_uv, jnp.eye(N_B_HEADS, dtype=F32))

    w_o = p['w_o'][l]
    up = p['ffn_w_up'][l]
    fcw, fcb = p['ffn_conv_w'][l], p['ffn_conv_b'][l]

    def ff_cols(m):
        return jnp.moveaxis(m.reshape(m.shape[:-1] + (N_FF, FF_CHUNK)), -2, 0)

    def conv_w(m):
        return jnp.pad(ff_cols(m), ((0, 0), (0, SUBLANE - FFN_CONV), (0, 0)))

    row = lambda a: a.reshape(1, -1)
    return dict(
        w_in=w_in.astype(BF16),
        wq_blk=wq_blk.reshape(Q_LORA, -1).astype(BF16), wq_swp=wq_swp.reshape(Q_LORA, -1).astype(BF16),
        wk_pad=wk_pad.reshape(KV_LORA, -1).astype(BF16), w_uv=w_uv.reshape(KV_LORA, -1).astype(BF16),
        wkt=wkt.astype(BF16), wv_pad=wv_pad.reshape(N_B_HEADS, KV_LORA, D_B).astype(BF16),
        w_r=_block_diag(p['lru_w_r'][l]).astype(BF16), w_i=_block_diag(p['lru_w_i'][l]).astype(BF16),
        q_g=row(p['mla_q_norm_g'][l]), kv_g=row(p['mla_kv_norm_g'][l]),
        conv_w=p['lru_conv_w'][l], conv_b=row(p['lru_conv_b'][l]),
        b_r=row(p['lru_b_r'][l]), b_i=row(p['lru_b_i'][l]), lam=row(p['lru_lam'][l]),
        w_oa=w_o[:D_A].astype(BF16), w_ob=w_o[D_A:D_A + D_B].astype(BF16), w_oc=w_o[D_A + D_B:].astype(BF16),
        ln1_g=row(p['ln1_g'][l]), ln1_b=row(p['ln1_b'][l]), ln2_g=row(p['ln2_g'][l]), ln2_b=row(p['ln2_b'][l]),
        w_up_g=ff_cols(up[:, :D_FF]).astype(BF16), w_up_v=ff_cols(up[:, D_FF:]).astype(BF16),
        fcw_g=conv_w(fcw[:, :D_FF]), fcw_v=conv_w(fcw[:, D_FF:]),
        fcb_g=ff_cols(fcb[:D_FF].reshape(1, D_FF)), fcb_v=ff_cols(fcb[D_FF:].reshape(1, D_FF)),
        w_down=p['ffn_w_down'][l].reshape(N_FF, FF_CHUNK, D_MODEL).astype(BF16),
    )


def _gmlp_params(p, l, ln):
    w_s = p['gmlp_w_s'][l][:, :ln, :ln]
    b_s = jnp.repeat(p['gmlp_b_s'][l][:, :ln].T, A_HEAD, axis=1)
    return w_s, b_s


def _ffn_state_in(st):
    def one(m):
        m = jnp.moveaxis(m.reshape(m.shape[0], FFN_CONV - 1, N_FF, FF_CHUNK), 2, 0)
        return jnp.pad(m, ((0, 0), (0, 0), (SUBLANE - (FFN_CONV - 1), 0), (0, 0)))
    return one(st[..., :D_FF]), one(st[..., D_FF:])


def _ffn_state_out(og, ov):
    def one(m):
        m = jnp.moveaxis(m[:, :, SUBLANE - (FFN_CONV - 1):, :], 0, 2)
        return m.reshape(m.shape[0], FFN_CONV - 1, D_FF)
    return jnp.concatenate([one(og), one(ov)], axis=-1)


PROMPT_TT = 256
PROMPT_TQ = 256
PROMPT_TK = 256
SAMPLE_NB = 8
SAMPLE_TK = 1024


def kernel(x_prompt, x_sample, cache_kv_latent, cache_k_rope, state_lru_h, state_lru_conv, state_ffn_conv,
           ln1_g, ln1_b, ln2_g, ln2_b, w_in, w_o, gmlp_w_s, gmlp_b_s, mla_q_norm_g, mla_w_uq,
           mla_kv_norm_g, mla_w_uk, mla_w_uv, lru_conv_w, lru_conv_b, lru_w_r, lru_b_r, lru_w_i, lru_b_i,
           lru_lam, ffn_w_up, ffn_conv_w, ffn_conv_b, ffn_w_down):
    p = dict(ln1_g=ln1_g, ln1_b=ln1_b, ln2_g=ln2_g, ln2_b=ln2_b, w_in=w_in, w_o=w_o, gmlp_w_s=gmlp_w_s,
             gmlp_b_s=gmlp_b_s, mla_q_norm_g=mla_q_norm_g, mla_w_uq=mla_w_uq, mla_kv_norm_g=mla_kv_norm_g,
             mla_w_uk=mla_w_uk, mla_w_uv=mla_w_uv, lru_conv_w=lru_conv_w, lru_conv_b=lru_conv_b,
             lru_w_r=lru_w_r, lru_b_r=lru_b_r, lru_w_i=lru_w_i, lru_b_i=lru_b_i, lru_lam=lru_lam,
             ffn_w_up=ffn_w_up, ffn_conv_w=ffn_conv_w, ffn_conv_b=ffn_conv_b, ffn_w_down=ffn_w_down)
    bp, s_len, _ = x_prompt.shape
    bd, t_len, _ = x_sample.shape
    past = cache_kv_latent.shape[2]
    depth = w_in.shape[0]
    ln_p, ln_d = min(s_len, GMLP_CHUNK), min(t_len, GMLP_CHUNK)

    tabs_p = _rope_tables(jnp.arange(s_len), 1)
    tabs_d = _rope_tables(past + jnp.arange(t_len), SAMPLE_NB)
    zero_rows = jnp.zeros((bp, SUBLANE, D_C), F32)
    zero_ffn = jnp.zeros((N_FF, bp, SUBLANE, FF_CHUNK), F32)

    xp = x_prompt.reshape(bp * s_len, D_MODEL)
    xd = x_sample.reshape(bd * t_len, D_MODEL)
    outs = [[] for _ in range(11)]
    for l in range(depth):
        lw = _prep_layer(l, p)

        lw['w_s'], lw['b_s'] = _gmlp_params(p, l, ln_p)
        a, c, q, lat, kr, hst, cst, k, v = _mixer_in(
            xp, lw, tabs_p, zero_rows, zero_rows,
            n_seq=bp, seq_len=s_len, nb=1, tt=PROMPT_TT, ln=ln_p, prompt=True)
        b = _attn_prompt(q, k, v, n_seq=bp, seq_len=s_len, tq=PROMPT_TQ, tk=PROMPT_TK)
        xp, og, ov = _mixer_out(xp, a, b, c, lw, zero_ffn, zero_ffn, n_seq=bp, seq_len=s_len, nb=1, tt=PROMPT_TT)
        outs[0].append(lat.reshape(bp, s_len, KV_LORA))
        outs[1].append(kr.reshape(bp, s_len, ROPE_DIM))
        outs[2].append(hst[:, 0, :])
        outs[3].append(cst[:, SUBLANE - (LRU_CONV - 1):, :])
        outs[4].append(_ffn_state_out(og, ov))

        lw['w_s'], lw['b_s'] = _gmlp_params(p, l, ln_d)
        h0 = jnp.broadcast_to(state_lru_h[l][:, None, :], (bd, SUBLANE, D_C))
        conv0 = jnp.pad(state_lru_conv[l], ((0, 0), (SUBLANE - (LRU_CONV - 1), 0), (0, 0)))
        sg, sv = _ffn_state_in(state_ffn_conv[l])
        a, c, q, lat, kr, hst, cst, vg = _mixer_in(
            xd, lw, tabs_d, h0, conv0,
            n_seq=bd, seq_len=t_len, nb=SAMPLE_NB, tt=t_len, ln=ln_d, prompt=False)
        b = _attn_sample(q, cache_kv_latent, cache_k_rope, lat, kr, lw['wkt'], lw['wv_pad'],
                         layer=l, n_seq=bd, t_new=t_len, tk=SAMPLE_TK)
        xd, og, ov = _mixer_out(xd, a, b, c, lw, sg, sv, n_seq=bd, seq_len=t_len, nb=SAMPLE_NB, tt=t_len)
        outs[5].append(lat.reshape(bd, t_len, KV_LORA))
        outs[6].append(kr.reshape(bd, t_len, ROPE_DIM))
        outs[7].append(vg.reshape(bd, t_len, D_A))
        outs[8].append(hst[:, 0, :])
        outs[9].append(cst[:, SUBLANE - (LRU_CONV - 1):, :])
        outs[10].append(_ffn_state_out(og, ov))

    return (xp.reshape(bp, s_len, D_MODEL), xd.reshape(bd, t_len, D_MODEL)) + tuple(jnp.stack(o) for o in outs)
```

```python
import functools
import math

import jax
import jax.numpy as jnp
from jax import lax
from jax.experimental import pallas as pl
from jax.experimental.pallas import tpu as pltpu

F32 = jnp.float32
BF16 = jnp.bfloat16

D_MODEL = 1024
DEPTH = 4
CHUNK = 64
GMLP_CHUNK = 128
D_A = 256
N_A_HEADS = 4
A_HEAD = 64
D_B = 512
N_B_HEADS = 8
V_DIM = 64
NOPE_DIM = 64
ROPE_DIM = 32
Q_LORA = 384
KV_LORA = 256
ROPE_THETA = 10000.0
ATTN_SCALE = (NOPE_DIM + ROPE_DIM) ** -0.5
D_C = 256
N_C_BLOCKS = 4
C_HEAD = 64
LRU_CONV = 4
LRU_C = 8.0
D_FF = 2816
FFN_CONV = 3
ALPHA = (2.0 * DEPTH) ** 0.25
LN_EPS = 1e-5
RMS_EPS = 1e-6

LANE = 128
SUBLANE = 8
HEAD_BLK = 128
VMEM_LIMIT = 56 * 1024 * 1024

C_U, C_V, C_CQ, C_CKV, C_KR, C_XC, C_GATE, C_END = 0, 256, 512, 896, 1152, 1408, 1664, 1920

FF_CHUNK = 256
ROW_BLK = 64
N_FF = D_FF // FF_CHUNK


def _dot(a, b):
    return jnp.dot(a, b, preferred_element_type=F32)


def _dot_nt(a, b):
    return lax.dot_general(a, b, (((1,), (1,)), ((), ())), preferred_element_type=F32)


def _rms_norm(x, g):
    ms = jnp.mean(x * x, axis=-1, keepdims=True)
    return x * lax.rsqrt(ms + RMS_EPS) * g


def _layer_norm(x, g, b):
    mu = jnp.mean(x, axis=-1, keepdims=True)
    xc = x - mu
    var = jnp.mean(xc * xc, axis=-1, keepdims=True)
    return xc * lax.rsqrt(var + LN_EPS) * g + b


def _mixer_in_kernel(nb, tt, ln, prompt,
                     x_ref, win_ref, wqb_ref, wqs_ref, wkp_ref, wuv_ref, wr_ref, wi_ref,
                     ws_ref, bs_ref, qg_ref, kvg_ref, cw_ref, cb_ref, br_ref, bi_ref, lam_ref,
                     tqc_ref, tqs_ref, tkc_ref, tks_ref, h0_ref, conv0_ref,
                     *rest):
    if prompt:
        (a_ref, c_ref, q_ref, lat_ref, kr_ref, hst_ref, cst_ref, k_ref, v_ref,
         xb_ref, cv_ref, hc_ref) = rest
        vg_ref = None
    else:
        (a_ref, c_ref, q_ref, lat_ref, kr_ref, hst_ref, cst_ref, vg_ref,
         xb_ref, cv_ref, hc_ref) = rest
        k_ref = v_ref = None
    rows = nb * tt
    t_idx = pl.program_id(1)

    @pl.when(t_idx == 0)
    def _():
        cv_ref[:, 0:SUBLANE, :] = conv0_ref[...]
        hc_ref[...] = h0_ref[...]

    xb_ref[...] = x_ref[...].astype(BF16)

    def proj(c0, c1):
        return _dot(xb_ref[...], win_ref[:, c0:c1])

    xc_in = proj(C_XC, C_GATE)
    z_gate = proj(C_GATE, C_END)
    z_u = proj(C_U, C_V)
    z_v = proj(C_V, C_CQ)
    z_cq = proj(C_CQ, C_CKV)
    z_ckv = proj(C_CKV, C_KR)
    kr2 = proj(C_KR, C_XC)

    xcs = []
    for s in range(nb):
        cv_ref[s, SUBLANE:SUBLANE + tt, :] = xc_in[s * tt:(s + 1) * tt]
        acc = cb_ref[...]
        for j in range(LRU_CONV):
            off = SUBLANE - (LRU_CONV - 1) + j
            acc = acc + cv_ref[s, off:off + tt, :] * cw_ref[j:j + 1, :]
        xcs.append(acc)
        tail = cv_ref[s, tt:tt + SUBLANE, :]
        cst_ref[s] = tail
        cv_ref[s, 0:SUBLANE, :] = tail
    xc = xcs[0] if nb == 1 else jnp.concatenate(xcs, axis=0)
    xcb = xc.astype(BF16)
    z_r = _dot(xcb, wr_ref[...])
    z_i = _dot(xcb, wi_ref[...])

    u = jax.nn.gelu(z_u)
    v = jax.nn.gelu(z_v)
    if vg_ref is not None:
        vg_ref[...] = v
    vb = v.astype(BF16)
    row_i = lax.broadcasted_iota(jnp.int32, (ln, ln), 0)
    col_i = lax.broadcasted_iota(jnp.int32, (ln, ln), 1)
    w_tril = [jnp.where(row_i >= col_i, ws_ref[h], 0.0).astype(BF16) for h in range(N_A_HEADS)]
    head_of_lane = lax.broadcasted_iota(jnp.int32, (ln, D_A), 1) // A_HEAD
    for c in range(rows // ln):
        r0 = c * ln
        vc = vb[r0:r0 + ln]
        gate = bs_ref[...]
        for h in range(N_A_HEADS):
            gate = gate + _dot(w_tril[h], jnp.where(head_of_lane == h, vc, jnp.zeros_like(vc)))
        a_ref[r0:r0 + ln, :] = (u[r0:r0 + ln] * gate).astype(BF16)

    cqn = _rms_norm(z_cq, qg_ref[...]).astype(BF16)
    tqc = tqc_ref[...]
    tqs = tqs_ref[...]
    for j in range(N_B_HEADS // 2):
        c0, c1 = 2 * j * HEAD_BLK, 2 * (j + 1) * HEAD_BLK
        if prompt:
            q_ref[c0:c1, :] = (_dot_nt(wqb_ref[c0:c1, :], cqn) * tqc
                               + _dot_nt(wqs_ref[c0:c1, :], cqn) * tqs).astype(BF16)
        else:
            q_ref[:, c0:c1] = (_dot(cqn, wqb_ref[:, c0:c1]) * tqc
                               + _dot(cqn, wqs_ref[:, c0:c1]) * tqs).astype(BF16)
    ckvn = _rms_norm(z_ckv, kvg_ref[...])
    lat_ref[...] = ckvn
    kr_rot = kr2[:, :HEAD_BLK] * tkc_ref[...] + kr2[:, HEAD_BLK:] * tks_ref[...]
    kr_ref[...] = kr_rot[:, :ROPE_DIM]
    if prompt:
        ckvn_b = ckvn.astype(BF16)
        for j in range(N_B_HEADS // 2):
            c0 = 2 * j * HEAD_BLK
            kp = _dot(ckvn_b, wkp_ref[:, c0:c0 + 2 * HEAD_BLK])
            k_ref[:, c0:c0 + HEAD_BLK] = (kp[:, :HEAD_BLK] + kr_rot).astype(BF16)
            k_ref[:, c0 + HEAD_BLK:c0 + 2 * HEAD_BLK] = (kp[:, HEAD_BLK:] + kr_rot).astype(BF16)
        v_t = _dot_nt(wuv_ref[...], ckvn_b).astype(BF16)
        for h in range(N_B_HEADS):
            v_ref[h * V_AUG:h * V_AUG + V_DIM, :] = v_t[h * V_DIM:(h + 1) * V_DIM]
            v_ref[h * V_AUG + V_DIM:(h + 1) * V_AUG, :] = jnp.ones((V_ONES, rows), BF16)

    r = jax.nn.sigmoid(z_r + br_ref[...])
    ig = jax.nn.sigmoid(z_i + bi_ref[...])
    log_a = (-LRU_C) * r * jax.nn.softplus(-lam_ref[...])
    a = jnp.exp(log_a)
    b_in = jnp.sqrt(-jnp.tanh(log_a) * (a * a + 1.0)) * (ig * xc)

    groups = rows // SUBLANE
    a3 = a.reshape(groups, SUBLANE, D_C)
    b3 = b_in.reshape(groups, SUBLANE, D_C)
    sub = lax.broadcasted_iota(jnp.int32, (groups, SUBLANE, D_C), 1)
    for k in (1, 2, 4):
        keep = sub >= k
        a_sh = jnp.where(keep, pltpu.roll(a3, k, 1), 1.0)
        b_sh = jnp.where(keep, pltpu.roll(b3, k, 1), 0.0)
        b3 = a3 * b_sh + b3
        a3 = a3 * a_sh
    gs = tt // SUBLANE
    h_rows = []
    for s in range(nb):
        hb = hc_ref[s]
        for g in range(s * gs, (s + 1) * gs):
            hr = a3[g] * hb + b3[g]
            h_rows.append(hr)
            hb = jnp.broadcast_to(hr[SUBLANE - 1:SUBLANE, :], (SUBLANE, D_C))
        hc_ref[s] = hb
        hst_ref[s] = hb
    c_ref[...] = (jnp.concatenate(h_rows, axis=0) * jax.nn.gelu(z_gate)).astype(BF16)


def _const_spec(shape):
    nd = len(shape)
    return pl.BlockSpec(shape, lambda *_: (0,) * nd)


def _mixer_in(x2d, lw, tabs, h0, conv0, *, n_seq, seq_len, nb, tt, ln, prompt):
    nt = seq_len // tt
    assert nb == 1 or nt == 1
    rows = nb * tt
    n_rows = n_seq * seq_len
    grid = (n_seq // nb, nt)
    row_map = lambda b, t: (b * nt + t, 0)
    tab_map = lambda b, t: (t, 0)
    seq_map = lambda b, t: (b, 0, 0)

    sfx = '_t' if prompt else ''
    weights = [lw['w_in'], lw['wq_blk' + sfx], lw['wq_swp' + sfx], lw['wk_pad'], lw['w_uv' + sfx], lw['w_r'], lw['w_i'],
               lw['w_s'], lw['b_s'], lw['q_g'], lw['kv_g'], lw['conv_w'], lw['conv_b'], lw['b_r'],
               lw['b_i'], lw['lam']]
    col_map = lambda b, t: (0, b * nt + t)
    if prompt:
        q_tab_spec = pl.BlockSpec((2 * HEAD_BLK, rows), lambda b, t: (0, t))
        q_shape = jax.ShapeDtypeStruct((N_B_HEADS * HEAD_BLK, n_rows), BF16)
        q_spec = pl.BlockSpec((N_B_HEADS * HEAD_BLK, rows), col_map)
    else:
        q_tab_spec = pl.BlockSpec((rows, 2 * HEAD_BLK), tab_map)
        q_shape = jax.ShapeDtypeStruct((n_rows, N_B_HEADS * HEAD_BLK), BF16)
        q_spec = pl.BlockSpec((rows, N_B_HEADS * HEAD_BLK), row_map)
    in_specs = ([pl.BlockSpec((rows, D_MODEL), row_map)]
                + [_const_spec(w.shape) for w in weights]
                + [q_tab_spec, q_tab_spec,
                   pl.BlockSpec((rows, HEAD_BLK), tab_map), pl.BlockSpec((rows, HEAD_BLK), tab_map),
                   pl.BlockSpec((nb, SUBLANE, D_C), seq_map), pl.BlockSpec((nb, SUBLANE, D_C), seq_map)])
    out_shape = [jax.ShapeDtypeStruct((n_rows, D_A), BF16),
                 jax.ShapeDtypeStruct((n_rows, D_C), BF16),
                 q_shape,
                 jax.ShapeDtypeStruct((n_rows, KV_LORA), F32),
                 jax.ShapeDtypeStruct((n_rows, ROPE_DIM), F32),
                 jax.ShapeDtypeStruct((n_seq, SUBLANE, D_C), F32),
                 jax.ShapeDtypeStruct((n_seq, SUBLANE, D_C), F32)]
    out_specs = [pl.BlockSpec((rows, D_A), row_map), pl.BlockSpec((rows, D_C), row_map),
                 q_spec, pl.BlockSpec((rows, KV_LORA), row_map),
                 pl.BlockSpec((rows, ROPE_DIM), row_map),
                 pl.BlockSpec((nb, SUBLANE, D_C), seq_map), pl.BlockSpec((nb, SUBLANE, D_C), seq_map)]
    if prompt:
        out_shape += [jax.ShapeDtypeStruct((n_rows, N_B_HEADS * HEAD_BLK), BF16),
                      jax.ShapeDtypeStruct((N_B_HEADS * V_AUG, n_rows), BF16)]
        out_specs += [pl.BlockSpec((rows, N_B_HEADS * HEAD_BLK), row_map),
                      pl.BlockSpec((N_B_HEADS * V_AUG, rows), col_map)]
    else:
        out_shape += [jax.ShapeDtypeStruct((n_rows, D_A), F32)]
        out_specs += [pl.BlockSpec((rows, D_A), row_map)]
    scratch = [pltpu.VMEM((rows, D_MODEL), BF16),
               pltpu.VMEM((nb, tt + SUBLANE, D_C), F32),
               pltpu.VMEM((nb, SUBLANE, D_C), F32)]
    return pl.pallas_call(
        functools.partial(_mixer_in_kernel, nb, tt, ln, prompt),
        grid=grid, in_specs=in_specs, out_specs=out_specs, out_shape=out_shape,
        scratch_shapes=scratch,
        compiler_params=pltpu.CompilerParams(dimension_semantics=("arbitrary", "arbitrary"),
                                             vmem_limit_bytes=VMEM_LIMIT),
        name="mixer_in_prompt" if prompt else "mixer_in_sample",
    )(x2d, *weights, *tabs, h0, conv0)


NEG_BIG = -1e30
LOG2E = math.log2(math.e)
V_ONES = 16
V_AUG = V_DIM + V_ONES
ATTN_HEAD_GROUP = 4


def _last_kv_tile(qi, tq, tk):
    return ((qi + 1) * tq - 1) // tk


def _kv_steps(nq, tq, tk):
    pairs = [(i, j) for i in range(nq) for j in range(_last_kv_tile(i, tq, tk) + 1)]
    return (jnp.asarray([p[0] for p in pairs], jnp.int32), jnp.asarray([p[1] for p in pairs], jnp.int32))


def _attn_prompt_kernel(tq, tk, qi_tab, ki_tab, q_ref, k_ref, v_ref, o_ref, m_ref, acc_ref):
    step_idx = pl.program_id(1)
    qi = qi_tab[step_idx]
    ki = ki_tab[step_idx]
    last = _last_kv_tile(qi, tq, tk)

    @pl.when(ki == 0)
    def _():
        m_ref[...] = jnp.full(m_ref.shape, NEG_BIG, F32)
        acc_ref[...] = jnp.zeros(acc_ref.shape, F32)

    def step(masked):
        if masked:
            k_chunk = (ki * tk + lax.broadcasted_iota(jnp.int32, (tk, tq), 0)) // CHUNK
            q_chunk = (qi * tq + lax.broadcasted_iota(jnp.int32, (tk, tq), 1)) // CHUNK
            visible = k_chunk <= q_chunk
        def scores(h):
            return _dot(k_ref[:, h * HEAD_BLK:(h + 1) * HEAD_BLK], q_ref[h * HEAD_BLK:(h + 1) * HEAD_BLK, :])

        def softmax(h, s):
            if masked:
                s = jnp.where(visible, s, NEG_BIG)
            m_prev = m_ref[h]
            m_new = jnp.maximum(m_prev, jnp.max(s, axis=0, keepdims=True))
            m_ref[h] = m_new
            return jnp.exp2(m_prev - m_new), jnp.exp2(s - m_new).astype(BF16)

        group = ATTN_HEAD_GROUP
        s_next = [scores(h) for h in range(group)]
        for h0 in range(0, N_B_HEADS, group):
            s_cur = s_next
            if h0 + group < N_B_HEADS:
                s_next = [scores(h) for h in range(h0 + group, h0 + 2 * group)]
            probs = [softmax(h0 + i, s_cur[i]) for i in range(group)]
            for i, (alpha, p) in enumerate(probs):
                rows = slice((h0 + i) * V_AUG, (h0 + i + 1) * V_AUG)
                acc_ref[rows, :] = alpha * acc_ref[rows, :] + _dot(v_ref[rows, :], p)

    needs_mask = (ki * tk + tk - 1) // CHUNK > (qi * tq) // CHUNK

    @pl.when(needs_mask)
    def _():
        step(True)

    @pl.when(jnp.logical_not(needs_mask))
    def _():
        step(False)

    @pl.when(ki == last)
    def _():
        for j in range(N_B_HEADS // 2):
            halves = []
            for h in (2 * j, 2 * j + 1):
                denom = acc_ref[h * V_AUG + V_DIM:h * V_AUG + V_DIM + 1, :]
                halves.append(acc_ref[h * V_AUG:h * V_AUG + V_DIM, :] / denom)
            o_ref[:, j * LANE:(j + 1) * LANE] = jnp.concatenate(halves, axis=0).T.astype(BF16)


def _attn_prompt(q_t, k, v_t, *, n_seq, seq_len, tq, tk):
    nq, nk = seq_len // tq, seq_len // tk
    qi_tab, ki_tab = _kv_steps(nq, tq, tk)
    grid_spec = pltpu.PrefetchScalarGridSpec(
        num_scalar_prefetch=2,
        grid=(n_seq, qi_tab.shape[0]),
        in_specs=[pl.BlockSpec((N_B_HEADS * HEAD_BLK, tq), lambda b, s, qt, kt: (0, b * nq + qt[s])),
                  pl.BlockSpec((tk, N_B_HEADS * HEAD_BLK), lambda b, s, qt, kt: (b * nk + kt[s], 0)),
                  pl.BlockSpec((N_B_HEADS * V_AUG, tk), lambda b, s, qt, kt: (0, b * nk + kt[s]))],
        out_specs=pl.BlockSpec((tq, D_B), lambda b, s, qt, kt: (b * nq + qt[s], 0)),
        scratch_shapes=[pltpu.VMEM((N_B_HEADS, 1, tq), F32),
                        pltpu.VMEM((N_B_HEADS * V_AUG, tq), F32)])
    return pl.pallas_call(
        functools.partial(_attn_prompt_kernel, tq, tk),
        grid_spec=grid_spec,
        out_shape=jax.ShapeDtypeStruct((n_seq * seq_len, D_B), BF16),
        compiler_params=pltpu.CompilerParams(dimension_semantics=("arbitrary", "arbitrary"),
                                             vmem_limit_bytes=VMEM_LIMIT),
        name="attn_prompt",
    )(qi_tab, ki_tab, q_t, k, v_t)


def _attn_sample_kernel(t_new, layer_unused, q_ref, clat_ref, ckr_ref, nlat_ref, nkr_ref, wkt_ref, wv_ref,
                        o_ref, ql_ref, qr_ref, m_ref, l_ref, acc_ref):
    ki = pl.program_id(1)
    nt = (((1,), (1,)), ((), ()))
    rows = N_B_HEADS * t_new

    def attend(c_b, kr_b):
        width = c_b.shape[0]
        s = (lax.dot_general(ql_ref[...], c_b, nt, preferred_element_type=F32)
             + lax.dot_general(qr_ref[...], kr_b, nt, preferred_element_type=F32))
        m_prev = m_ref[...]
        m_new = jnp.maximum(m_prev, jnp.max(s, axis=1, keepdims=True))
        alpha = jnp.exp(m_prev - m_new)
        if width % LANE == 0:
            m_b = jnp.concatenate([m_new] * (width // LANE), axis=1)
        else:
            m_b = m_new[:, :width]
        p = jnp.exp(s - m_b)
        l_ref[...] = alpha * l_ref[...] + jnp.sum(p, axis=1, keepdims=True)
        m_ref[...] = m_new
        acc_ref[...] = (jnp.concatenate([alpha, alpha], axis=1) * acc_ref[...]
                        + _dot(p.astype(BF16), c_b))

    @pl.when(ki == 0)
    def _():
        for h in range(N_B_HEADS):
            q_h = q_ref[:, h * HEAD_BLK:(h + 1) * HEAD_BLK]
            ql_ref[h * t_new:(h + 1) * t_new, :] = _dot(q_h, wkt_ref[h]).astype(BF16)
            qr_ref[h * t_new:(h + 1) * t_new, :] = q_h[:, :ROPE_DIM]
        m_ref[...] = jnp.full(m_ref.shape, NEG_BIG, F32)
        l_ref[...] = jnp.zeros(l_ref.shape, F32)
        acc_ref[...] = jnp.zeros(acc_ref.shape, F32)
        attend(nlat_ref[...].astype(BF16), nkr_ref[...].astype(BF16))

    attend(clat_ref[...].astype(BF16), ckr_ref[...].astype(BF16))

    @pl.when(ki == pl.num_programs(1) - 1)
    def _():
        l = l_ref[...]
        o_lat = (acc_ref[...] / jnp.concatenate([l, l], axis=1)).astype(BF16)
        out = _dot(o_lat[0:t_new], wv_ref[0])
        for h in range(1, N_B_HEADS):
            out = out + _dot(o_lat[h * t_new:(h + 1) * t_new], wv_ref[h])
        o_ref[...] = out.astype(BF16)


def _attn_sample(q, cache_lat, cache_kr, new_lat, new_kr, wkt, wv_pad, *, layer, n_seq, t_new, tk):
    past = cache_lat.shape[2]
    nk = past // tk
    row_map = lambda b, j: (b, 0)
    rows = N_B_HEADS * t_new
    return pl.pallas_call(
        functools.partial(_attn_sample_kernel, t_new, layer),
        grid=(n_seq, nk),
        in_specs=[pl.BlockSpec((t_new, N_B_HEADS * HEAD_BLK), row_map),
                  pl.BlockSpec((None, None, tk, KV_LORA), lambda b, j: (layer, b, j, 0)),
                  pl.BlockSpec((None, None, tk, ROPE_DIM), lambda b, j: (layer, b, j, 0)),
                  pl.BlockSpec((t_new, KV_LORA), row_map),
                  pl.BlockSpec((t_new, ROPE_DIM), row_map),
                  _const_spec(wkt.shape), _const_spec(wv_pad.shape)],
        out_specs=pl.BlockSpec((t_new, D_B), row_map),
        out_shape=jax.ShapeDtypeStruct((n_seq * t_new, D_B), BF16),
        scratch_shapes=[pltpu.VMEM((rows, KV_LORA), BF16),
                        pltpu.VMEM((rows, ROPE_DIM), BF16),
                        pltpu.VMEM((rows, LANE), F32),
                        pltpu.VMEM((rows, LANE), F32),
                        pltpu.VMEM((rows, KV_LORA), F32)],
        compiler_params=pltpu.CompilerParams(dimension_semantics=("arbitrary", "arbitrary"),
                                             vmem_limit_bytes=VMEM_LIMIT),
        name="attn_sample",
    )(q, cache_lat, cache_kr, new_lat, new_kr, wkt, wv_pad)


def _mixer_out_kernel(nb, tt,
                      x_ref, a_ref, b_ref, c_ref, woa_ref, wob_ref, woc_ref, g1_ref, b1_ref,
                      wug_ref, wuv_ref, cwg_ref, cwv_ref, cbg_ref, cbv_ref, wd_ref, g2_ref, b2_ref,
                      sg_ref, sv_ref,
                      y_ref, og_ref, ov_ref,
                      x1_ref, x1b_ref, acc_ref, bufg0_ref, bufg1_ref, bufv0_ref, bufv1_ref,
                      carg_ref, carv_ref, act0_ref, act1_ref):
    bufg_ref = (bufg0_ref, bufg1_ref)
    bufv_ref = (bufv0_ref, bufv1_ref)
    act_ref = (act0_ref, act1_ref)
    t_idx = pl.program_id(1)
    mix = _dot(a_ref[...], woa_ref[...]) + _dot(b_ref[...], wob_ref[...]) + _dot(c_ref[...], woc_ref[...])
    x1 = _layer_norm(ALPHA * x_ref[...] + mix, g1_ref[...], b1_ref[...])
    x1_ref[...] = x1
    x1b_ref[...] = x1.astype(BF16)
    acc_ref[...] = jnp.zeros(acc_ref.shape, F32)

    @pl.when(t_idx == 0)
    def _():
        carg_ref[...] = sg_ref[...]
        carv_ref[...] = sv_ref[...]

    def stage(up, buf_ref, car_ref, j):
        for s in range(nb):
            buf_ref[j % 2][s, 0:SUBLANE, :] = car_ref[j, s]
            buf_ref[j % 2][s, SUBLANE:SUBLANE + tt, :] = up[s * tt:(s + 1) * tt]

    def conv_rows(buf_ref, w, bias, j, s, r0, nr):
        acc = bias
        for i in range(FFN_CONV):
            off = SUBLANE - (FFN_CONV - 1) + i + r0
            acc = acc + buf_ref[j % 2][s, off:off + nr, :] * w[i:i + 1, :]
        return acc

    def save_tail(buf_ref, car_ref, out_ref, j):
        for s in range(nb):
            tail = buf_ref[j % 2][s, tt:tt + SUBLANE, :]
            car_ref[j, s] = tail
            out_ref[j, s] = tail

    def up_proj(j):
        xb = x1b_ref[...]
        stage(_dot(xb, wug_ref[j]), bufg_ref, carg_ref, j)
        stage(_dot(xb, wuv_ref[j]), bufv_ref, carv_ref, j)

    def elementwise(j):
        wg, wv, bg, bv = cwg_ref[j], cwv_ref[j], cbg_ref[j], cbv_ref[j]
        for s in range(nb):
            for r0 in range(0, tt, ROW_BLK):
                nr = min(ROW_BLK, tt - r0)
                gate = conv_rows(bufg_ref, wg, bg, j, s, r0, nr)
                val = conv_rows(bufv_ref, wv, bv, j, s, r0, nr)
                act_ref[j % 2][s * tt + r0:s * tt + r0 + nr, :] = (jax.nn.gelu(gate) * val).astype(BF16)
        save_tail(bufg_ref, carg_ref, og_ref, j)
        save_tail(bufv_ref, carv_ref, ov_ref, j)

    up_proj(0)
    for j in range(N_FF + 1):
        if j + 1 < N_FF:
            up_proj(j + 1)
        if j >= 1:
            acc_ref[...] += _dot(act_ref[(j - 1) % 2][...], wd_ref[j - 1])
        if j < N_FF:
            elementwise(j)
    y_ref[...] = _layer_norm(ALPHA * x1_ref[...] + acc_ref[...], g2_ref[...], b2_ref[...])


def _mixer_out(x2d, a, b, c, lw, sg, sv, *, n_seq, seq_len, nb, tt):
    nt = seq_len // tt
    assert nb == 1 or nt == 1
    rows = nb * tt
    n_rows = n_seq * seq_len
    row_map = lambda bi, t: (bi * nt + t, 0)
    st_map = lambda bi, t: (0, bi, 0, 0)
    weights_a = [lw['w_oa'], lw['w_ob'], lw['w_oc'], lw['ln1_g'], lw['ln1_b'],
                 lw['w_up_g'], lw['w_up_v'], lw['fcw_g'], lw['fcw_v'], lw['fcb_g'], lw['fcb_v'],
                 lw['w_down'], lw['ln2_g'], lw['ln2_b']]
    st_spec = pl.BlockSpec((N_FF, nb, SUBLANE, FF_CHUNK), st_map)
    in_specs = ([pl.BlockSpec((rows, D_MODEL), row_map), pl.BlockSpec((rows, D_A), row_map),
                 pl.BlockSpec((rows, D_B), row_map), pl.BlockSpec((rows, D_C), row_map)]
                + [pl.BlockSpec(w.shape, functools.partial(lambda nd, *_: (0,) * nd, len(w.shape)),
                                pipeline_mode=pl.Buffered(1)) for w in weights_a]
                + [st_spec, st_spec])
    st_shape = jax.ShapeDtypeStruct((N_FF, n_seq, SUBLANE, FF_CHUNK), F32)
    return pl.pallas_call(
        functools.partial(_mixer_out_kernel, nb, tt),
        grid=(n_seq // nb, nt),
        in_specs=in_specs,
        out_specs=[pl.BlockSpec((rows, D_MODEL), row_map), st_spec, st_spec],
        out_shape=[jax.ShapeDtypeStruct((n_rows, D_MODEL), F32), st_shape, st_shape],
        scratch_shapes=[pltpu.VMEM((rows, D_MODEL), F32),
                        pltpu.VMEM((rows, D_MODEL), BF16),
                        pltpu.VMEM((rows, D_MODEL), F32),
                        pltpu.VMEM((nb, tt + SUBLANE, FF_CHUNK), F32),
                        pltpu.VMEM((nb, tt + SUBLANE, FF_CHUNK), F32),
                        pltpu.VMEM((nb, tt + SUBLANE, FF_CHUNK), F32),
                        pltpu.VMEM((nb, tt + SUBLANE, FF_CHUNK), F32),
                        pltpu.VMEM((N_FF, nb, SUBLANE, FF_CHUNK), F32),
                        pltpu.VMEM((N_FF, nb, SUBLANE, FF_CHUNK), F32),
                        pltpu.VMEM((rows, FF_CHUNK), BF16),
                        pltpu.VMEM((rows, FF_CHUNK), BF16)],
        compiler_params=pltpu.CompilerParams(dimension_semantics=("arbitrary", "arbitrary"),
                                             vmem_limit_bytes=VMEM_LIMIT),
        name="mixer_out",
    )(x2d, a, b, c, *weights_a, sg, sv)


def _rope_tables(pos, reps, feature_major_q):
    half = ROPE_DIM // 2
    inv = 1.0 / (ROPE_THETA ** (jnp.arange(0, ROPE_DIM, 2, dtype=F32) / ROPE_DIM))
    ang = pos.astype(F32)[:, None] * inv[None, :]
    cos, sin = jnp.cos(ang), jnp.sin(ang)
    n = pos.shape[0]
    zeros = jnp.zeros((n, HEAD_BLK - ROPE_DIM), F32)
    tkc = jnp.concatenate([cos, cos, zeros], axis=1)
    tks = jnp.concatenate([sin, sin, zeros], axis=1)
    tqc = jnp.concatenate([cos, cos, jnp.ones((n, NOPE_DIM), F32),
                           jnp.zeros((n, HEAD_BLK - ROPE_DIM - NOPE_DIM), F32)], axis=1) * ATTN_SCALE
    tqs = tks * ATTN_SCALE
    tqc = jnp.concatenate([tqc, tqc], axis=1)
    tqs = jnp.concatenate([tqs, tqs], axis=1)
    if feature_major_q:
        return ((tqc * LOG2E).T, (tqs * LOG2E).T, tkc, tks)
    return tuple(jnp.tile(t, (reps, 1)) for t in (tqc, tqs, tkc, tks))


def _block_diag(w):
    n, d, e = w.shape
    return jnp.einsum('nde,nm->ndme', w, jnp.eye(n, dtype=w.dtype)).reshape(n * d, n * e)


def _prep_layer(l, p):
    half = ROPE_DIM // 2
    w = p['w_in'][l]
    u, v, cq, ckv, kr, xc, gate = jnp.split(w, [256, 512, 896, 1152, 1184, 1440], axis=1)
    kr1, kr2 = kr[:, :half], kr[:, half:]
    z = jnp.zeros((D_MODEL, HEAD_BLK - ROPE_DIM), F32)
    w_in = jnp.concatenate([u, v, cq, ckv, kr1, kr2, z, -kr2, kr1, z, xc, gate], axis=1)

    wq = p['mla_w_uq'][l].reshape(Q_LORA, N_B_HEADS, NOPE_DIM + ROPE_DIM)
    nope, r1, r2 = wq[..., :NOPE_DIM], wq[..., NOPE_DIM:NOPE_DIM + half], wq[..., NOPE_DIM + half:]
    zq = lambda k: jnp.zeros((Q_LORA, N_B_HEADS, k), F32)
    wq_blk = jnp.concatenate([r1, r2, nope, zq(HEAD_BLK - ROPE_DIM - NOPE_DIM)], axis=-1)
    wq_swp = jnp.concatenate([-r2, r1, zq(HEAD_BLK - ROPE_DIM)], axis=-1)

    w_uk, w_uv = p['mla_w_uk'][l], p['mla_w_uv'][l]
    zk = jnp.zeros((KV_LORA, N_B_HEADS, ROPE_DIM), F32)
    wk_pad = jnp.concatenate([zk, w_uk, zk], axis=-1)
    zt = jnp.zeros((N_B_HEADS, ROPE_DIM, KV_LORA), F32)
    wkt = jnp.concatenate([zt, jnp.transpose(w_uk, (1, 2, 0)), zt], axis=1)
    wv_pad = jnp.einsum('chd,hg->hcgd', w_uv, jnp.eye(N_B_HEADS, dtype=F32))

    w_o = p['w_o'][l]
    up = p['ffn_w_up'][l]
    fcw, fcb = p['ffn_conv_w'][l], p['ffn_conv_b'][l]

    def ff_cols(m):
        return jnp.moveaxis(m.reshape(m.shape[:-1] + (N_FF, FF_CHUNK)), -2, 0)

    def conv_w(m):
        return jnp.pad(ff_cols(m), ((0, 0), (0, SUBLANE - FFN_CONV), (0, 0)))

    row = lambda a: a.reshape(1, -1)
    return dict(
        w_in=w_in.astype(BF16),
        wq_blk=wq_blk.reshape(Q_LORA, -1).astype(BF16), wq_swp=wq_swp.reshape(Q_LORA, -1).astype(BF16),
        wq_blk_t=wq_blk.reshape(Q_LORA, -1).T.astype(BF16), wq_swp_t=wq_swp.reshape(Q_LORA, -1).T.astype(BF16),
        wk_pad=wk_pad.reshape(KV_LORA, -1).astype(BF16), w_uv=w_uv.reshape(KV_LORA, -1).astype(BF16),
        w_uv_t=w_uv.reshape(KV_LORA, -1).T.astype(BF16),
        wkt=wkt.astype(BF16), wv_pad=wv_pad.reshape(N_B_HEADS, KV_LORA, D_B).astype(BF16),
        w_r=_block_diag(p['lru_w_r'][l]).astype(BF16), w_i=_block_diag(p['lru_w_i'][l]).astype(BF16),
        q_g=row(p['mla_q_norm_g'][l]), kv_g=row(p['mla_kv_norm_g'][l]),
        conv_w=p['lru_conv_w'][l], conv_b=row(p['lru_conv_b'][l]),
        b_r=row(p['lru_b_r'][l]), b_i=row(p['lru_b_i'][l]), lam=row(p['lru_lam'][l]),
        w_oa=w_o[:D_A].astype(BF16), w_ob=w_o[D_A:D_A + D_B].astype(BF16), w_oc=w_o[D_A + D_B:].astype(BF16),
        ln1_g=row(p['ln1_g'][l]), ln1_b=row(p['ln1_b'][l]), ln2_g=row(p['ln2_g'][l]), ln2_b=row(p['ln2_b'][l]),
        w_up_g=ff_cols(up[:, :D_FF]).astype(BF16), w_up_v=ff_cols(up[:, D_FF:]).astype(BF16),
        fcw_g=conv_w(fcw[:, :D_FF]), fcw_v=conv_w(fcw[:, D_FF:]),
        fcb_g=ff_cols(fcb[:D_FF].reshape(1, D_FF)), fcb_v=ff_cols(fcb[D_FF:].reshape(1, D_FF)),
        w_down=p['ffn_w_down'][l].reshape(N_FF, FF_CHUNK, D_MODEL).astype(BF16),
    )


def _gmlp_params(p, l, ln):
    w_s = p['gmlp_w_s'][l][:, :ln, :ln]
    b_s = jnp.repeat(p['gmlp_b_s'][l][:, :ln].T, A_HEAD, axis=1)
    return w_s, b_s


def _ffn_state_in(st):
    def one(m):
        m = jnp.moveaxis(m.reshape(m.shape[0], FFN_CONV - 1, N_FF, FF_CHUNK), 2, 0)
        return jnp.pad(m, ((0, 0), (0, 0), (SUBLANE - (FFN_CONV - 1), 0), (0, 0)))
    return one(st[..., :D_FF]), one(st[..., D_FF:])


def _ffn_state_out(og, ov):
    def one(m):
        m = jnp.moveaxis(m[:, :, SUBLANE - (FFN_CONV - 1):, :], 0, 2)
        return m.reshape(m.shape[0], FFN_CONV - 1, D_FF)
    return jnp.concatenate([one(og), one(ov)], axis=-1)


PROMPT_TT_IN = 512
PROMPT_TT_OUT = 256
PROMPT_TQ = 512
PROMPT_TK = 512
SAMPLE_NB = 8
SAMPLE_TK = 1024


def kernel(x_prompt, x_sample, cache_kv_latent, cache_k_rope, state_lru_h, state_lru_conv, state_ffn_conv,
           ln1_g, ln1_b, ln2_g, ln2_b, w_in, w_o, gmlp_w_s, gmlp_b_s, mla_q_norm_g, mla_w_uq,
           mla_kv_norm_g, mla_w_uk, mla_w_uv, lru_conv_w, lru_conv_b, lru_w_r, lru_b_r, lru_w_i, lru_b_i,
           lru_lam, ffn_w_up, ffn_conv_w, ffn_conv_b, ffn_w_down):
    p = dict(ln1_g=ln1_g, ln1_b=ln1_b, ln2_g=ln2_g, ln2_b=ln2_b, w_in=w_in, w_o=w_o, gmlp_w_s=gmlp_w_s,
             gmlp_b_s=gmlp_b_s, mla_q_norm_g=mla_q_norm_g, mla_w_uq=mla_w_uq, mla_kv_norm_g=mla_kv_norm_g,
             mla_w_uk=mla_w_uk, mla_w_uv=mla_w_uv, lru_conv_w=lru_conv_w, lru_conv_b=lru_conv_b,
             lru_w_r=lru_w_r, lru_b_r=lru_b_r, lru_w_i=lru_w_i, lru_b_i=lru_b_i, lru_lam=lru_lam,
             ffn_w_up=ffn_w_up, ffn_conv_w=ffn_conv_w, ffn_conv_b=ffn_conv_b, ffn_w_down=ffn_w_down)
    bp, s_len, _ = x_prompt.shape
    bd, t_len, _ = x_sample.shape
    past = cache_kv_latent.shape[2]
    depth = w_in.shape[0]
    ln_p, ln_d = min(s_len, GMLP_CHUNK), min(t_len, GMLP_CHUNK)

    tabs_p = _rope_tables(jnp.arange(s_len), 1, True)
    tabs_d = _rope_tables(past + jnp.arange(t_len), SAMPLE_NB, False)
    zero_rows = jnp.zeros((bp, SUBLANE, D_C), F32)
    zero_ffn = jnp.zeros((N_FF, bp, SUBLANE, FF_CHUNK), F32)

    xp = x_prompt.reshape(bp * s_len, D_MODEL)
    xd = x_sample.reshape(bd * t_len, D_MODEL)
    outs = [[] for _ in range(11)]
    for l in range(depth):
        lw = _prep_layer(l, p)

        lw['w_s'], lw['b_s'] = _gmlp_params(p, l, ln_p)
        a, c, q, lat, kr, hst, cst, k, v = _mixer_in(
            xp, lw, tabs_p, zero_rows, zero_rows,
            n_seq=bp, seq_len=s_len, nb=1, tt=PROMPT_TT_IN, ln=ln_p, prompt=True)
        b = _attn_prompt(q, k, v, n_seq=bp, seq_len=s_len, tq=PROMPT_TQ, tk=PROMPT_TK)
        xp, og, ov = _mixer_out(xp, a, b, c, lw, zero_ffn, zero_ffn, n_seq=bp, seq_len=s_len, nb=1, tt=PROMPT_TT_OUT)
        outs[0].append(lat.reshape(bp, s_len, KV_LORA))
        outs[1].append(kr.reshape(bp, s_len, ROPE_DIM))
        outs[2].append(hst[:, 0, :])
        outs[3].append(cst[:, SUBLANE - (LRU_CONV - 1):, :])
        outs[4].append(_ffn_state_out(og, ov))

        lw['w_s'], lw['b_s'] = _gmlp_params(p, l, ln_d)
        h0 = jnp.broadcast_to(state_lru_h[l][:, None, :], (bd, SUBLANE, D_C))
        conv0 = jnp.pad(state_lru_conv[l], ((0, 0), (SUBLANE - (LRU_CONV - 1), 0), (0, 0)))
        sg, sv = _ffn_state_in(state_ffn_conv[l])
        a, c, q, lat, kr, hst, cst, vg = _mixer_in(
            xd, lw, tabs_d, h0, conv0,
            n_seq=bd, seq_len=t_len, nb=SAMPLE_NB, tt=t_len, ln=ln_d, prompt=False)
        b = _attn_sample(q, cache_kv_latent, cache_k_rope, lat, kr, lw['wkt'], lw['wv_pad'],
                         layer=l, n_seq=bd, t_new=t_len, tk=SAMPLE_TK)
        xd, og, ov = _mixer_out(xd, a, b, c, lw, sg, sv, n_seq=bd, seq_len=t_len, nb=SAMPLE_NB, tt=t_len)
        outs[5].append(lat.reshape(bd, t_len, KV_LORA))
        outs[6].append(kr.reshape(bd, t_len, ROPE_DIM))
        outs[7].append(vg.reshape(bd, t_len, D_A))
        outs[8].append(hst[:, 0, :])
        outs[9].append(cst[:, SUBLANE - (LRU_CONV - 1):, :])
        outs[10].append(_ffn_state_out(og, ov))

    return (xp.reshape(bp, s_len, D_MODEL), xd.reshape(bd, t_len, D_MODEL)) + tuple(jnp.stack(o) for o in outs)
```

```python
import functools
import math

import jax
import jax.numpy as jnp
from jax import lax
from jax.experimental import pallas as pl
from jax.experimental.pallas import tpu as pltpu

F32 = jnp.float32
BF16 = jnp.bfloat16

D_MODEL = 1024
DEPTH = 4
CHUNK = 64
GMLP_CHUNK = 128
D_A = 256
N_A_HEADS = 4
A_HEAD = 64
D_B = 512
N_B_HEADS = 8
V_DIM = 64
NOPE_DIM = 64
ROPE_DIM = 32
Q_LORA = 384
KV_LORA = 256
ROPE_THETA = 10000.0
ATTN_SCALE = (NOPE_DIM + ROPE_DIM) ** -0.5
D_C = 256
N_C_BLOCKS = 4
C_HEAD = 64
LRU_CONV = 4
LRU_C = 8.0
D_FF = 2816
FFN_CONV = 3
ALPHA = (2.0 * DEPTH) ** 0.25
LN_EPS = 1e-5
RMS_EPS = 1e-6

LANE = 128
SUBLANE = 8
HEAD_BLK = 128
VMEM_LIMIT = 56 * 1024 * 1024

C_U, C_V, C_CQ, C_CKV, C_KR, C_XC, C_GATE, C_END = 0, 256, 512, 896, 1152, 1408, 1664, 1920

FF_CHUNK = 256
ROW_BLK = 64
DOWN_GROUP = 3
N_FF = D_FF // FF_CHUNK


def _dot(a, b):
    return jnp.dot(a, b, preferred_element_type=F32)


def _dot_nt(a, b):
    return lax.dot_general(a, b, (((1,), (1,)), ((), ())), preferred_element_type=F32)


def _rms_norm(x, g):
    ms = jnp.mean(x * x, axis=-1, keepdims=True)
    return x * lax.rsqrt(ms + RMS_EPS) * g


def _layer_norm(x, g, b):
    mu = jnp.mean(x, axis=-1, keepdims=True)
    xc = x - mu
    var = jnp.mean(xc * xc, axis=-1, keepdims=True)
    return xc * lax.rsqrt(var + LN_EPS) * g + b


def _mixer_in_kernel(nb, tt, ln, prompt, n_aliased,
                     x_ref, win_ref, wqb_ref, wqs_ref, wkp_ref, wuv_ref, wr_ref, wi_ref,
                     ws_ref, bs_ref, qg_ref, kvg_ref, cw_ref, cb_ref, br_ref, bi_ref, lam_ref,
                     tqc_ref, tqs_ref, tkc_ref, tks_ref, h0_ref, conv0_ref,
                     *rest):
    rest = rest[n_aliased:]
    if prompt:
        (a_ref, c_ref, q_ref, lat_ref, kr_ref, hst_ref, cst_ref, k_ref, v_ref,
         xb_ref, cv_ref, hc_ref) = rest
        vg_ref = None
    else:
        (a_ref, c_ref, q_ref, lat_ref, kr_ref, hst_ref, cst_ref, vg_ref,
         xb_ref, cv_ref, hc_ref) = rest
        k_ref = v_ref = None
    rows = nb * tt
    t_idx = pl.program_id(1)

    @pl.when(t_idx == 0)
    def _():
        cv_ref[:, 0:SUBLANE, :] = conv0_ref[...]
        hc_ref[...] = h0_ref[...]

    xb_ref[...] = x_ref[...].astype(BF16)

    def proj(c0, c1):
        return _dot(xb_ref[...], win_ref[:, c0:c1])

    xc_in = proj(C_XC, C_GATE)
    z_gate = proj(C_GATE, C_END)
    z_u = proj(C_U, C_V)
    z_v = proj(C_V, C_CQ)
    z_cq = proj(C_CQ, C_CKV)
    z_ckv = proj(C_CKV, C_KR)
    kr2 = proj(C_KR, C_XC)

    xcs = []
    for s in range(nb):
        cv_ref[s, SUBLANE:SUBLANE + tt, :] = xc_in[s * tt:(s + 1) * tt]
        acc = cb_ref[...]
        for j in range(LRU_CONV):
            off = SUBLANE - (LRU_CONV - 1) + j
            acc = acc + cv_ref[s, off:off + tt, :] * cw_ref[j:j + 1, :]
        xcs.append(acc)
        tail = cv_ref[s, tt:tt + SUBLANE, :]
        cst_ref[s] = tail
        cv_ref[s, 0:SUBLANE, :] = tail
    xc = xcs[0] if nb == 1 else jnp.concatenate(xcs, axis=0)
    xcb = xc.astype(BF16)
    z_r = _dot(xcb, wr_ref[...])
    z_i = _dot(xcb, wi_ref[...])

    u = jax.nn.gelu(z_u)
    v = jax.nn.gelu(z_v)
    if vg_ref is not None:
        vg_ref[...] = v
    vb = v.astype(BF16)
    row_i = lax.broadcasted_iota(jnp.int32, (ln, ln), 0)
    col_i = lax.broadcasted_iota(jnp.int32, (ln, ln), 1)
    w_tril = [jnp.where(row_i >= col_i, ws_ref[h], 0.0).astype(BF16) for h in range(N_A_HEADS)]
    head_of_lane = lax.broadcasted_iota(jnp.int32, (ln, D_A), 1) // A_HEAD
    for c in range(rows // ln):
        r0 = c * ln
        vc = vb[r0:r0 + ln]
        gate = bs_ref[...]
        for h in range(N_A_HEADS):
            gate = gate + _dot(w_tril[h], jnp.where(head_of_lane == h, vc, jnp.zeros_like(vc)))
        a_ref[r0:r0 + ln, :] = (u[r0:r0 + ln] * gate).astype(BF16)

    cqn = _rms_norm(z_cq, qg_ref[...]).astype(BF16)
    tqc = tqc_ref[...]
    tqs = tqs_ref[...]
    for j in range(N_B_HEADS // 2):
        c0, c1 = 2 * j * HEAD_BLK, 2 * (j + 1) * HEAD_BLK
        if prompt:
            q_ref[c0:c1, :] = (_dot_nt(wqb_ref[c0:c1, :], cqn) * tqc
                               + _dot_nt(wqs_ref[c0:c1, :], cqn) * tqs).astype(BF16)
        else:
            q_ref[:, c0:c1] = (_dot(cqn, wqb_ref[:, c0:c1]) * tqc
                               + _dot(cqn, wqs_ref[:, c0:c1]) * tqs).astype(BF16)
    ckvn = _rms_norm(z_ckv, kvg_ref[...])
    lat_ref[...] = ckvn
    kr_rot = kr2[:, :HEAD_BLK] * tkc_ref[...] + kr2[:, HEAD_BLK:] * tks_ref[...]
    kr_ref[...] = kr_rot[:, :ROPE_DIM]
    if prompt:
        ckvn_b = ckvn.astype(BF16)
        for j in range(N_B_HEADS // 2):
            c0 = 2 * j * HEAD_BLK
            kp = _dot(ckvn_b, wkp_ref[:, c0:c0 + 2 * HEAD_BLK])
            k_ref[:, c0:c0 + HEAD_BLK] = (kp[:, :HEAD_BLK] + kr_rot).astype(BF16)
            k_ref[:, c0 + HEAD_BLK:c0 + 2 * HEAD_BLK] = (kp[:, HEAD_BLK:] + kr_rot).astype(BF16)
        v_t = _dot_nt(wuv_ref[...], ckvn_b).astype(BF16)
        for h in range(N_B_HEADS):
            v_ref[h * V_AUG:h * V_AUG + V_DIM, :] = v_t[h * V_DIM:(h + 1) * V_DIM]
            v_ref[h * V_AUG + V_DIM:(h + 1) * V_AUG, :] = jnp.ones((V_ONES, rows), BF16)

    r = jax.nn.sigmoid(z_r + br_ref[...])
    ig = jax.nn.sigmoid(z_i + bi_ref[...])
    log_a = (-LRU_C) * r * jax.nn.softplus(-lam_ref[...])
    a = jnp.exp(log_a)
    b_in = jnp.sqrt(-jnp.tanh(log_a) * (a * a + 1.0)) * (ig * xc)

    groups = rows // SUBLANE
    a3 = a.reshape(groups, SUBLANE, D_C)
    b3 = b_in.reshape(groups, SUBLANE, D_C)
    sub = lax.broadcasted_iota(jnp.int32, (groups, SUBLANE, D_C), 1)
    for k in (1, 2, 4):
        keep = sub >= k
        a_sh = jnp.where(keep, pltpu.roll(a3, k, 1), 1.0)
        b_sh = jnp.where(keep, pltpu.roll(b3, k, 1), 0.0)
        b3 = a3 * b_sh + b3
        a3 = a3 * a_sh
    gs = tt // SUBLANE
    h_rows = []
    for s in range(nb):
        hb = hc_ref[s]
        for g in range(s * gs, (s + 1) * gs):
            hr = a3[g] * hb + b3[g]
            h_rows.append(hr)
            hb = jnp.broadcast_to(hr[SUBLANE - 1:SUBLANE, :], (SUBLANE, D_C))
        hc_ref[s] = hb
        hst_ref[s] = hb
    c_ref[...] = (jnp.concatenate(h_rows, axis=0) * jax.nn.gelu(z_gate)).astype(BF16)


def _layer_spec(w, layer, single_buffer=False):
    tail = (0,) * (w.ndim - 1)
    mode = dict(pipeline_mode=pl.Buffered(1)) if single_buffer else {}
    return pl.BlockSpec((None,) + w.shape[1:], lambda *_: (layer,) + tail, **mode)


def _mixer_in(x2d, lw, layer, tabs, h0, conv0, stacked, *, n_seq, seq_len, nb, tt, ln, prompt):
    depth = lw['w_in'].shape[0]
    nt = seq_len // tt
    assert nb == 1 or nt == 1
    rows = nb * tt
    n_rows = n_seq * seq_len
    grid = (n_seq // nb, nt)
    row_map = lambda b, t: (b * nt + t, 0)
    layer_row_map = lambda b, t: (layer, b * nt + t, 0)
    tab_map = lambda b, t: (t, 0)
    seq_map = lambda b, t: (b, 0, 0)

    sfx = '_t' if prompt else ''
    weights = [lw['w_in'], lw['wq_blk' + sfx], lw['wq_swp' + sfx], lw['wk_pad'], lw['w_uv' + sfx], lw['w_r'], lw['w_i'],
               lw['w_s'], lw['b_s'], lw['q_g'], lw['kv_g'], lw['conv_w'], lw['conv_b'], lw['b_r'],
               lw['b_i'], lw['lam']]
    col_map = lambda b, t: (0, b * nt + t)
    if prompt:
        q_tab_spec = pl.BlockSpec((2 * HEAD_BLK, rows), lambda b, t: (0, t))
        q_shape = jax.ShapeDtypeStruct((N_B_HEADS * HEAD_BLK, n_rows), BF16)
        q_spec = pl.BlockSpec((N_B_HEADS * HEAD_BLK, rows), col_map)
    else:
        q_tab_spec = pl.BlockSpec((rows, 2 * HEAD_BLK), tab_map)
        q_shape = jax.ShapeDtypeStruct((n_rows, N_B_HEADS * HEAD_BLK), BF16)
        q_spec = pl.BlockSpec((rows, N_B_HEADS * HEAD_BLK), row_map)
    in_specs = ([pl.BlockSpec((rows, D_MODEL), row_map)]
                + [_layer_spec(w, layer) for w in weights]
                + [q_tab_spec, q_tab_spec,
                   pl.BlockSpec((rows, HEAD_BLK), tab_map), pl.BlockSpec((rows, HEAD_BLK), tab_map),
                   pl.BlockSpec((nb, SUBLANE, D_C), seq_map), pl.BlockSpec((nb, SUBLANE, D_C), seq_map)])
    out_shape = [jax.ShapeDtypeStruct((n_rows, D_A), BF16),
                 jax.ShapeDtypeStruct((n_rows, D_C), BF16),
                 q_shape,
                 jax.ShapeDtypeStruct((depth, n_rows, KV_LORA), F32),
                 jax.ShapeDtypeStruct((depth, n_rows, ROPE_DIM), F32),
                 jax.ShapeDtypeStruct((n_seq, SUBLANE, D_C), F32),
                 jax.ShapeDtypeStruct((n_seq, SUBLANE, D_C), F32)]
    out_specs = [pl.BlockSpec((rows, D_A), row_map), pl.BlockSpec((rows, D_C), row_map),
                 q_spec, pl.BlockSpec((None, rows, KV_LORA), layer_row_map),
                 pl.BlockSpec((None, rows, ROPE_DIM), layer_row_map),
                 pl.BlockSpec((nb, SUBLANE, D_C), seq_map), pl.BlockSpec((nb, SUBLANE, D_C), seq_map)]
    if prompt:
        out_shape += [jax.ShapeDtypeStruct((n_rows, N_B_HEADS * HEAD_BLK), BF16),
                      jax.ShapeDtypeStruct((N_B_HEADS * V_AUG, n_rows), BF16)]
        out_specs += [pl.BlockSpec((rows, N_B_HEADS * HEAD_BLK), row_map),
                      pl.BlockSpec((N_B_HEADS * V_AUG, rows), col_map)]
    else:
        out_shape += [jax.ShapeDtypeStruct((n_rows, D_A), F32)]
        out_specs += [pl.BlockSpec((rows, D_A), row_map)]
    scratch = [pltpu.VMEM((rows, D_MODEL), BF16),
               pltpu.VMEM((nb, tt + SUBLANE, D_C), F32),
               pltpu.VMEM((nb, SUBLANE, D_C), F32)]
    stacked = tuple(stacked or ())
    n_in = len(in_specs)
    in_specs += [pl.BlockSpec(memory_space=pl.ANY)] * len(stacked)
    aliases = {n_in + i: 3 + i for i in range(len(stacked))}
    return pl.pallas_call(
        functools.partial(_mixer_in_kernel, nb, tt, ln, prompt, len(stacked)),
        grid=grid, in_specs=in_specs, out_specs=out_specs, out_shape=out_shape,
        scratch_shapes=scratch, input_output_aliases=aliases,
        compiler_params=pltpu.CompilerParams(dimension_semantics=("arbitrary", "arbitrary"),
                                             vmem_limit_bytes=VMEM_LIMIT),
        name="mixer_in_prompt" if prompt else "mixer_in_sample",
    )(x2d, *weights, *tabs, h0, conv0, *stacked)


NEG_BIG = -1e30
LOG2E = math.log2(math.e)
V_ONES = 16
V_AUG = V_DIM + V_ONES
ATTN_HEAD_GROUP = 4


def _last_kv_tile(qi, tq, tk):
    return ((qi + 1) * tq - 1) // tk


def _kv_steps(nq, tq, tk):
    pairs = [(i, j) for i in range(nq) for j in range(_last_kv_tile(i, tq, tk) + 1)]
    return (jnp.asarray([p[0] for p in pairs], jnp.int32), jnp.asarray([p[1] for p in pairs], jnp.int32))


def _attn_prompt_kernel(tq, tk, qi_tab, ki_tab, q_ref, k_ref, v_ref, o_ref, m_ref, acc_ref):
    step_idx = pl.program_id(1)
    qi = qi_tab[step_idx]
    ki = ki_tab[step_idx]
    last = _last_kv_tile(qi, tq, tk)

    @pl.when(ki == 0)
    def _():
        m_ref[...] = jnp.full(m_ref.shape, NEG_BIG, F32)
        acc_ref[...] = jnp.zeros(acc_ref.shape, F32)

    half = tq // 2

    def step(diag):
        if diag:
            local = (lax.broadcasted_iota(jnp.int32, (half, half), 0) // CHUNK
                     <= lax.broadcasted_iota(jnp.int32, (half, half), 1) // CHUNK)
            parts = ((0, half, half), (half, tq, tk))
        else:
            parts = ((0, tq, tk),)

        def scores(h):
            blk = slice(h * HEAD_BLK, (h + 1) * HEAD_BLK)
            return [_dot(k_ref[0:nk, blk], q_ref[blk, c0:c1]) for (c0, c1, nk) in parts]

        def softmax(h, s_parts):
            out = []
            for (c0, c1, nk), s in zip(parts, s_parts):
                if diag:
                    low = jnp.where(local, s[nk - half:], NEG_BIG)
                    s = low if nk == half else jnp.concatenate([s[:nk - half], low], axis=0)
                m_prev = m_ref[h, :, c0:c1]
                m_new = jnp.maximum(m_prev, jnp.max(s, axis=0, keepdims=True))
                m_ref[h, :, c0:c1] = m_new
                out.append((jnp.exp2(m_prev - m_new), jnp.exp2(s - m_new).astype(BF16)))
            return out

        group = ATTN_HEAD_GROUP
        s_next = [scores(h) for h in range(group)]
        for h0 in range(0, N_B_HEADS, group):
            s_cur = s_next
            if h0 + group < N_B_HEADS:
                s_next = [scores(h) for h in range(h0 + group, h0 + 2 * group)]
            probs = [softmax(h0 + i, s_cur[i]) for i in range(group)]
            for i, prob_parts in enumerate(probs):
                rows = slice((h0 + i) * V_AUG, (h0 + i + 1) * V_AUG)
                for (c0, c1, nk), (alpha, p) in zip(parts, prob_parts):
                    acc_ref[rows, c0:c1] = alpha * acc_ref[rows, c0:c1] + _dot(v_ref[rows, 0:nk], p)

    @pl.when(ki == qi)
    def _():
        step(True)

    @pl.when(ki != qi)
    def _():
        step(False)

    @pl.when(ki == last)
    def _():
        for j in range(N_B_HEADS // 2):
            halves = []
            for h in (2 * j, 2 * j + 1):
                denom = acc_ref[h * V_AUG + V_DIM:h * V_AUG + V_DIM + 1, :]
                halves.append(acc_ref[h * V_AUG:h * V_AUG + V_DIM, :] / denom)
            o_ref[:, j * LANE:(j + 1) * LANE] = jnp.concatenate(halves, axis=0).T.astype(BF16)


def _attn_prompt(q_t, k, v_t, *, n_seq, seq_len, tq, tk):
    assert tq == tk and (tq // 2) % CHUNK == 0
    nq, nk = seq_len // tq, seq_len // tk
    qi_tab, ki_tab = _kv_steps(nq, tq, tk)
    grid_spec = pltpu.PrefetchScalarGridSpec(
        num_scalar_prefetch=2,
        grid=(n_seq, qi_tab.shape[0]),
        in_specs=[pl.BlockSpec((N_B_HEADS * HEAD_BLK, tq), lambda b, s, qt, kt: (0, b * nq + qt[s])),
                  pl.BlockSpec((tk, N_B_HEADS * HEAD_BLK), lambda b, s, qt, kt: (b * nk + kt[s], 0)),
                  pl.BlockSpec((N_B_HEADS * V_AUG, tk), lambda b, s, qt, kt: (0, b * nk + kt[s]))],
        out_specs=pl.BlockSpec((tq, D_B), lambda b, s, qt, kt: (b * nq + qt[s], 0)),
        scratch_shapes=[pltpu.VMEM((N_B_HEADS, 1, tq), F32),
                        pltpu.VMEM((N_B_HEADS * V_AUG, tq), F32)])
    return pl.pallas_call(
        functools.partial(_attn_prompt_kernel, tq, tk),
        grid_spec=grid_spec,
        out_shape=jax.ShapeDtypeStruct((n_seq * seq_len, D_B), BF16),
        compiler_params=pltpu.CompilerParams(dimension_semantics=("arbitrary", "arbitrary"),
                                             vmem_limit_bytes=VMEM_LIMIT),
        name="attn_prompt",
    )(qi_tab, ki_tab, q_t, k, v_t)


def _attn_sample_kernel(t_new, tk, q_ref, clat_ref, ckr_ref, nlat_ref, nkr_ref, wkt_ref, wv_ref, o_ref):
    past = clat_ref.shape[0]
    ql, qr = [], []
    for h in range(N_B_HEADS):
        q_h = q_ref[:, h * HEAD_BLK:(h + 1) * HEAD_BLK]
        ql.append(_dot(q_h, wkt_ref[h]).astype(BF16))
        qr.append(q_h[:, :ROPE_DIM])
    ql = jnp.concatenate(ql, axis=0)
    qr = jnp.concatenate(qr, axis=0)

    def keys(i):
        if i == 0:
            return nlat_ref[...].astype(BF16), nkr_ref[...].astype(BF16)
        r0 = (i - 1) * tk
        return clat_ref[r0:r0 + tk, :].astype(BF16), ckr_ref[r0:r0 + tk, :].astype(BF16)

    def scores(i):
        c_b, kr_b = keys(i)
        return _dot_nt(ql, c_b) + _dot_nt(qr, kr_b), c_b

    rows = N_B_HEADS * t_new
    m = jnp.full((rows, 1), NEG_BIG, F32)
    l = jnp.zeros((rows, 1), F32)
    acc = jnp.zeros((rows, KV_LORA), F32)
    n_tiles = 1 + past // tk
    nxt = scores(0)
    for i in range(n_tiles):
        s, c_b = nxt
        if i + 1 < n_tiles:
            nxt = scores(i + 1)
        m_new = jnp.maximum(m, jnp.max(s, axis=1, keepdims=True))
        alpha = jnp.exp2(m - m_new)
        p = jnp.exp2(s - m_new)
        l = alpha * l + jnp.sum(p, axis=1, keepdims=True)
        acc = alpha * acc + _dot(p.astype(BF16), c_b)
        m = m_new
    o_lat = (acc / l).astype(BF16)
    out = _dot(o_lat[0:t_new], wv_ref[0])
    for h in range(1, N_B_HEADS):
        out = out + _dot(o_lat[h * t_new:(h + 1) * t_new], wv_ref[h])
    o_ref[...] = out.astype(BF16)


def _attn_sample(q, cache_lat, cache_kr, new_lat, new_kr, wkt, wv_pad, *, layer, n_seq, t_new, tk):
    past = cache_lat.shape[2]
    assert past % tk == 0
    row_map = lambda b: (b, 0)
    return pl.pallas_call(
        functools.partial(_attn_sample_kernel, t_new, tk),
        grid=(n_seq,),
        in_specs=[pl.BlockSpec((t_new, N_B_HEADS * HEAD_BLK), row_map),
                  pl.BlockSpec((None, None, past, KV_LORA), lambda b: (layer, b, 0, 0)),
                  pl.BlockSpec((None, None, past, ROPE_DIM), lambda b: (layer, b, 0, 0)),
                  pl.BlockSpec((None, t_new, KV_LORA), lambda b: (layer, b, 0)),
                  pl.BlockSpec((None, t_new, ROPE_DIM), lambda b: (layer, b, 0)),
                  _layer_spec(wkt, layer), _layer_spec(wv_pad, layer)],
        out_specs=pl.BlockSpec((t_new, D_B), row_map),
        out_shape=jax.ShapeDtypeStruct((n_seq * t_new, D_B), BF16),
        compiler_params=pltpu.CompilerParams(dimension_semantics=("arbitrary",),
                                             vmem_limit_bytes=VMEM_LIMIT),
        name="attn_sample",
    )(q, cache_lat, cache_kr, new_lat, new_kr, wkt, wv_pad)


def _mixer_out_kernel(nb, tt,
                      x_ref, a_ref, b_ref, c_ref, wo_ref, g1_ref, b1_ref,
                      wu_ref, cw_ref, cb_ref, wd_ref, g2_ref, b2_ref, s_ref,
                      y_ref, o_ref,
                      x1_ref, x1b_ref, acc_ref, bufg0_ref, bufg1_ref, bufv0_ref, bufv1_ref,
                      car_ref, act0_ref, act1_ref):
    bufg_ref = (bufg0_ref, bufg1_ref)
    bufv_ref = (bufv0_ref, bufv1_ref)
    act_ref = (act0_ref, act1_ref)
    t_idx = pl.program_id(1)

    @pl.when(t_idx == 0)
    def _():
        car_ref[...] = s_ref[...]

    mix = (_dot(a_ref[...], wo_ref[0:D_A, :]) + _dot(b_ref[...], wo_ref[D_A:D_A + D_B, :])
           + _dot(c_ref[...], wo_ref[D_A + D_B:, :]))
    x1 = _layer_norm(ALPHA * x_ref[...] + mix, g1_ref[...], b1_ref[...])
    x1_ref[...] = x1
    x1b_ref[...] = x1.astype(BF16)

    def cols(j, value_half):
        c0 = (D_FF if value_half else 0) + j * FF_CHUNK
        return slice(c0, c0 + FF_CHUNK)

    def stage(up, buf_ref, j, cs):
        for s in range(nb):
            buf_ref[j % 2][s, 0:SUBLANE, :] = car_ref[s, :, cs]
            buf_ref[j % 2][s, SUBLANE:SUBLANE + tt, :] = up[s * tt:(s + 1) * tt]

    def conv_rows(buf_ref, j, cs, s, r0, nr):
        acc = cb_ref[:, cs]
        for i in range(FFN_CONV):
            off = SUBLANE - (FFN_CONV - 1) + i + r0
            acc = acc + buf_ref[j % 2][s, off:off + nr, :] * cw_ref[i:i + 1, cs]
        return acc

    def save_tail(buf_ref, j, cs):
        for s in range(nb):
            tail = buf_ref[j % 2][s, tt:tt + SUBLANE, :]
            car_ref[s, :, cs] = tail
            o_ref[s, :, cs] = tail

    def up_proj(j):
        xb = x1b_ref[...]
        stage(_dot(xb, wu_ref[:, cols(j, False)]), bufg_ref, j, cols(j, False))
        stage(_dot(xb, wu_ref[:, cols(j, True)]), bufv_ref, j, cols(j, True))

    def elementwise(j):
        for s in range(nb):
            for r0 in range(0, tt, ROW_BLK):
                nr = min(ROW_BLK, tt - r0)
                gate = conv_rows(bufg_ref, j, cols(j, False), s, r0, nr)
                val = conv_rows(bufv_ref, j, cols(j, True), s, r0, nr)
                grp, pos = divmod(j, DOWN_GROUP)
                act_ref[grp % 2][s * tt + r0:s * tt + r0 + nr, pos * FF_CHUNK:(pos + 1) * FF_CHUNK] = (
                    jax.nn.gelu(gate) * val).astype(BF16)
        save_tail(bufg_ref, j, cols(j, False))
        save_tail(bufv_ref, j, cols(j, True))

    def down_proj(grp):
        n_chunks = min(DOWN_GROUP, N_FF - grp * DOWN_GROUP)
        k0, width = grp * DOWN_GROUP * FF_CHUNK, n_chunks * FF_CHUNK
        return _dot(act_ref[grp % 2][:, 0:width], wd_ref[k0:k0 + width, :])

    up_proj(0)
    down = None
    for j in range(N_FF + 1):
        if j + 1 < N_FF:
            up_proj(j + 1)
        if j >= 1 and (j % DOWN_GROUP == 0 or j == N_FF):
            part = down_proj((j - 1) // DOWN_GROUP)
            if down is None:
                acc_ref[...] = part
            else:
                acc_ref[...] += part
            down = True
        if j < N_FF:
            elementwise(j)
    y_ref[...] = _layer_norm(ALPHA * x1_ref[...] + acc_ref[...], g2_ref[...], b2_ref[...])


def _mixer_out(x2d, a, b, c, lw, layer, state, *, n_seq, seq_len, nb, tt):
    nt = seq_len // tt
    assert nb == 1 or nt == 1
    rows = nb * tt
    n_rows = n_seq * seq_len
    row_map = lambda bi, t: (bi * nt + t, 0)
    weights_a = [lw['w_o'], lw['ln1_g'], lw['ln1_b'], lw['w_up'], lw['ffn_cw'], lw['ffn_cb'],
                 lw['w_down'], lw['ln2_g'], lw['ln2_b']]
    st_spec = pl.BlockSpec((nb, SUBLANE, 2 * D_FF), lambda bi, t: (bi, 0, 0))
    in_specs = ([pl.BlockSpec((rows, D_MODEL), row_map), pl.BlockSpec((rows, D_A), row_map),
                 pl.BlockSpec((rows, D_B), row_map), pl.BlockSpec((rows, D_C), row_map)]
                + [_layer_spec(w, layer, single_buffer=True) for w in weights_a]
                + [st_spec])
    st_shape = jax.ShapeDtypeStruct((n_seq, SUBLANE, 2 * D_FF), F32)
    return pl.pallas_call(
        functools.partial(_mixer_out_kernel, nb, tt),
        grid=(n_seq // nb, nt),
        in_specs=in_specs,
        out_specs=[pl.BlockSpec((rows, D_MODEL), row_map), st_spec],
        out_shape=[jax.ShapeDtypeStruct((n_rows, D_MODEL), F32), st_shape],
        scratch_shapes=[pltpu.VMEM((rows, D_MODEL), F32),
                        pltpu.VMEM((rows, D_MODEL), BF16),
                        pltpu.VMEM((rows, D_MODEL), F32),
                        pltpu.VMEM((nb, tt + SUBLANE, FF_CHUNK), F32),
                        pltpu.VMEM((nb, tt + SUBLANE, FF_CHUNK), F32),
                        pltpu.VMEM((nb, tt + SUBLANE, FF_CHUNK), F32),
                        pltpu.VMEM((nb, tt + SUBLANE, FF_CHUNK), F32),
                        pltpu.VMEM((nb, SUBLANE, 2 * D_FF), F32),
                        pltpu.VMEM((rows, DOWN_GROUP * FF_CHUNK), BF16),
                        pltpu.VMEM((rows, DOWN_GROUP * FF_CHUNK), BF16)],
        compiler_params=pltpu.CompilerParams(dimension_semantics=("arbitrary", "arbitrary"),
                                             vmem_limit_bytes=VMEM_LIMIT),
        name="mixer_out",
    )(x2d, a, b, c, *weights_a, state)


def _rope_tables(pos, reps, feature_major_q):
    half = ROPE_DIM // 2
    inv = 1.0 / (ROPE_THETA ** (jnp.arange(0, ROPE_DIM, 2, dtype=F32) / ROPE_DIM))
    ang = pos.astype(F32)[:, None] * inv[None, :]
    cos, sin = jnp.cos(ang), jnp.sin(ang)
    n = pos.shape[0]
    zeros = jnp.zeros((n, HEAD_BLK - ROPE_DIM), F32)
    tkc = jnp.concatenate([cos, cos, zeros], axis=1)
    tks = jnp.concatenate([sin, sin, zeros], axis=1)
    tqc = jnp.concatenate([cos, cos, jnp.ones((n, NOPE_DIM), F32),
                           jnp.zeros((n, HEAD_BLK - ROPE_DIM - NOPE_DIM), F32)], axis=1)
    tqc = jnp.concatenate([tqc, tqc], axis=1) * (ATTN_SCALE * LOG2E)
    tqs = jnp.concatenate([tks, tks], axis=1) * (ATTN_SCALE * LOG2E)
    if feature_major_q:
        return (tqc.T, tqs.T, tkc, tks)
    return tuple(jnp.tile(t, (reps, 1)) for t in (tqc, tqs, tkc, tks))


def _block_diag(w):
    depth, n, d, e = w.shape
    return jnp.einsum('lnde,nm->lndme', w, jnp.eye(n, dtype=w.dtype)).reshape(depth, n * d, n * e)


def _prep_weights(p):
    half = ROPE_DIM // 2
    depth = p['w_in'].shape[0]
    u, v, cq, ckv, kr, xc, gate = jnp.split(p['w_in'], [256, 512, 896, 1152, 1184, 1440], axis=2)
    kr1, kr2 = kr[..., :half], kr[..., half:]
    z = jnp.zeros((depth, D_MODEL, HEAD_BLK - ROPE_DIM), F32)
    w_in = jnp.concatenate([u, v, cq, ckv, kr1, kr2, z, -kr2, kr1, z, xc, gate], axis=2)

    wq = p['mla_w_uq'].reshape(depth, Q_LORA, N_B_HEADS, NOPE_DIM + ROPE_DIM)
    nope, r1, r2 = wq[..., :NOPE_DIM], wq[..., NOPE_DIM:NOPE_DIM + half], wq[..., NOPE_DIM + half:]
    zq = lambda k: jnp.zeros((depth, Q_LORA, N_B_HEADS, k), F32)
    wq_blk = jnp.concatenate([r1, r2, nope, zq(HEAD_BLK - ROPE_DIM - NOPE_DIM)], axis=-1)
    wq_blk = wq_blk.reshape(depth, Q_LORA, -1).astype(BF16)
    wq_swp = jnp.concatenate([-r2, r1, zq(HEAD_BLK - ROPE_DIM)], axis=-1).reshape(depth, Q_LORA, -1).astype(BF16)

    w_uk, w_uv = p['mla_w_uk'], p['mla_w_uv']
    zk = jnp.zeros((depth, KV_LORA, N_B_HEADS, ROPE_DIM), F32)
    wk_pad = jnp.concatenate([zk, w_uk, zk], axis=-1).reshape(depth, KV_LORA, -1)
    zt = jnp.zeros((depth, N_B_HEADS, ROPE_DIM, KV_LORA), F32)
    wkt = jnp.concatenate([zt, jnp.transpose(w_uk, (0, 2, 3, 1)), zt], axis=2)
    wv_pad = jnp.einsum('lchd,hg->lhcgd', w_uv, jnp.eye(N_B_HEADS, dtype=F32)).reshape(
        depth, N_B_HEADS, KV_LORA, D_B)
    w_uv2 = w_uv.reshape(depth, KV_LORA, -1).astype(BF16)

    row = lambda a: a.reshape(depth, 1, -1)
    return dict(
        w_in=w_in.astype(BF16),
        wq_blk=wq_blk, wq_swp=wq_swp,
        wq_blk_t=jnp.swapaxes(wq_blk, 1, 2), wq_swp_t=jnp.swapaxes(wq_swp, 1, 2),
        wk_pad=wk_pad.astype(BF16), w_uv=w_uv2, w_uv_t=jnp.swapaxes(w_uv2, 1, 2),
        wkt=wkt.astype(BF16), wv_pad=wv_pad.astype(BF16),
        w_r=_block_diag(p['lru_w_r']).astype(BF16), w_i=_block_diag(p['lru_w_i']).astype(BF16),
        q_g=row(p['mla_q_norm_g']), kv_g=row(p['mla_kv_norm_g']),
        conv_w=p['lru_conv_w'], conv_b=row(p['lru_conv_b']),
        b_r=row(p['lru_b_r']), b_i=row(p['lru_b_i']), lam=row(p['lru_lam']),
        w_o=p['w_o'].astype(BF16),
        ln1_g=row(p['ln1_g']), ln1_b=row(p['ln1_b']), ln2_g=row(p['ln2_g']), ln2_b=row(p['ln2_b']),
        w_up=p['ffn_w_up'].astype(BF16),
        ffn_cw=jnp.pad(p['ffn_conv_w'], ((0, 0), (0, SUBLANE - FFN_CONV), (0, 0))),
        ffn_cb=row(p['ffn_conv_b']),
        w_down=p['ffn_w_down'].astype(BF16),
    )


def _gmlp_params(p, ln):
    w_s = p['gmlp_w_s'][:, :, :ln, :ln]
    b_s = jnp.repeat(jnp.swapaxes(p['gmlp_b_s'][:, :, :ln], 1, 2), A_HEAD, axis=2)
    return w_s, b_s


def _ffn_state_in(st):
    return jnp.pad(st, ((0, 0), (SUBLANE - (FFN_CONV - 1), 0), (0, 0)))


def _ffn_state_out(st):
    return st[:, SUBLANE - (FFN_CONV - 1):, :]


PROMPT_TT_IN = 512
PROMPT_TT_OUT = 256
PROMPT_TQ = 512
PROMPT_TK = 512
SAMPLE_NB = 8
SAMPLE_TK = 1024


def kernel(x_prompt, x_sample, cache_kv_latent, cache_k_rope, state_lru_h, state_lru_conv, state_ffn_conv,
           ln1_g, ln1_b, ln2_g, ln2_b, w_in, w_o, gmlp_w_s, gmlp_b_s, mla_q_norm_g, mla_w_uq,
           mla_kv_norm_g, mla_w_uk, mla_w_uv, lru_conv_w, lru_conv_b, lru_w_r, lru_b_r, lru_w_i, lru_b_i,
           lru_lam, ffn_w_up, ffn_conv_w, ffn_conv_b, ffn_w_down):
    p = dict(ln1_g=ln1_g, ln1_b=ln1_b, ln2_g=ln2_g, ln2_b=ln2_b, w_in=w_in, w_o=w_o, gmlp_w_s=gmlp_w_s,
             gmlp_b_s=gmlp_b_s, mla_q_norm_g=mla_q_norm_g, mla_w_uq=mla_w_uq, mla_kv_norm_g=mla_kv_norm_g,
             mla_w_uk=mla_w_uk, mla_w_uv=mla_w_uv, lru_conv_w=lru_conv_w, lru_conv_b=lru_conv_b,
             lru_w_r=lru_w_r, lru_b_r=lru_b_r, lru_w_i=lru_w_i, lru_b_i=lru_b_i, lru_lam=lru_lam,
             ffn_w_up=ffn_w_up, ffn_conv_w=ffn_conv_w, ffn_conv_b=ffn_conv_b, ffn_w_down=ffn_w_down)
    bp, s_len, _ = x_prompt.shape
    bd, t_len, _ = x_sample.shape
    past = cache_kv_latent.shape[2]
    depth = w_in.shape[0]
    ln_p, ln_d = min(s_len, GMLP_CHUNK), min(t_len, GMLP_CHUNK)

    tabs_p = _rope_tables(jnp.arange(s_len), 1, True)
    tabs_d = _rope_tables(past + jnp.arange(t_len), SAMPLE_NB, False)
    zero_rows = jnp.zeros((bp, SUBLANE, D_C), F32)
    zero_ffn = jnp.zeros((bp, SUBLANE, 2 * D_FF), F32)

    xp = x_prompt.reshape(bp * s_len, D_MODEL)
    xd = x_sample.reshape(bd * t_len, D_MODEL)
    lw = _prep_weights(p)
    lw_p = dict(lw, **dict(zip(('w_s', 'b_s'), _gmlp_params(p, ln_p))))
    lw_d = dict(lw, **dict(zip(('w_s', 'b_s'), _gmlp_params(p, ln_d))))
    outs = {i: [] for i in (2, 3, 4, 7, 8, 9, 10)}
    stacked_p = stacked_d = None
    for l in range(depth):
        a, c, q, lat_p, kr_p, hst, cst, k, v = _mixer_in(
            xp, lw_p, l, tabs_p, zero_rows, zero_rows, stacked_p,
            n_seq=bp, seq_len=s_len, nb=1, tt=PROMPT_TT_IN, ln=ln_p, prompt=True)
        stacked_p = (lat_p, kr_p)
        b = _attn_prompt(q, k, v, n_seq=bp, seq_len=s_len, tq=PROMPT_TQ, tk=PROMPT_TK)
        xp, ffn_st = _mixer_out(xp, a, b, c, lw, l, zero_ffn,
                                n_seq=bp, seq_len=s_len, nb=1, tt=PROMPT_TT_OUT)
        outs[2].append(hst[:, 0, :])
        outs[3].append(cst[:, SUBLANE - (LRU_CONV - 1):, :])
        outs[4].append(_ffn_state_out(ffn_st))

        h0 = jnp.broadcast_to(state_lru_h[l][:, None, :], (bd, SUBLANE, D_C))
        conv0 = jnp.pad(state_lru_conv[l], ((0, 0), (SUBLANE - (LRU_CONV - 1), 0), (0, 0)))
        ffn_st0 = _ffn_state_in(state_ffn_conv[l])
        a, c, q, lat_d, kr_d, hst, cst, vg = _mixer_in(
            xd, lw_d, l, tabs_d, h0, conv0, stacked_d,
            n_seq=bd, seq_len=t_len, nb=SAMPLE_NB, tt=t_len, ln=ln_d, prompt=False)
        stacked_d = (lat_d, kr_d)
        b = _attn_sample(q, cache_kv_latent, cache_k_rope, lat_d, kr_d, lw['wkt'], lw['wv_pad'],
                         layer=l, n_seq=bd, t_new=t_len, tk=SAMPLE_TK)
        xd, ffn_st = _mixer_out(xd, a, b, c, lw, l, ffn_st0, n_seq=bd, seq_len=t_len, nb=SAMPLE_NB, tt=t_len)
        outs[7].append(vg.reshape(bd, t_len, D_A))
        outs[8].append(hst[:, 0, :])
        outs[9].append(cst[:, SUBLANE - (LRU_CONV - 1):, :])
        outs[10].append(_ffn_state_out(ffn_st))

    st = {i: jnp.stack(o) for i, o in outs.items()}
    return (xp.reshape(bp, s_len, D_MODEL), xd.reshape(bd, t_len, D_MODEL),
            lat_p.reshape(depth, bp, s_len, KV_LORA), kr_p.reshape(depth, bp, s_len, ROPE_DIM),
            st[2], st[3], st[4],
            lat_d.reshape(depth, bd, t_len, KV_LORA), kr_d.reshape(depth, bd, t_len, ROPE_DIM),
            st[7], st[8], st[9], st[10])
```

```python
import functools
import math

import jax
import jax.numpy as jnp
from jax import lax
from jax.experimental import pallas as pl
from jax.experimental.pallas import tpu as pltpu

F32 = jnp.float32
BF16 = jnp.bfloat16

D_MODEL = 1024
DEPTH = 4
CHUNK = 64
GMLP_CHUNK = 128
D_A = 256
N_A_HEADS = 4
A_HEAD = 64
D_B = 512
N_B_HEADS = 8
V_DIM = 64
NOPE_DIM = 64
ROPE_DIM = 32
Q_LORA = 384
KV_LORA = 256
ROPE_THETA = 10000.0
ATTN_SCALE = (NOPE_DIM + ROPE_DIM) ** -0.5
D_C = 256
N_C_BLOCKS = 4
C_HEAD = 64
LRU_CONV = 4
LRU_C = 8.0
D_FF = 2816
FFN_CONV = 3
ALPHA = (2.0 * DEPTH) ** 0.25
LN_EPS = 1e-5
RMS_EPS = 1e-6

LANE = 128
SUBLANE = 8
HEAD_BLK = 128
VMEM_LIMIT = 56 * 1024 * 1024

C_U, C_V, C_CQ, C_CKV, C_KR, C_XC, C_GATE, C_END = 0, 256, 512, 896, 1152, 1408, 1664, 1920

FF_CHUNK = 256
ROW_BLK = 64
DOWN_GROUP = 3
N_FF = D_FF // FF_CHUNK


def _dot(a, b):
    return jnp.dot(a, b, preferred_element_type=F32)


def _dot_nt(a, b):
    return lax.dot_general(a, b, (((1,), (1,)), ((), ())), preferred_element_type=F32)


def _rms_norm(x, g):
    ms = jnp.mean(x * x, axis=-1, keepdims=True)
    return x * lax.rsqrt(ms + RMS_EPS) * g


def _layer_norm(x, g, b):
    mu = jnp.mean(x, axis=-1, keepdims=True)
    xc = x - mu
    var = jnp.mean(xc * xc, axis=-1, keepdims=True)
    return xc * lax.rsqrt(var + LN_EPS) * g + b


def _mixer_in_kernel(nb, tt, ln, prompt, n_aliased,
                     x_ref, win_ref, wqb_ref, wqs_ref, wkp_ref, wuv_ref, wr_ref, wi_ref,
                     ws_ref, bs_ref, qg_ref, kvg_ref, cw_ref, cb_ref, br_ref, bi_ref, lam_ref,
                     tqc_ref, tqs_ref, tkc_ref, tks_ref, h0_ref, conv0_ref,
                     *rest):
    rest = rest[n_aliased:]
    if prompt:
        (a_ref, c_ref, q_ref, lat_ref, kr_ref, hst_ref, cst_ref, k_ref, v_ref,
         xb_ref, cv_ref, hc_ref) = rest
        vg_ref = None
    else:
        (a_ref, c_ref, q_ref, lat_ref, kr_ref, hst_ref, cst_ref, vg_ref,
         xb_ref, cv_ref, hc_ref) = rest
        k_ref = v_ref = None
    rows = nb * tt
    t_idx = pl.program_id(1)

    @pl.when(t_idx == 0)
    def _():
        cv_ref[:, 0:SUBLANE, :] = conv0_ref[...]
        hc_ref[...] = h0_ref[...]

    xb_ref[...] = x_ref[...].astype(BF16)

    def proj(c0, c1):
        return _dot(xb_ref[...], win_ref[:, c0:c1])

    xc_in = proj(C_XC, C_GATE)
    z_gate = proj(C_GATE, C_END)
    z_u = proj(C_U, C_V)
    z_v = proj(C_V, C_CQ)
    z_cq = proj(C_CQ, C_CKV)
    z_ckv = proj(C_CKV, C_KR)
    kr2 = proj(C_KR, C_XC)

    xcs = []
    for s in range(nb):
        cv_ref[s, SUBLANE:SUBLANE + tt, :] = xc_in[s * tt:(s + 1) * tt]
        acc = cb_ref[...]
        for j in range(LRU_CONV):
            off = SUBLANE - (LRU_CONV - 1) + j
            acc = acc + cv_ref[s, off:off + tt, :] * cw_ref[j:j + 1, :]
        xcs.append(acc)
        tail = cv_ref[s, tt:tt + SUBLANE, :]
        cst_ref[s] = tail
        cv_ref[s, 0:SUBLANE, :] = tail
    xc = xcs[0] if nb == 1 else jnp.concatenate(xcs, axis=0)
    xcb = xc.astype(BF16)
    z_r = _dot(xcb, wr_ref[...])
    z_i = _dot(xcb, wi_ref[...])

    u = jax.nn.gelu(z_u)
    v = jax.nn.gelu(z_v)
    if vg_ref is not None:
        vg_ref[...] = v
    vb = v.astype(BF16)
    row_i = lax.broadcasted_iota(jnp.int32, (ln, ln), 0)
    col_i = lax.broadcasted_iota(jnp.int32, (ln, ln), 1)
    w_tril = [jnp.where(row_i >= col_i, ws_ref[h], 0.0).astype(BF16) for h in range(N_A_HEADS)]
    head_of_lane = lax.broadcasted_iota(jnp.int32, (ln, D_A), 1) // A_HEAD
    for c in range(rows // ln):
        r0 = c * ln
        vc = vb[r0:r0 + ln]
        gate = bs_ref[...]
        for h in range(N_A_HEADS):
            gate = gate + _dot(w_tril[h], jnp.where(head_of_lane == h, vc, jnp.zeros_like(vc)))
        a_ref[r0:r0 + ln, :] = (u[r0:r0 + ln] * gate).astype(BF16)

    cqn = _rms_norm(z_cq, qg_ref[...]).astype(BF16)
    tqc = tqc_ref[...]
    tqs = tqs_ref[...]
    for j in range(N_B_HEADS // 2):
        c0, c1 = 2 * j * HEAD_BLK, 2 * (j + 1) * HEAD_BLK
        if prompt:
            qb = _dot_nt(wqb_ref[c0:c1, :], cqn)
            half = ROPE_DIM // 2
            for i in range(2):
                r0 = i * HEAD_BLK
                r1, r2 = qb[r0:r0 + half], qb[r0 + half:r0 + ROPE_DIM]
                rest = qb[r0 + ROPE_DIM:r0 + HEAD_BLK] * (ATTN_SCALE * LOG2E)
                blk = jnp.concatenate([r1 * tqc - r2 * tqs, r1 * tqs + r2 * tqc, rest], axis=0)
                q_ref[c0 + r0:c0 + r0 + HEAD_BLK, :] = blk.astype(BF16)
        else:
            q_ref[:, c0:c1] = (_dot(cqn, wqb_ref[:, c0:c1]) * tqc
                               + _dot(cqn, wqs_ref[:, c0:c1]) * tqs).astype(BF16)
    ckvn = _rms_norm(z_ckv, kvg_ref[...])
    lat_ref[...] = ckvn
    kr_rot = kr2[:, :HEAD_BLK] * tkc_ref[...] + kr2[:, HEAD_BLK:] * tks_ref[...]
    kr_ref[...] = kr_rot[:, :ROPE_DIM]
    if prompt:
        ckvn_b = ckvn.astype(BF16)
        for j in range(N_B_HEADS // 2):
            c0 = 2 * j * HEAD_BLK
            kp = _dot(ckvn_b, wkp_ref[:, c0:c0 + 2 * HEAD_BLK])
            k_ref[:, c0:c0 + HEAD_BLK] = (kp[:, :HEAD_BLK] + kr_rot).astype(BF16)
            k_ref[:, c0 + HEAD_BLK:c0 + 2 * HEAD_BLK] = (kp[:, HEAD_BLK:] + kr_rot).astype(BF16)
        v_t = _dot_nt(wuv_ref[...], ckvn_b).astype(BF16)
        for h in range(N_B_HEADS):
            v_ref[h * V_AUG:h * V_AUG + V_DIM, :] = v_t[h * V_DIM:(h + 1) * V_DIM]
            v_ref[h * V_AUG + V_DIM:(h + 1) * V_AUG, :] = jnp.ones((V_ONES, rows), BF16)

    r = jax.nn.sigmoid(z_r + br_ref[...])
    ig = jax.nn.sigmoid(z_i + bi_ref[...])
    log_a = (-LRU_C) * r * jax.nn.softplus(-lam_ref[...])
    a = jnp.exp(log_a)
    b_in = jnp.sqrt(-jnp.tanh(log_a) * (a * a + 1.0)) * (ig * xc)

    groups = rows // SUBLANE
    a3 = a.reshape(groups, SUBLANE, D_C)
    b3 = b_in.reshape(groups, SUBLANE, D_C)
    sub = lax.broadcasted_iota(jnp.int32, (groups, SUBLANE, D_C), 1)
    for k in (1, 2, 4):
        keep = sub >= k
        a_sh = jnp.where(keep, pltpu.roll(a3, k, 1), 1.0)
        b_sh = jnp.where(keep, pltpu.roll(b3, k, 1), 0.0)
        b3 = a3 * b_sh + b3
        a3 = a3 * a_sh
    gs = tt // SUBLANE
    h_rows = []
    for s in range(nb):
        hb = hc_ref[s]
        for g in range(s * gs, (s + 1) * gs):
            hr = a3[g] * hb + b3[g]
            h_rows.append(hr)
            hb = jnp.broadcast_to(hr[SUBLANE - 1:SUBLANE, :], (SUBLANE, D_C))
        hc_ref[s] = hb
        hst_ref[s] = hb
    c_ref[...] = (jnp.concatenate(h_rows, axis=0) * jax.nn.gelu(z_gate)).astype(BF16)


def _layer_spec(w, layer, single_buffer=False):
    tail = (0,) * (w.ndim - 1)
    mode = dict(pipeline_mode=pl.Buffered(1)) if single_buffer else {}
    return pl.BlockSpec((None,) + w.shape[1:], lambda *_: (layer,) + tail, **mode)


def _mixer_in(x2d, lw, layer, tabs, h0, conv0, stacked, *, n_seq, seq_len, nb, tt, ln, prompt):
    depth = lw['w_in'].shape[0]
    nt = seq_len // tt
    assert nb == 1 or nt == 1
    rows = nb * tt
    n_rows = n_seq * seq_len
    grid = (n_seq // nb, nt)
    row_map = lambda b, t: (b * nt + t, 0)
    layer_row_map = lambda b, t: (layer, b * nt + t, 0)
    tab_map = lambda b, t: (t, 0)
    seq_map = lambda b, t: (b, 0, 0)

    sfx = '_t' if prompt else ''
    weights = [lw['w_in'], lw['wq_blk' + sfx], lw['wq_swp' + sfx], lw['wk_pad'], lw['w_uv' + sfx], lw['w_r'], lw['w_i'],
               lw['w_s'], lw['b_s'], lw['q_g'], lw['kv_g'], lw['conv_w'], lw['conv_b'], lw['b_r'],
               lw['b_i'], lw['lam']]
    col_map = lambda b, t: (0, b * nt + t)
    if prompt:
        q_tab_spec = pl.BlockSpec((ROPE_DIM // 2, rows), lambda b, t: (0, t))
        q_shape = jax.ShapeDtypeStruct((N_B_HEADS * HEAD_BLK, n_rows), BF16)
        q_spec = pl.BlockSpec((N_B_HEADS * HEAD_BLK, rows), col_map)
    else:
        q_tab_spec = pl.BlockSpec((rows, 2 * HEAD_BLK), tab_map)
        q_shape = jax.ShapeDtypeStruct((n_rows, N_B_HEADS * HEAD_BLK), BF16)
        q_spec = pl.BlockSpec((rows, N_B_HEADS * HEAD_BLK), row_map)
    in_specs = ([pl.BlockSpec((rows, D_MODEL), row_map)]
                + [_layer_spec(w, layer) for w in weights]
                + [q_tab_spec, q_tab_spec,
                   pl.BlockSpec((rows, HEAD_BLK), tab_map), pl.BlockSpec((rows, HEAD_BLK), tab_map),
                   pl.BlockSpec((nb, SUBLANE, D_C), seq_map), pl.BlockSpec((nb, SUBLANE, D_C), seq_map)])
    out_shape = [jax.ShapeDtypeStruct((n_rows, D_A), BF16),
                 jax.ShapeDtypeStruct((n_rows, D_C), BF16),
                 q_shape,
                 jax.ShapeDtypeStruct((depth, n_rows, KV_LORA), F32),
                 jax.ShapeDtypeStruct((depth, n_rows, ROPE_DIM), F32),
                 jax.ShapeDtypeStruct((n_seq, SUBLANE, D_C), F32),
                 jax.ShapeDtypeStruct((n_seq, SUBLANE, D_C), F32)]
    out_specs = [pl.BlockSpec((rows, D_A), row_map), pl.BlockSpec((rows, D_C), row_map),
                 q_spec, pl.BlockSpec((None, rows, KV_LORA), layer_row_map),
                 pl.BlockSpec((None, rows, ROPE_DIM), layer_row_map),
                 pl.BlockSpec((nb, SUBLANE, D_C), seq_map), pl.BlockSpec((nb, SUBLANE, D_C), seq_map)]
    if prompt:
        out_shape += [jax.ShapeDtypeStruct((n_rows, N_B_HEADS * HEAD_BLK), BF16),
                      jax.ShapeDtypeStruct((N_B_HEADS * V_AUG, n_rows), BF16)]
        out_specs += [pl.BlockSpec((rows, N_B_HEADS * HEAD_BLK), row_map),
                      pl.BlockSpec((N_B_HEADS * V_AUG, rows), col_map)]
    else:
        out_shape += [jax.ShapeDtypeStruct((n_rows, D_A), F32)]
        out_specs += [pl.BlockSpec((rows, D_A), row_map)]
    scratch = [pltpu.VMEM((rows, D_MODEL), BF16),
               pltpu.VMEM((nb, tt + SUBLANE, D_C), F32),
               pltpu.VMEM((nb, SUBLANE, D_C), F32)]
    stacked = tuple(stacked or ())
    n_in = len(in_specs)
    in_specs += [pl.BlockSpec(memory_space=pl.ANY)] * len(stacked)
    aliases = {n_in + i: 3 + i for i in range(len(stacked))}
    return pl.pallas_call(
        functools.partial(_mixer_in_kernel, nb, tt, ln, prompt, len(stacked)),
        grid=grid, in_specs=in_specs, out_specs=out_specs, out_shape=out_shape,
        scratch_shapes=scratch, input_output_aliases=aliases,
        compiler_params=pltpu.CompilerParams(dimension_semantics=("arbitrary", "arbitrary"),
                                             vmem_limit_bytes=VMEM_LIMIT),
        name="mixer_in_prompt" if prompt else "mixer_in_sample",
    )(x2d, *weights, *tabs, h0, conv0, *stacked)


NEG_BIG = -1e30
LOG2E = math.log2(math.e)
V_ONES = 16
V_AUG = V_DIM + V_ONES
ATTN_HEAD_GROUP = 4


def _last_kv_tile(qi, tq, tk):
    return ((qi + 1) * tq - 1) // tk


def _kv_steps(nq, tq, tk):
    pairs = [(i, j) for i in range(nq) for j in range(_last_kv_tile(i, tq, tk) + 1)]
    return (jnp.asarray([p[0] for p in pairs], jnp.int32), jnp.asarray([p[1] for p in pairs], jnp.int32))


def _attn_prompt_kernel(tq, tk, qi_tab, ki_tab, q_ref, k_ref, v_ref, o_ref, m_ref, acc_ref):
    step_idx = pl.program_id(1)
    qi = qi_tab[step_idx]
    ki = ki_tab[step_idx]
    last = _last_kv_tile(qi, tq, tk)

    @pl.when(ki == 0)
    def _():
        m_ref[...] = jnp.full(m_ref.shape, NEG_BIG, F32)
        acc_ref[...] = jnp.zeros(acc_ref.shape, F32)

    half = tq // 2

    def step(diag):
        if diag:
            local = (lax.broadcasted_iota(jnp.int32, (half, half), 0) // CHUNK
                     <= lax.broadcasted_iota(jnp.int32, (half, half), 1) // CHUNK)
            parts = ((0, half, half), (half, tq, tk))
        else:
            parts = ((0, tq, tk),)

        def scores(h):
            blk = slice(h * HEAD_BLK, (h + 1) * HEAD_BLK)
            return [_dot(k_ref[0:nk, blk], q_ref[blk, c0:c1]) for (c0, c1, nk) in parts]

        def softmax(h, s_parts):
            out = []
            for (c0, c1, nk), s in zip(parts, s_parts):
                if diag:
                    low = jnp.where(local, s[nk - half:], NEG_BIG)
                    s = low if nk == half else jnp.concatenate([s[:nk - half], low], axis=0)
                m_prev = m_ref[h, :, c0:c1]
                m_new = jnp.maximum(m_prev, jnp.max(s, axis=0, keepdims=True))
                m_ref[h, :, c0:c1] = m_new
                out.append((jnp.exp2(m_prev - m_new), jnp.exp2(s - m_new).astype(BF16)))
            return out

        group = ATTN_HEAD_GROUP
        s_next = [scores(h) for h in range(group)]
        for h0 in range(0, N_B_HEADS, group):
            s_cur = s_next
            if h0 + group < N_B_HEADS:
                s_next = [scores(h) for h in range(h0 + group, h0 + 2 * group)]
            probs = [softmax(h0 + i, s_cur[i]) for i in range(group)]
            for i, prob_parts in enumerate(probs):
                rows = slice((h0 + i) * V_AUG, (h0 + i + 1) * V_AUG)
                for (c0, c1, nk), (alpha, p) in zip(parts, prob_parts):
                    acc_ref[rows, c0:c1] = alpha * acc_ref[rows, c0:c1] + _dot(v_ref[rows, 0:nk], p)

    @pl.when(ki == qi)
    def _():
        step(True)

    @pl.when(ki != qi)
    def _():
        step(False)

    @pl.when(ki == last)
    def _():
        for j in range(N_B_HEADS // 2):
            halves = []
            for h in (2 * j, 2 * j + 1):
                denom = acc_ref[h * V_AUG + V_DIM:h * V_AUG + V_DIM + 1, :]
                halves.append(acc_ref[h * V_AUG:h * V_AUG + V_DIM, :] / denom)
            o_ref[:, j * LANE:(j + 1) * LANE] = jnp.concatenate(halves, axis=0).T.astype(BF16)


def _attn_prompt(q_t, k, v_t, *, n_seq, seq_len, tq, tk):
    assert tq == tk and (tq // 2) % CHUNK == 0
    nq, nk = seq_len // tq, seq_len // tk
    qi_tab, ki_tab = _kv_steps(nq, tq, tk)
    grid_spec = pltpu.PrefetchScalarGridSpec(
        num_scalar_prefetch=2,
        grid=(n_seq, qi_tab.shape[0]),
        in_specs=[pl.BlockSpec((N_B_HEADS * HEAD_BLK, tq), lambda b, s, qt, kt: (0, b * nq + qt[s])),
                  pl.BlockSpec((tk, N_B_HEADS * HEAD_BLK), lambda b, s, qt, kt: (b * nk + kt[s], 0)),
                  pl.BlockSpec((N_B_HEADS * V_AUG, tk), lambda b, s, qt, kt: (0, b * nk + kt[s]))],
        out_specs=pl.BlockSpec((tq, D_B), lambda b, s, qt, kt: (b * nq + qt[s], 0)),
        scratch_shapes=[pltpu.VMEM((N_B_HEADS, 1, tq), F32),
                        pltpu.VMEM((N_B_HEADS * V_AUG, tq), F32)])
    return pl.pallas_call(
        functools.partial(_attn_prompt_kernel, tq, tk),
        grid_spec=grid_spec,
        out_shape=jax.ShapeDtypeStruct((n_seq * seq_len, D_B), BF16),
        compiler_params=pltpu.CompilerParams(dimension_semantics=("arbitrary", "arbitrary"),
                                             vmem_limit_bytes=VMEM_LIMIT),
        name="attn_prompt",
    )(qi_tab, ki_tab, q_t, k, v_t)


def _attn_sample_kernel(t_new, tk, q_ref, clat_ref, ckr_ref, nlat_ref, nkr_ref, wkt_ref, wv_ref, o_ref):
    past = clat_ref.shape[0]
    ql, qr = [], []
    for h in range(N_B_HEADS):
        q_h = q_ref[:, h * HEAD_BLK:(h + 1) * HEAD_BLK]
        ql.append(_dot(q_h, wkt_ref[h]).astype(BF16))
        qr.append(q_h[:, :ROPE_DIM])
    ql = jnp.concatenate(ql, axis=0)
    qr = jnp.concatenate(qr, axis=0)

    def scores(i):
        if i == 0:
            c_b = nlat_ref[...].astype(BF16)
            return _dot_nt(ql, c_b) + _dot_nt(qr, nkr_ref[...].astype(BF16)), c_b
        r0 = (i - 1) * tk
        c_b = clat_ref[r0:r0 + tk, :].astype(BF16)
        return _dot_nt(ql, c_b) + _dot(qr, ckr_ref[:, r0:r0 + tk].astype(BF16)), c_b

    rows = N_B_HEADS * t_new
    m = jnp.full((rows, 1), NEG_BIG, F32)
    l = jnp.zeros((rows, 1), F32)
    acc = jnp.zeros((rows, KV_LORA), F32)
    n_tiles = 1 + past // tk
    nxt = scores(0)
    for i in range(n_tiles):
        s, c_b = nxt
        if i + 1 < n_tiles:
            nxt = scores(i + 1)
        m_new = jnp.maximum(m, jnp.max(s, axis=1, keepdims=True))
        alpha = jnp.exp2(m - m_new)
        p = jnp.exp2(s - m_new)
        l = alpha * l + jnp.sum(p, axis=1, keepdims=True)
        acc = alpha * acc + _dot(p.astype(BF16), c_b)
        m = m_new
    o_lat = (acc / l).astype(BF16)
    out = _dot(o_lat[0:t_new], wv_ref[0])
    for h in range(1, N_B_HEADS):
        out = out + _dot(o_lat[h * t_new:(h + 1) * t_new], wv_ref[h])
    o_ref[...] = out.astype(BF16)


def _attn_sample(q, cache_lat, cache_kr, new_lat, new_kr, wkt, wv_pad, *, layer, n_seq, t_new, tk):
    past = cache_lat.shape[2]
    assert past % tk == 0
    row_map = lambda b: (b, 0)
    return pl.pallas_call(
        functools.partial(_attn_sample_kernel, t_new, tk),
        grid=(n_seq,),
        in_specs=[pl.BlockSpec((t_new, N_B_HEADS * HEAD_BLK), row_map),
                  pl.BlockSpec((None, None, past, KV_LORA), lambda b: (layer, b, 0, 0)),
                  pl.BlockSpec((None, None, ROPE_DIM, past), lambda b: (layer, b, 0, 0)),
                  pl.BlockSpec((None, t_new, KV_LORA), lambda b: (layer, b, 0)),
                  pl.BlockSpec((None, t_new, ROPE_DIM), lambda b: (layer, b, 0)),
                  _layer_spec(wkt, layer), _layer_spec(wv_pad, layer)],
        out_specs=pl.BlockSpec((t_new, D_B), row_map),
        out_shape=jax.ShapeDtypeStruct((n_seq * t_new, D_B), BF16),
        compiler_params=pltpu.CompilerParams(dimension_semantics=("arbitrary",),
                                             vmem_limit_bytes=VMEM_LIMIT),
        name="attn_sample",
    )(q, cache_lat, cache_kr, new_lat, new_kr, wkt, wv_pad)


def _mixer_out_kernel(nb, tt,
                      x_ref, a_ref, b_ref, c_ref, wo_ref, g1_ref, b1_ref,
                      wu_ref, cw_ref, cb_ref, wd_ref, g2_ref, b2_ref, s_ref,
                      y_ref, o_ref,
                      x1_ref, xp_ref, x1b_ref, acc_ref, bufg0_ref, bufg1_ref, bufv0_ref, bufv1_ref,
                      car_ref, act0_ref, act1_ref):
    bufg_ref = (bufg0_ref, bufg1_ref)
    bufv_ref = (bufv0_ref, bufv1_ref)
    act_ref = (act0_ref, act1_ref)
    t_idx = pl.program_id(1)
    rows = nb * tt
    n_tiles = rows // SUBLANE
    halo = (FFN_CONV - 1) * SUBLANE

    @pl.when(t_idx == 0)
    def _():
        car_ref[...] = s_ref[...]

    mix = (_dot(a_ref[...], wo_ref[0:D_A, :]) + _dot(b_ref[...], wo_ref[D_A:D_A + D_B, :])
           + _dot(c_ref[...], wo_ref[D_A + D_B:, :]))
    x1 = _layer_norm(ALPHA * x_ref[...] + mix, g1_ref[...], b1_ref[...])
    x1_ref[...] = x1
    for c in range(D_MODEL // LANE):
        for k in range(n_tiles):
            xp_ref[c, pl.ds(_interleave_start(k, n_tiles), SUBLANE, stride=SUBLANE), :] = (
                x1[k * SUBLANE:(k + 1) * SUBLANE, c * LANE:(c + 1) * LANE])
        x1b_ref[:, c * LANE:(c + 1) * LANE] = xp_ref[c].astype(BF16)

    def cols(j, value_half):
        c0 = (D_FF if value_half else 0) + j * FF_CHUNK
        return slice(c0, c0 + FF_CHUNK)

    first_sublane = lax.broadcasted_iota(jnp.int32, (SUBLANE, FF_CHUNK), 0) == 0

    def stage(up, buf_ref, j, cs):
        buf = buf_ref[j % 2]
        buf[halo:halo + rows, :] = up
        for i in range(FFN_CONV - 1):
            carry_rows = slice(i * SUBLANE, (i + 1) * SUBLANE)
            last = up[rows - halo + i * SUBLANE:rows - halo + (i + 1) * SUBLANE]
            if nb == 1:
                last = pltpu.roll(last, 1, 0)
                prev = jnp.where(first_sublane, car_ref[carry_rows, cs], last)
            else:
                prev = car_ref[carry_rows, cs]
            buf[carry_rows, :] = prev
            car_ref[carry_rows, cs] = last
            o_ref[carry_rows, cs] = last

    def conv_rows(buf_ref, j, cs, r0, nr):
        acc = cb_ref[:, cs]
        for i in range(FFN_CONV):
            acc = acc + buf_ref[j % 2][r0 + i * SUBLANE:r0 + i * SUBLANE + nr, :] * cw_ref[i:i + 1, cs]
        return acc

    def up_proj(j):
        xb = x1b_ref[...]
        stage(_dot(xb, wu_ref[:, cols(j, False)]), bufg_ref, j, cols(j, False))
        stage(_dot(xb, wu_ref[:, cols(j, True)]), bufv_ref, j, cols(j, True))

    def elementwise(j):
        for r0 in range(0, rows, ROW_BLK):
            nr = min(ROW_BLK, rows - r0)
            gate = conv_rows(bufg_ref, j, cols(j, False), r0, nr)
            val = conv_rows(bufv_ref, j, cols(j, True), r0, nr)
            grp, pos = divmod(j, DOWN_GROUP)
            act_ref[grp % 2][r0:r0 + nr, pos * FF_CHUNK:(pos + 1) * FF_CHUNK] = (
                jax.nn.gelu(gate) * val).astype(BF16)

    def down_proj(grp):
        n_chunks = min(DOWN_GROUP, N_FF - grp * DOWN_GROUP)
        k0, width = grp * DOWN_GROUP * FF_CHUNK, n_chunks * FF_CHUNK
        return _dot(act_ref[grp % 2][:, 0:width], wd_ref[k0:k0 + width, :])

    up_proj(0)
    down = None
    for j in range(N_FF + 1):
        if j + 1 < N_FF:
            up_proj(j + 1)
        if j >= 1 and (j % DOWN_GROUP == 0 or j == N_FF):
            part = down_proj((j - 1) // DOWN_GROUP)
            for c in range(D_MODEL // LANE):
                if down is None:
                    acc_ref[c] = part[:, c * LANE:(c + 1) * LANE]
                else:
                    acc_ref[c] += part[:, c * LANE:(c + 1) * LANE]
            down = True
        if j < N_FF:
            elementwise(j)
    ffn = jnp.concatenate(
        [jnp.concatenate([acc_ref[c, pl.ds(_interleave_start(k, n_tiles), SUBLANE, stride=SUBLANE), :]
                          for k in range(n_tiles)], axis=0)
         for c in range(D_MODEL // LANE)], axis=1)
    y_ref[...] = _layer_norm(ALPHA * x1_ref[...] + ffn, g2_ref[...], b2_ref[...])


def _interleave_start(k, n_tiles):
    group, part = divmod(k, n_tiles // SUBLANE)
    return part * SUBLANE * SUBLANE + group


FFN_STATE_ROWS = (FFN_CONV - 1) * SUBLANE


def _mixer_out(x2d, a, b, c, lw, layer, state, *, n_seq, seq_len, nb, tt):
    nt = seq_len // tt
    assert nb == 1 or nt == 1
    rows = nb * tt
    assert nb in (1, SUBLANE) and rows % (SUBLANE * SUBLANE) == 0
    n_rows = n_seq * seq_len
    row_map = lambda bi, t: (bi * nt + t, 0)
    weights_a = [lw['w_o'], lw['ln1_g'], lw['ln1_b'], lw['w_up'], lw['ffn_cw'], lw['ffn_cb'],
                 lw['w_down'], lw['ln2_g'], lw['ln2_b']]
    st_spec = pl.BlockSpec((None, FFN_STATE_ROWS, 2 * D_FF), lambda bi, t: (bi, 0, 0))
    in_specs = ([pl.BlockSpec((rows, D_MODEL), row_map), pl.BlockSpec((rows, D_A), row_map),
                 pl.BlockSpec((rows, D_B), row_map), pl.BlockSpec((rows, D_C), row_map)]
                + [_layer_spec(w, layer, single_buffer=True) for w in weights_a]
                + [st_spec])
    st_shape = jax.ShapeDtypeStruct((n_seq // nb, FFN_STATE_ROWS, 2 * D_FF), F32)
    buf_rows = FFN_STATE_ROWS + rows
    return pl.pallas_call(
        functools.partial(_mixer_out_kernel, nb, tt),
        grid=(n_seq // nb, nt),
        in_specs=in_specs,
        out_specs=[pl.BlockSpec((rows, D_MODEL), row_map), st_spec],
        out_shape=[jax.ShapeDtypeStruct((n_rows, D_MODEL), F32), st_shape],
        scratch_shapes=[pltpu.VMEM((rows, D_MODEL), F32),
                        pltpu.VMEM((D_MODEL // LANE, rows, LANE), F32),
                        pltpu.VMEM((rows, D_MODEL), BF16),
                        pltpu.VMEM((D_MODEL // LANE, rows, LANE), F32),
                        pltpu.VMEM((buf_rows, FF_CHUNK), F32),
                        pltpu.VMEM((buf_rows, FF_CHUNK), F32),
                        pltpu.VMEM((buf_rows, FF_CHUNK), F32),
                        pltpu.VMEM((buf_rows, FF_CHUNK), F32),
                        pltpu.VMEM((FFN_STATE_ROWS, 2 * D_FF), F32),
                        pltpu.VMEM((rows, DOWN_GROUP * FF_CHUNK), BF16),
                        pltpu.VMEM((rows, DOWN_GROUP * FF_CHUNK), BF16)],
        compiler_params=pltpu.CompilerParams(dimension_semantics=("arbitrary", "arbitrary"),
                                             vmem_limit_bytes=VMEM_LIMIT),
        name="mixer_out",
    )(x2d, a, b, c, *weights_a, state)


def _rope_tables(pos, reps, feature_major_q):
    half = ROPE_DIM // 2
    inv = 1.0 / (ROPE_THETA ** (jnp.arange(0, ROPE_DIM, 2, dtype=F32) / ROPE_DIM))
    ang = pos.astype(F32)[:, None] * inv[None, :]
    cos, sin = jnp.cos(ang), jnp.sin(ang)
    n = pos.shape[0]
    zeros = jnp.zeros((n, HEAD_BLK - ROPE_DIM), F32)
    tkc = jnp.concatenate([cos, cos, zeros], axis=1)
    tks = jnp.concatenate([sin, sin, zeros], axis=1)
    tqc = jnp.concatenate([cos, cos, jnp.ones((n, NOPE_DIM), F32),
                           jnp.zeros((n, HEAD_BLK - ROPE_DIM - NOPE_DIM), F32)], axis=1)
    tqc = jnp.concatenate([tqc, tqc], axis=1) * (ATTN_SCALE * LOG2E)
    tqs = jnp.concatenate([tks, tks], axis=1) * (ATTN_SCALE * LOG2E)
    if feature_major_q:
        return ((cos * (ATTN_SCALE * LOG2E)).T, (sin * (ATTN_SCALE * LOG2E)).T, tkc, tks)
    return tuple(jnp.tile(t, (reps, 1)) for t in (tqc, tqs, tkc, tks))


def _block_diag(w):
    depth, n, d, e = w.shape
    return jnp.einsum('lnde,nm->lndme', w, jnp.eye(n, dtype=w.dtype)).reshape(depth, n * d, n * e)


def _prep_weights(p):
    half = ROPE_DIM // 2
    depth = p['w_in'].shape[0]
    u, v, cq, ckv, kr, xc, gate = jnp.split(p['w_in'], [256, 512, 896, 1152, 1184, 1440], axis=2)
    kr1, kr2 = kr[..., :half], kr[..., half:]
    z = jnp.zeros((depth, D_MODEL, HEAD_BLK - ROPE_DIM), F32)
    w_in = jnp.concatenate([u, v, cq, ckv, kr1, kr2, z, -kr2, kr1, z, xc, gate], axis=2)

    wq = p['mla_w_uq'].reshape(depth, Q_LORA, N_B_HEADS, NOPE_DIM + ROPE_DIM)
    nope, r1, r2 = wq[..., :NOPE_DIM], wq[..., NOPE_DIM:NOPE_DIM + half], wq[..., NOPE_DIM + half:]
    zq = lambda k: jnp.zeros((depth, Q_LORA, N_B_HEADS, k), F32)
    wq_blk = jnp.concatenate([r1, r2, nope, zq(HEAD_BLK - ROPE_DIM - NOPE_DIM)], axis=-1)
    wq_blk = wq_blk.reshape(depth, Q_LORA, -1).astype(BF16)
    wq_swp = jnp.concatenate([-r2, r1, zq(HEAD_BLK - ROPE_DIM)], axis=-1).reshape(depth, Q_LORA, -1).astype(BF16)

    w_uk, w_uv = p['mla_w_uk'], p['mla_w_uv']
    zk = jnp.zeros((depth, KV_LORA, N_B_HEADS, ROPE_DIM), F32)
    wk_pad = jnp.concatenate([zk, w_uk, zk], axis=-1).reshape(depth, KV_LORA, -1)
    zt = jnp.zeros((depth, N_B_HEADS, ROPE_DIM, KV_LORA), F32)
    wkt = jnp.concatenate([zt, jnp.transpose(w_uk, (0, 2, 3, 1)), zt], axis=2)
    wv_pad = jnp.einsum('lchd,hg->lhcgd', w_uv, jnp.eye(N_B_HEADS, dtype=F32)).reshape(
        depth, N_B_HEADS, KV_LORA, D_B)
    w_uv2 = w_uv.reshape(depth, KV_LORA, -1).astype(BF16)

    row = lambda a: a.reshape(depth, 1, -1)
    return dict(
        w_in=w_in.astype(BF16),
        wq_blk=wq_blk, wq_swp=wq_swp,
        wq_blk_t=jnp.swapaxes(wq_blk, 1, 2), wq_swp_t=jnp.swapaxes(wq_swp, 1, 2),
        wk_pad=wk_pad.astype(BF16), w_uv=w_uv2, w_uv_t=jnp.swapaxes(w_uv2, 1, 2),
        wkt=wkt.astype(BF16), wv_pad=wv_pad.astype(BF16),
        w_r=_block_diag(p['lru_w_r']).astype(BF16), w_i=_block_diag(p['lru_w_i']).astype(BF16),
        q_g=row(p['mla_q_norm_g']), kv_g=row(p['mla_kv_norm_g']),
        conv_w=p['lru_conv_w'], conv_b=row(p['lru_conv_b']),
        b_r=row(p['lru_b_r']), b_i=row(p['lru_b_i']), lam=row(p['lru_lam']),
        w_o=p['w_o'].astype(BF16),
        ln1_g=row(p['ln1_g']), ln1_b=row(p['ln1_b']), ln2_g=row(p['ln2_g']), ln2_b=row(p['ln2_b']),
        w_up=p['ffn_w_up'].astype(BF16),
        ffn_cw=jnp.pad(p['ffn_conv_w'], ((0, 0), (0, SUBLANE - FFN_CONV), (0, 0))),
        ffn_cb=row(p['ffn_conv_b']),
        w_down=p['ffn_w_down'].astype(BF16),
    )


def _gmlp_params(p, ln):
    w_s = p['gmlp_w_s'][:, :, :ln, :ln]
    b_s = jnp.repeat(jnp.swapaxes(p['gmlp_b_s'][:, :, :ln], 1, 2), A_HEAD, axis=2)
    return w_s, b_s


def _ffn_state_in(st, nb):
    n, steps, width = st.shape
    if nb == 1:
        return jnp.pad(st[:, :, None, :], ((0, 0), (0, 0), (0, SUBLANE - 1), (0, 0))).reshape(n, -1, width)
    return jnp.swapaxes(st.reshape(n // nb, nb, steps, width), 1, 2).reshape(n // nb, -1, width)


def _ffn_state_out(st, nb):
    n, _, width = st.shape
    st = st.reshape(n, FFN_CONV - 1, SUBLANE, width)
    if nb == 1:
        return st[:, :, 0, :]
    return jnp.swapaxes(st, 1, 2).reshape(n * nb, FFN_CONV - 1, width)


PROMPT_TT_IN = 512
PROMPT_TT_OUT = 256
PROMPT_TQ = 512
PROMPT_TK = 512
SAMPLE_NB = 8
SAMPLE_TK = 1024


def kernel(x_prompt, x_sample, cache_kv_latent, cache_k_rope, state_lru_h, state_lru_conv, state_ffn_conv,
           ln1_g, ln1_b, ln2_g, ln2_b, w_in, w_o, gmlp_w_s, gmlp_b_s, mla_q_norm_g, mla_w_uq,
           mla_kv_norm_g, mla_w_uk, mla_w_uv, lru_conv_w, lru_conv_b, lru_w_r, lru_b_r, lru_w_i, lru_b_i,
           lru_lam, ffn_w_up, ffn_conv_w, ffn_conv_b, ffn_w_down):
    p = dict(ln1_g=ln1_g, ln1_b=ln1_b, ln2_g=ln2_g, ln2_b=ln2_b, w_in=w_in, w_o=w_o, gmlp_w_s=gmlp_w_s,
             gmlp_b_s=gmlp_b_s, mla_q_norm_g=mla_q_norm_g, mla_w_uq=mla_w_uq, mla_kv_norm_g=mla_kv_norm_g,
             mla_w_uk=mla_w_uk, mla_w_uv=mla_w_uv, lru_conv_w=lru_conv_w, lru_conv_b=lru_conv_b,
             lru_w_r=lru_w_r, lru_b_r=lru_b_r, lru_w_i=lru_w_i, lru_b_i=lru_b_i, lru_lam=lru_lam,
             ffn_w_up=ffn_w_up, ffn_conv_w=ffn_conv_w, ffn_conv_b=ffn_conv_b, ffn_w_down=ffn_w_down)
    bp, s_len, _ = x_prompt.shape
    bd, t_len, _ = x_sample.shape
    past = cache_kv_latent.shape[2]
    depth = w_in.shape[0]
    ln_p, ln_d = min(s_len, GMLP_CHUNK), min(t_len, GMLP_CHUNK)

    tabs_p = _rope_tables(jnp.arange(s_len), 1, True)
    tabs_d = _rope_tables(past + jnp.arange(t_len), SAMPLE_NB, False)
    zero_rows = jnp.zeros((bp, SUBLANE, D_C), F32)
    zero_ffn = jnp.zeros((bp, FFN_STATE_ROWS, 2 * D_FF), F32)

    xp = x_prompt.reshape(bp * s_len, D_MODEL)
    xd = x_sample.reshape(bd * t_len, D_MODEL)
    cache_kr_t = jnp.swapaxes(cache_k_rope, 2, 3)
    lw = _prep_weights(p)
    lw_p =dict(lw, **dict(zip(('w_s', 'b_s'), _gmlp_params(p, ln_p))))
    lw_d = dict(lw, **dict(zip(('w_s', 'b_s'), _gmlp_params(p, ln_d))))
    outs = {i: [] for i in (2, 3, 4, 7, 8, 9, 10)}
    stacked_p = stacked_d = None
    for l in range(depth):
        a, c, q, lat_p, kr_p, hst, cst, k, v = _mixer_in(
            xp, lw_p, l, tabs_p, zero_rows, zero_rows, stacked_p,
            n_seq=bp, seq_len=s_len, nb=1, tt=PROMPT_TT_IN, ln=ln_p, prompt=True)
        stacked_p = (lat_p, kr_p)
        b = _attn_prompt(q, k, v, n_seq=bp, seq_len=s_len, tq=PROMPT_TQ, tk=PROMPT_TK)
        xp, ffn_st = _mixer_out(xp, a, b, c, lw, l, zero_ffn,
                                n_seq=bp, seq_len=s_len, nb=1, tt=PROMPT_TT_OUT)
        outs[2].append(hst[:, 0, :])
        outs[3].append(cst[:, SUBLANE - (LRU_CONV - 1):, :])
        outs[4].append(_ffn_state_out(ffn_st, 1))

        h0 = jnp.broadcast_to(state_lru_h[l][:, None, :], (bd, SUBLANE, D_C))
        conv0 = jnp.pad(state_lru_conv[l], ((0, 0), (SUBLANE - (LRU_CONV - 1), 0), (0, 0)))
        ffn_st0 = _ffn_state_in(state_ffn_conv[l], SAMPLE_NB)
        a, c, q, lat_d, kr_d, hst, cst, vg = _mixer_in(
            xd, lw_d, l, tabs_d, h0, conv0, stacked_d,
            n_seq=bd, seq_len=t_len, nb=SAMPLE_NB, tt=t_len, ln=ln_d, prompt=False)
        stacked_d = (lat_d, kr_d)
        b = _attn_sample(q, cache_kv_latent, cache_kr_t, lat_d, kr_d, lw['wkt'], lw['wv_pad'],
                         layer=l, n_seq=bd, t_new=t_len, tk=SAMPLE_TK)
        xd, ffn_st = _mixer_out(xd, a, b, c, lw, l, ffn_st0, n_seq=bd, seq_len=t_len, nb=SAMPLE_NB, tt=t_len)
        outs[7].append(vg.reshape(bd, t_len, D_A))
        outs[8].append(hst[:, 0, :])
        outs[9].append(cst[:, SUBLANE - (LRU_CONV - 1):, :])
        outs[10].append(_ffn_state_out(ffn_st, SAMPLE_NB))

    st = {i: jnp.stack(o) for i, o in outs.items()}
    return (xp.reshape(bp, s_len, D_MODEL), xd.reshape(bd, t_len, D_MODEL),
            lat_p.reshape(depth, bp, s_len, KV_LORA), kr_p.reshape(depth, bp, s_len, ROPE_DIM),
            st[2], st[3], st[4],
            lat_d.reshape(depth, bd, t_len, KV_LORA), kr_d.reshape(depth, bd, t_len, ROPE_DIM),
            st[7], st[8], st[9], st[10])
```

```python
import functools
import math

import jax
import jax.numpy as jnp
from jax import lax
from jax.experimental import pallas as pl
from jax.experimental.pallas import tpu as pltpu

F32 = jnp.float32
BF16 = jnp.bfloat16

D_MODEL = 1024
DEPTH = 4
CHUNK = 64
GMLP_CHUNK = 128
D_A = 256
N_A_HEADS = 4
A_HEAD = 64
D_B = 512
N_B_HEADS = 8
V_DIM = 64
NOPE_DIM = 64
ROPE_DIM = 32
Q_LORA = 384
KV_LORA = 256
ROPE_THETA = 10000.0
ATTN_SCALE = (NOPE_DIM + ROPE_DIM) ** -0.5
D_C = 256
N_C_BLOCKS = 4
C_HEAD = 64
LRU_CONV = 4
LRU_C = 8.0
D_FF = 2816
FFN_CONV = 3
ALPHA = (2.0 * DEPTH) ** 0.25
LN_EPS = 1e-5
RMS_EPS = 1e-6

LANE = 128
SUBLANE = 8
HEAD_BLK = 128
VMEM_LIMIT = 56 * 1024 * 1024

C_U, C_V, C_CQ, C_CKV, C_KR, C_XC, C_GATE, C_END = 0, 256, 512, 896, 1152, 1408, 1664, 1920

FF_CHUNK = 256
ROW_BLK = 64
DOWN_GROUP = 3
NEXT_PROLOGUE_AT = 1
N_FF = D_FF // FF_CHUNK


def _dot(a, b):
    return jnp.dot(a, b, preferred_element_type=F32)


def _dot_nt(a, b):
    return lax.dot_general(a, b, (((1,), (1,)), ((), ())), preferred_element_type=F32)


def _rms_norm(x, g):
    ms = jnp.mean(x * x, axis=-1, keepdims=True)
    return x * lax.rsqrt(ms + RMS_EPS) * g


def _layer_norm(x, g, b):
    mu = jnp.mean(x, axis=-1, keepdims=True)
    xc = x - mu
    var = jnp.mean(xc * xc, axis=-1, keepdims=True)
    return xc * lax.rsqrt(var + LN_EPS) * g + b


def _mixer_in_kernel(nb, tt, ln, prompt, n_aliased,
                     x_ref, win_ref, wqb_ref, wqs_ref, wkp_ref, wuv_ref, wr_ref, wi_ref,
                     ws_ref, bs_ref, qg_ref, kvg_ref, cw_ref, cb_ref, br_ref, bi_ref, lam_ref,
                     tqc_ref, tqs_ref, tkc_ref, tks_ref, h0_ref, conv0_ref,
                     *rest):
    rest = rest[n_aliased:]
    if prompt:
        (a_ref, c_ref, q_ref, lat_ref, kr_ref, hst_ref, cst_ref, k_ref, v_ref,
         xb_ref, cv_ref, hc_ref) = rest
        vg_ref = None
    else:
        (a_ref, c_ref, q_ref, lat_ref, kr_ref, hst_ref, cst_ref, vg_ref,
         xb_ref, cv_ref, hc_ref) = rest
        k_ref = v_ref = None
    rows = nb * tt
    t_idx = pl.program_id(1)

    @pl.when(t_idx == 0)
    def _():
        cv_ref[:, 0:SUBLANE, :] = conv0_ref[...]
        hc_ref[...] = h0_ref[...]

    xb_ref[...] = x_ref[...].astype(BF16)

    def proj(c0, c1):
        return _dot(xb_ref[...], win_ref[:, c0:c1])

    xc_in = proj(C_XC, C_GATE)
    z_gate = proj(C_GATE, C_END)
    z_u = proj(C_U, C_V)
    z_v = proj(C_V, C_CQ)

    xcs = []
    for s in range(nb):
        cv_ref[s, SUBLANE:SUBLANE + tt, :] = xc_in[s * tt:(s + 1) * tt]
        acc = cb_ref[...]
        for j in range(LRU_CONV):
            off = SUBLANE - (LRU_CONV - 1) + j
            acc = acc + cv_ref[s, off:off + tt, :] * cw_ref[j:j + 1, :]
        xcs.append(acc)
        tail = cv_ref[s, tt:tt + SUBLANE, :]
        cst_ref[s] = tail
        cv_ref[s, 0:SUBLANE, :] = tail
    xc = xcs[0] if nb == 1 else jnp.concatenate(xcs, axis=0)
    xcb = xc.astype(BF16)
    z_r = _dot(xcb, wr_ref[...])
    z_i = _dot(xcb, wi_ref[...])
    z_cq = proj(C_CQ, C_CKV)
    z_ckv = proj(C_CKV, C_KR)
    kr2 = proj(C_KR, C_XC)

    gpb = ln // SUBLANE
    sub = lax.broadcasted_iota(jnp.int32, (gpb, SUBLANE, D_C), 1)
    soft_lam = jax.nn.softplus(-lam_ref[...])
    chain = {}

    def lru_block(c):
        r0 = c * ln
        seq, first = divmod(r0, tt)
        r = jax.nn.sigmoid(z_r[r0:r0 + ln] + br_ref[...])
        ig = jax.nn.sigmoid(z_i[r0:r0 + ln] + bi_ref[...])
        log_a = (-LRU_C) * r * soft_lam
        a = jnp.exp(log_a)
        b_in = jnp.sqrt(-jnp.tanh(log_a) * (a * a + 1.0)) * (ig * xc[r0:r0 + ln])
        a3 = a.reshape(gpb, SUBLANE, D_C)
        b3 = b_in.reshape(gpb, SUBLANE, D_C)
        for k in (1, 2, 4):
            keep = sub >= k
            a_sh = jnp.where(keep, pltpu.roll(a3, k, 1), 1.0)
            b_sh = jnp.where(keep, pltpu.roll(b3, k, 1), 0.0)
            b3 = a3 * b_sh + b3
            a3 = a3 * a_sh
        hb = hc_ref[seq] if first == 0 else chain['h']
        h_rows = []
        for g in range(gpb):
            hr = a3[g] * hb + b3[g]
            h_rows.append(hr)
            hb = jnp.broadcast_to(hr[SUBLANE - 1:SUBLANE, :], (SUBLANE, D_C))
        chain['h'] = hb
        if first + ln == tt:
            hc_ref[seq] = hb
            hst_ref[seq] = hb
        c_ref[r0:r0 + ln, :] = (jnp.concatenate(h_rows, axis=0)
                                * jax.nn.gelu(z_gate[r0:r0 + ln])).astype(BF16)

    cqn = _rms_norm(z_cq, qg_ref[...]).astype(BF16)
    tqc = tqc_ref[...]
    tqs = tqs_ref[...]
    ckvn = _rms_norm(z_ckv, kvg_ref[...])
    lat_ref[...] = ckvn
    ckvn_b = ckvn.astype(BF16)
    kr_rot = kr2[:, :HEAD_BLK] * tkc_ref[...] + kr2[:, HEAD_BLK:] * tks_ref[...]
    kr_ref[...] = kr_rot[:, :ROPE_DIM]

    v = jax.nn.gelu(z_v)
    if vg_ref is not None:
        vg_ref[...] = v
    vb = v.astype(BF16)
    row_i = lax.broadcasted_iota(jnp.int32, (ln, ln), 0)
    col_i = lax.broadcasted_iota(jnp.int32, (ln, ln), 1)
    w_tril = [jnp.where(row_i >= col_i, ws_ref[h], 0.0).astype(BF16) for h in range(N_A_HEADS)]
    head_of_lane = lax.broadcasted_iota(jnp.int32, (ln, D_A), 1) // A_HEAD

    def gmlp_chunk(c):
        r0 = c * ln
        vc = vb[r0:r0 + ln]
        gate = bs_ref[...]
        for h in range(N_A_HEADS):
            gate = gate + _dot(w_tril[h], jnp.where(head_of_lane == h, vc, jnp.zeros_like(vc)))
        a_ref[r0:r0 + ln, :] = (jax.nn.gelu(z_u[r0:r0 + ln]) * gate).astype(BF16)

    def q_pair(j):
        c0, c1 = 2 * j * HEAD_BLK, 2 * (j + 1) * HEAD_BLK
        if prompt:
            qb = _dot_nt(wqb_ref[c0:c1, :], cqn)
            half = ROPE_DIM // 2
            for i in range(2):
                r0 = i * HEAD_BLK
                r1, r2 = qb[r0:r0 + half], qb[r0 + half:r0 + ROPE_DIM]
                rest = qb[r0 + ROPE_DIM:r0 + HEAD_BLK] * (ATTN_SCALE * LOG2E)
                blk = jnp.concatenate([r1 * tqc - r2 * tqs, r1 * tqs + r2 * tqc, rest], axis=0)
                q_ref[c0 + r0:c0 + r0 + HEAD_BLK, :] = blk.astype(BF16)
        else:
            q_ref[:, c0:c1] = (_dot(cqn, wqb_ref[:, c0:c1]) * tqc
                               + _dot(cqn, wqs_ref[:, c0:c1]) * tqs).astype(BF16)

    def k_pair(j):
        c0 = 2 * j * HEAD_BLK
        kp = _dot(ckvn_b, wkp_ref[:, c0:c0 + 2 * HEAD_BLK])
        k_ref[:, c0:c0 + HEAD_BLK] = (kp[:, :HEAD_BLK] + kr_rot).astype(BF16)
        k_ref[:, c0 + HEAD_BLK:c0 + 2 * HEAD_BLK] = (kp[:, HEAD_BLK:] + kr_rot).astype(BF16)

    def v_all():
        v_t = _dot_nt(wuv_ref[...], ckvn_b).astype(BF16)
        for h in range(N_B_HEADS):
            v_ref[h * V_AUG:h * V_AUG + V_DIM, :] = v_t[h * V_DIM:(h + 1) * V_DIM]
            v_ref[h * V_AUG + V_DIM:(h + 1) * V_AUG, :] = jnp.ones((V_ONES, rows), BF16)

    n_blocks = rows // ln
    mla = [functools.partial(q_pair, j) for j in range(N_B_HEADS // 2)]
    if prompt:
        mla += [functools.partial(k_pair, j) for j in range(N_B_HEADS // 2)] + [v_all]
    per_block = -(-len(mla) // n_blocks)
    for c in range(n_blocks):
        for piece in mla[c * per_block:(c + 1) * per_block]:
            piece()
        gmlp_chunk(c)
        lru_block(c)


def _layer_spec(w, layer, single_buffer=False):
    tail = (0,) * (w.ndim - 1)
    mode = dict(pipeline_mode=pl.Buffered(1)) if single_buffer else {}
    return pl.BlockSpec((None,) + w.shape[1:], lambda *_: (layer,) + tail, **mode)


def _mixer_in(x2d, lw, layer, tabs, h0, conv0, stacked, *, n_seq, seq_len, nb, tt, ln, prompt):
    depth = lw['w_in'].shape[0]
    nt = seq_len // tt
    assert nb == 1 or nt == 1
    rows = nb * tt
    n_rows = n_seq * seq_len
    grid = (n_seq // nb, nt)
    row_map = lambda b, t: (b * nt + t, 0)
    layer_row_map = lambda b, t: (layer, b * nt + t, 0)
    tab_map = lambda b, t: (t, 0)
    seq_map = lambda b, t: (b, 0, 0)

    sfx = '_t' if prompt else ''
    weights = [lw['w_in'], lw['wq_blk' + sfx], lw['wq_swp' + sfx], lw['wk_pad'], lw['w_uv' + sfx], lw['w_r'], lw['w_i'],
               lw['w_s'], lw['b_s'], lw['q_g'], lw['kv_g'], lw['conv_w'], lw['conv_b'], lw['b_r'],
               lw['b_i'], lw['lam']]
    col_map = lambda b, t: (0, b * nt + t)
    if prompt:
        q_tab_spec = pl.BlockSpec((ROPE_DIM // 2, rows), lambda b, t: (0, t))
        q_shape = jax.ShapeDtypeStruct((N_B_HEADS * HEAD_BLK, n_rows), BF16)
        q_spec = pl.BlockSpec((N_B_HEADS * HEAD_BLK, rows), col_map)
    else:
        q_tab_spec = pl.BlockSpec((rows, 2 * HEAD_BLK), tab_map)
        q_shape = jax.ShapeDtypeStruct((n_rows, N_B_HEADS * HEAD_BLK), BF16)
        q_spec = pl.BlockSpec((rows, N_B_HEADS * HEAD_BLK), row_map)
    in_specs = ([pl.BlockSpec((rows, D_MODEL), row_map)]
                + [_layer_spec(w, layer) for w in weights]
                + [q_tab_spec, q_tab_spec,
                   pl.BlockSpec((rows, HEAD_BLK), tab_map), pl.BlockSpec((rows, HEAD_BLK), tab_map),
                   pl.BlockSpec((nb, SUBLANE, D_C), seq_map), pl.BlockSpec((nb, SUBLANE, D_C), seq_map)])
    out_shape = [jax.ShapeDtypeStruct((n_rows, D_A), BF16),
                 jax.ShapeDtypeStruct((n_rows, D_C), BF16),
                 q_shape,
                 jax.ShapeDtypeStruct((depth, n_rows, KV_LORA), F32),
                 jax.ShapeDtypeStruct((depth, n_rows, ROPE_DIM), F32),
                 jax.ShapeDtypeStruct((n_seq, SUBLANE, D_C), F32),
                 jax.ShapeDtypeStruct((n_seq, SUBLANE, D_C), F32)]
    out_specs = [pl.BlockSpec((rows, D_A), row_map), pl.BlockSpec((rows, D_C), row_map),
                 q_spec, pl.BlockSpec((None, rows, KV_LORA), layer_row_map),
                 pl.BlockSpec((None, rows, ROPE_DIM), layer_row_map),
                 pl.BlockSpec((nb, SUBLANE, D_C), seq_map), pl.BlockSpec((nb, SUBLANE, D_C), seq_map)]
    if prompt:
        out_shape += [jax.ShapeDtypeStruct((n_rows, N_B_HEADS * HEAD_BLK), BF16),
                      jax.ShapeDtypeStruct((N_B_HEADS * V_AUG, n_rows), BF16)]
        out_specs += [pl.BlockSpec((rows, N_B_HEADS * HEAD_BLK), row_map),
                      pl.BlockSpec((N_B_HEADS * V_AUG, rows), col_map)]
    else:
        out_shape += [jax.ShapeDtypeStruct((n_rows, D_A), F32)]
        out_specs += [pl.BlockSpec((rows, D_A), row_map)]
    scratch = [pltpu.VMEM((rows, D_MODEL), BF16),
               pltpu.VMEM((nb, tt + SUBLANE, D_C), F32),
               pltpu.VMEM((nb, SUBLANE, D_C), F32)]
    stacked = tuple(stacked or ())
    n_in = len(in_specs)
    in_specs += [pl.BlockSpec(memory_space=pl.ANY)] * len(stacked)
    aliases = {n_in + i: 3 + i for i in range(len(stacked))}
    return pl.pallas_call(
        functools.partial(_mixer_in_kernel, nb, tt, ln, prompt, len(stacked)),
        grid=grid, in_specs=in_specs, out_specs=out_specs, out_shape=out_shape,
        scratch_shapes=scratch, input_output_aliases=aliases,
        compiler_params=pltpu.CompilerParams(dimension_semantics=("arbitrary", "arbitrary"),
                                             vmem_limit_bytes=VMEM_LIMIT),
        name="mixer_in_prompt" if prompt else "mixer_in_sample",
    )(x2d, *weights, *tabs, h0, conv0, *stacked)


NEG_BIG = -1e30
LOG2E = math.log2(math.e)
V_ONES = 16
V_AUG = V_DIM + V_ONES
ATTN_HEAD_GROUP = 2


def _last_kv_tile(qi, tq, tk):
    return ((qi + 1) * tq - 1) // tk


def _kv_steps(nq, tq, tk):
    pairs = [(i, j) for i in range(nq) for j in range(_last_kv_tile(i, tq, tk) + 1)]
    return (jnp.asarray([p[0] for p in pairs], jnp.int32), jnp.asarray([p[1] for p in pairs], jnp.int32))


def _attn_prompt_kernel(tq, tk, qi_tab, ki_tab, q_ref, k_ref, v_ref, o_ref, m_ref, acc_ref):
    step_idx = pl.program_id(1)
    qi = qi_tab[step_idx]
    ki = ki_tab[step_idx]
    last = _last_kv_tile(qi, tq, tk)

    @pl.when(ki == 0)
    def _():
        m_ref[...] = jnp.full(m_ref.shape, NEG_BIG, F32)
        acc_ref[...] = jnp.zeros(acc_ref.shape, F32)

    half = tq // 2

    def step(diag):
        if diag:
            local = (lax.broadcasted_iota(jnp.int32, (half, half), 0) // CHUNK
                     <= lax.broadcasted_iota(jnp.int32, (half, half), 1) // CHUNK)
            parts = ((0, half, half), (half, tq, tk))
        else:
            parts = ((0, tq, tk),)

        def scores(h):
            blk = slice(h * HEAD_BLK, (h + 1) * HEAD_BLK)
            return [_dot(k_ref[0:nk, blk], q_ref[blk, c0:c1]) for (c0, c1, nk) in parts]

        def softmax(h, s_parts):
            out = []
            for (c0, c1, nk), s in zip(parts, s_parts):
                if diag:
                    low = jnp.where(local, s[nk - half:], NEG_BIG)
                    s = low if nk == half else jnp.concatenate([s[:nk - half], low], axis=0)
                m_prev = m_ref[h, :, c0:c1]
                m_new = jnp.maximum(m_prev, jnp.max(s, axis=0, keepdims=True))
                m_ref[h, :, c0:c1] = m_new
                out.append((jnp.exp2(m_prev - m_new), jnp.exp2(s - m_new).astype(BF16)))
            return out

        group = ATTN_HEAD_GROUP
        s_next = [scores(h) for h in range(group)]
        for h0 in range(0, N_B_HEADS, group):
            s_cur = s_next
            if h0 + group < N_B_HEADS:
                s_next = [scores(h) for h in range(h0 + group, h0 + 2 * group)]
            probs = [softmax(h0 + i, s_cur[i]) for i in range(group)]
            for i, prob_parts in enumerate(probs):
                rows = slice((h0 + i) * V_AUG, (h0 + i + 1) * V_AUG)
                for (c0, c1, nk), (alpha, p) in zip(parts, prob_parts):
                    acc_ref[rows, c0:c1] = alpha * acc_ref[rows, c0:c1] + _dot(v_ref[rows, 0:nk], p)

    @pl.when(ki == qi)
    def _():
        step(True)

    @pl.when(ki != qi)
    def _():
        step(False)

    @pl.when(ki == last)
    def _():
        for j in range(N_B_HEADS // 2):
            halves = []
            for h in (2 * j, 2 * j + 1):
                denom = acc_ref[h * V_AUG + V_DIM:h * V_AUG + V_DIM + 1, :]
                halves.append(acc_ref[h * V_AUG:h * V_AUG + V_DIM, :] / denom)
            o_ref[:, j * LANE:(j + 1) * LANE] = jnp.concatenate(halves, axis=0).T.astype(BF16)


def _attn_prompt(q_t, k, v_t, *, n_seq, seq_len, tq, tk):
    assert tq == tk and (tq // 2) % CHUNK == 0
    nq, nk = seq_len // tq, seq_len // tk
    qi_tab, ki_tab = _kv_steps(nq, tq, tk)
    grid_spec = pltpu.PrefetchScalarGridSpec(
        num_scalar_prefetch=2,
        grid=(n_seq, qi_tab.shape[0]),
        in_specs=[pl.BlockSpec((N_B_HEADS * HEAD_BLK, tq), lambda b, s, qt, kt: (0, b * nq + qt[s])),
                  pl.BlockSpec((tk, N_B_HEADS * HEAD_BLK), lambda b, s, qt, kt: (b * nk + kt[s], 0)),
                  pl.BlockSpec((N_B_HEADS * V_AUG, tk), lambda b, s, qt, kt: (0, b * nk + kt[s]))],
        out_specs=pl.BlockSpec((tq, D_B), lambda b, s, qt, kt: (b * nq + qt[s], 0)),
        scratch_shapes=[pltpu.VMEM((N_B_HEADS, 1, tq), F32),
                        pltpu.VMEM((N_B_HEADS * V_AUG, tq), F32)])
    return pl.pallas_call(
        functools.partial(_attn_prompt_kernel, tq, tk),
        grid_spec=grid_spec,
        out_shape=jax.ShapeDtypeStruct((n_seq * seq_len, D_B), BF16),
        compiler_params=pltpu.CompilerParams(dimension_semantics=("arbitrary", "arbitrary"),
                                             vmem_limit_bytes=VMEM_LIMIT),
        name="attn_prompt",
    )(qi_tab, ki_tab, q_t, k, v_t)


def _attn_sample_kernel(t_new, tk, q_ref, clat_ref, ckr_ref, nlat_ref, nkr_ref, wkt_ref, wv_ref, o_ref):
    past = clat_ref.shape[0]
    ql, qr = [], []
    for h in range(N_B_HEADS):
        q_h = q_ref[:, h * HEAD_BLK:(h + 1) * HEAD_BLK]
        ql.append(_dot(q_h, wkt_ref[h]).astype(BF16))
        qr.append(q_h[:, :ROPE_DIM])
    ql = jnp.concatenate(ql, axis=0)
    qr = jnp.concatenate(qr, axis=0)

    def scores(i):
        if i == 0:
            c_b = nlat_ref[...].astype(BF16)
            return _dot_nt(ql, c_b) + _dot_nt(qr, nkr_ref[...].astype(BF16)), c_b
        r0 = (i - 1) * tk
        c_b = clat_ref[r0:r0 + tk, :].astype(BF16)
        return _dot_nt(ql, c_b) + _dot(qr, ckr_ref[:, r0:r0 + tk].astype(BF16)), c_b

    rows = N_B_HEADS * t_new
    m = jnp.full((rows, 1), NEG_BIG, F32)
    l = jnp.zeros((rows, 1), F32)
    acc = jnp.zeros((rows, KV_LORA), F32)
    n_tiles = 1 + past // tk
    nxt = scores(0)
    for i in range(n_tiles):
        s, c_b = nxt
        if i + 1 < n_tiles:
            nxt = scores(i + 1)
        m_new = jnp.maximum(m, jnp.max(s, axis=1, keepdims=True))
        alpha = jnp.exp2(m - m_new)
        p = jnp.exp2(s - m_new)
        l = alpha * l + jnp.sum(p, axis=1, keepdims=True)
        acc = alpha * acc + _dot(p.astype(BF16), c_b)
        m = m_new
    o_lat = (acc / l).astype(BF16)
    out = _dot(o_lat[0:t_new], wv_ref[0])
    for h in range(1, N_B_HEADS):
        out = out + _dot(o_lat[h * t_new:(h + 1) * t_new], wv_ref[h])
    o_ref[...] = out.astype(BF16)


def _attn_sample(q, cache_lat, cache_kr, new_lat, new_kr, wkt, wv_pad, *, layer, n_seq, t_new, tk):
    past = cache_lat.shape[2]
    assert past % tk == 0
    row_map = lambda b: (b, 0)
    return pl.pallas_call(
        functools.partial(_attn_sample_kernel, t_new, tk),
        grid=(n_seq,),
        in_specs=[pl.BlockSpec((t_new, N_B_HEADS * HEAD_BLK), row_map),
                  pl.BlockSpec((None, None, past, KV_LORA), lambda b: (layer, b, 0, 0)),
                  pl.BlockSpec((None, None, ROPE_DIM, past), lambda b: (layer, b, 0, 0)),
                  pl.BlockSpec((None, t_new, KV_LORA), lambda b: (layer, b, 0)),
                  pl.BlockSpec((None, t_new, ROPE_DIM), lambda b: (layer, b, 0)),
                  _layer_spec(wkt, layer), _layer_spec(wv_pad, layer)],
        out_specs=pl.BlockSpec((t_new, D_B), row_map),
        out_shape=jax.ShapeDtypeStruct((n_seq * t_new, D_B), BF16),
        compiler_params=pltpu.CompilerParams(dimension_semantics=("arbitrary",),
                                             vmem_limit_bytes=VMEM_LIMIT),
        name="attn_sample",
    )(q, cache_lat, cache_kr, new_lat, new_kr, wkt, wv_pad)


def _mixer_out_kernel(nb, tt, n_sub,
                      x_ref, a_ref, b_ref, c_ref, wo_ref, g1_ref, b1_ref,
                      wu_ref, cw_ref, cb_ref, wd_ref, g2_ref, b2_ref, s_ref,
                      y_ref, o_ref,
                      x1_ref, xp_ref, x1b_ref, acc_ref, bufg0_ref, bufg1_ref, bufv0_ref, bufv1_ref,
                      car_ref, act0_ref, act1_ref):
    bufg_ref = (bufg0_ref, bufg1_ref)
    bufv_ref = (bufv0_ref, bufv1_ref)
    act_ref = (act0_ref, act1_ref)
    t_idx = pl.program_id(1)
    rows = nb * tt
    n_tiles = rows // SUBLANE
    halo = (FFN_CONV - 1) * SUBLANE

    @pl.when(t_idx == 0)
    def _():
        car_ref[...] = s_ref[...]

    n_items = n_sub * N_FF

    def prologue(t):
        rs = slice(t * rows, (t + 1) * rows)
        mix = (_dot(a_ref[rs, :], wo_ref[0:D_A, :]) + _dot(b_ref[rs, :], wo_ref[D_A:D_A + D_B, :])
               + _dot(c_ref[rs, :], wo_ref[D_A + D_B:, :]))
        x1 = _layer_norm(ALPHA * x_ref[rs, :] + mix, g1_ref[...], b1_ref[...])
        x1_ref[rs, :] = x1
        for c in range(D_MODEL // LANE):
            for k in range(n_tiles):
                xp_ref[c, pl.ds(_interleave_start(k, n_tiles), SUBLANE, stride=SUBLANE), :] = (
                    x1[k * SUBLANE:(k + 1) * SUBLANE, c * LANE:(c + 1) * LANE])
            x1b_ref[t, :, c * LANE:(c + 1) * LANE] = xp_ref[c].astype(BF16)

    def epilogue(t):
        rs = slice(t * rows, (t + 1) * rows)
        ffn = jnp.concatenate(
            [jnp.concatenate([acc_ref[t, c, pl.ds(_interleave_start(k, n_tiles), SUBLANE, stride=SUBLANE), :]
                              for k in range(n_tiles)], axis=0)
             for c in range(D_MODEL // LANE)], axis=1)
        y_ref[rs, :] = _layer_norm(ALPHA * x1_ref[rs, :] + ffn, g2_ref[...], b2_ref[...])

    def cols(j, value_half):
        c0 = (D_FF if value_half else 0) + j * FF_CHUNK
        return slice(c0, c0 + FF_CHUNK)

    first_sublane = lax.broadcasted_iota(jnp.int32, (SUBLANE, FF_CHUNK), 0) == 0

    def stage(up, buf, cs):
        buf[halo:halo + rows, :] = up
        for i in range(FFN_CONV - 1):
            carry_rows = slice(i * SUBLANE, (i + 1) * SUBLANE)
            last = up[rows - halo + i * SUBLANE:rows - halo + (i + 1) * SUBLANE]
            if nb == 1:
                last = pltpu.roll(last, 1, 0)
                prev = jnp.where(first_sublane, car_ref[carry_rows, cs], last)
            else:
                prev = car_ref[carry_rows, cs]
            buf[carry_rows, :] = prev
            car_ref[carry_rows, cs] = last
            o_ref[carry_rows, cs] = last

    def conv_rows(buf, cs, r0, nr):
        acc = cb_ref[:, cs]
        for i in range(FFN_CONV):
            acc = acc + buf[r0 + i * SUBLANE:r0 + i * SUBLANE + nr, :] * cw_ref[i:i + 1, cs]
        return acc

    def up_proj(i):
        t, j = divmod(i, N_FF)
        xb = x1b_ref[t]
        stage(_dot(xb, wu_ref[:, cols(j, False)]), bufg_ref[i % 2], cols(j, False))
        stage(_dot(xb, wu_ref[:, cols(j, True)]), bufv_ref[i % 2], cols(j, True))

    groups_per_tile = -(-N_FF // DOWN_GROUP)

    def act_slot(t, grp):
        return act_ref[(t * groups_per_tile + grp) % 2]

    def elementwise(i):
        t, j = divmod(i, N_FF)
        grp, pos = divmod(j, DOWN_GROUP)
        for r0 in range(0, rows, ROW_BLK):
            nr = min(ROW_BLK, rows - r0)
            gate = conv_rows(bufg_ref[i % 2], cols(j, False), r0, nr)
            val = conv_rows(bufv_ref[i % 2], cols(j, True), r0, nr)
            act_slot(t, grp)[r0:r0 + nr, pos * FF_CHUNK:(pos + 1) * FF_CHUNK] = (
                jax.nn.gelu(gate) * val).astype(BF16)

    def down_proj(t, grp):
        n_chunks = min(DOWN_GROUP, N_FF - grp * DOWN_GROUP)
        k0, width = grp * DOWN_GROUP * FF_CHUNK, n_chunks * FF_CHUNK
        part = _dot(act_slot(t, grp)[:, 0:width], wd_ref[k0:k0 + width, :])
        for c in range(D_MODEL // LANE):
            if grp == 0:
                acc_ref[t, c] = part[:, c * LANE:(c + 1) * LANE]
            else:
                acc_ref[t, c] += part[:, c * LANE:(c + 1) * LANE]

    prologue(0)
    up_proj(0)
    pending = None
    for i in range(n_items + 1):
        if i + 1 < n_items:
            up_proj(i + 1)
        if i % N_FF == NEXT_PROLOGUE_AT and i // N_FF + 1 < n_sub:
            prologue(i // N_FF + 1)
        if pending is not None:
            down_proj(*pending)
            if pending[1] == groups_per_tile - 1:
                epilogue(pending[0])
            pending = None
        if i < n_items:
            elementwise(i)
            t, j = divmod(i, N_FF)
            if (j + 1) % DOWN_GROUP == 0 or j == N_FF - 1:
                pending = (t, j // DOWN_GROUP)


def _interleave_start(k, n_tiles):
    group, part = divmod(k, n_tiles // SUBLANE)
    return part * SUBLANE * SUBLANE + group


FFN_STATE_ROWS = (FFN_CONV - 1) * SUBLANE


def _mixer_out(x2d, a, b, c, lw, layer, state, *, n_seq, seq_len, nb, tt, n_sub):
    nt = seq_len // (tt * n_sub)
    assert nb == 1 or (nt == 1 and n_sub == 1)
    rows = nb * tt
    blk_rows = rows * n_sub
    assert nb in (1, SUBLANE) and rows % (SUBLANE * SUBLANE) == 0
    n_rows = n_seq * seq_len
    row_map = lambda bi, t: (bi * nt + t, 0)
    weights_a = [lw['w_o'], lw['ln1_g'], lw['ln1_b'], lw['w_up'], lw['ffn_cw'], lw['ffn_cb'],
                 lw['w_down'], lw['ln2_g'], lw['ln2_b']]
    st_spec = pl.BlockSpec((None, FFN_STATE_ROWS, 2 * D_FF), lambda bi, t: (bi, 0, 0))
    in_specs = ([pl.BlockSpec((blk_rows, D_MODEL), row_map), pl.BlockSpec((blk_rows, D_A), row_map),
                 pl.BlockSpec((blk_rows, D_B), row_map), pl.BlockSpec((blk_rows, D_C), row_map)]
                + [_layer_spec(w, layer, single_buffer=True) for w in weights_a]
                + [st_spec])
    st_shape = jax.ShapeDtypeStruct((n_seq // nb, FFN_STATE_ROWS, 2 * D_FF), F32)
    buf_rows = FFN_STATE_ROWS + rows
    return pl.pallas_call(
        functools.partial(_mixer_out_kernel, nb, tt, n_sub),
        grid=(n_seq // nb, nt),
        in_specs=in_specs,
        out_specs=[pl.BlockSpec((blk_rows, D_MODEL), row_map), st_spec],
        out_shape=[jax.ShapeDtypeStruct((n_rows, D_MODEL), F32), st_shape],
        scratch_shapes=[pltpu.VMEM((blk_rows, D_MODEL), F32),
                        pltpu.VMEM((D_MODEL // LANE, rows, LANE), F32),
                        pltpu.VMEM((n_sub, rows, D_MODEL), BF16),
                        pltpu.VMEM((n_sub, D_MODEL // LANE, rows, LANE), F32),
                        pltpu.VMEM((buf_rows, FF_CHUNK), F32),
                        pltpu.VMEM((buf_rows, FF_CHUNK), F32),
                        pltpu.VMEM((buf_rows, FF_CHUNK), F32),
                        pltpu.VMEM((buf_rows, FF_CHUNK), F32),
                        pltpu.VMEM((FFN_STATE_ROWS, 2 * D_FF), F32),
                        pltpu.VMEM((rows, DOWN_GROUP * FF_CHUNK), BF16),
                        pltpu.VMEM((rows, DOWN_GROUP * FF_CHUNK), BF16)],
        compiler_params=pltpu.CompilerParams(dimension_semantics=("arbitrary", "arbitrary"),
                                             vmem_limit_bytes=VMEM_LIMIT),
        name="mixer_out",
    )(x2d, a, b, c, *weights_a, state)


def _rope_tables(pos, reps, feature_major_q):
    half = ROPE_DIM // 2
    inv = 1.0 / (ROPE_THETA ** (jnp.arange(0, ROPE_DIM, 2, dtype=F32) / ROPE_DIM))
    ang = pos.astype(F32)[:, None] * inv[None, :]
    cos, sin = jnp.cos(ang), jnp.sin(ang)
    n = pos.shape[0]
    zeros = jnp.zeros((n, HEAD_BLK - ROPE_DIM), F32)
    tkc = jnp.concatenate([cos, cos, zeros], axis=1)
    tks = jnp.concatenate([sin, sin, zeros], axis=1)
    tqc = jnp.concatenate([cos, cos, jnp.ones((n, NOPE_DIM), F32),
                           jnp.zeros((n, HEAD_BLK - ROPE_DIM - NOPE_DIM), F32)], axis=1)
    tqc = jnp.concatenate([tqc, tqc], axis=1) * (ATTN_SCALE * LOG2E)
    tqs = jnp.concatenate([tks, tks], axis=1) * (ATTN_SCALE * LOG2E)
    if feature_major_q:
        return ((cos * (ATTN_SCALE * LOG2E)).T, (sin * (ATTN_SCALE * LOG2E)).T, tkc, tks)
    return tuple(jnp.tile(t, (reps, 1)) for t in (tqc, tqs, tkc, tks))


def _block_diag(w):
    depth, n, d, e = w.shape
    return jnp.einsum('lnde,nm->lndme', w, jnp.eye(n, dtype=w.dtype)).reshape(depth, n * d, n * e)


def _prep_weights(p):
    half = ROPE_DIM // 2
    depth = p['w_in'].shape[0]
    u, v, cq, ckv, kr, xc, gate = jnp.split(p['w_in'], [256, 512, 896, 1152, 1184, 1440], axis=2)
    kr1, kr2 = kr[..., :half], kr[..., half:]
    z = jnp.zeros((depth, D_MODEL, HEAD_BLK - ROPE_DIM), F32)
    w_in = jnp.concatenate([u, v, cq, ckv, kr1, kr2, z, -kr2, kr1, z, xc, gate], axis=2)

    wq = p['mla_w_uq'].reshape(depth, Q_LORA, N_B_HEADS, NOPE_DIM + ROPE_DIM)
    nope, r1, r2 = wq[..., :NOPE_DIM], wq[..., NOPE_DIM:NOPE_DIM + half], wq[..., NOPE_DIM + half:]
    zq = lambda k: jnp.zeros((depth, Q_LORA, N_B_HEADS, k), F32)
    wq_blk = jnp.concatenate([r1, r2, nope, zq(HEAD_BLK - ROPE_DIM - NOPE_DIM)], axis=-1)
    wq_blk = wq_blk.reshape(depth, Q_LORA, -1).astype(BF16)
    wq_swp = jnp.concatenate([-r2, r1, zq(HEAD_BLK - ROPE_DIM)], axis=-1).reshape(depth, Q_LORA, -1).astype(BF16)

    w_uk, w_uv = p['mla_w_uk'], p['mla_w_uv']
    zk = jnp.zeros((depth, KV_LORA, N_B_HEADS, ROPE_DIM), F32)
    wk_pad = jnp.concatenate([zk, w_uk, zk], axis=-1).reshape(depth, KV_LORA, -1)
    zt = jnp.zeros((depth, N_B_HEADS, ROPE_DIM, KV_LORA), F32)
    wkt = jnp.concatenate([zt, jnp.transpose(w_uk, (0, 2, 3, 1)), zt], axis=2)
    wv_pad = jnp.einsum('lchd,hg->lhcgd', w_uv, jnp.eye(N_B_HEADS, dtype=F32)).reshape(
        depth, N_B_HEADS, KV_LORA, D_B)
    w_uv2 = w_uv.reshape(depth, KV_LORA, -1).astype(BF16)

    row = lambda a: a.reshape(depth, 1, -1)
    return dict(
        w_in=w_in.astype(BF16),
        wq_blk=wq_blk, wq_swp=wq_swp,
        wq_blk_t=jnp.swapaxes(wq_blk, 1, 2), wq_swp_t=jnp.swapaxes(wq_swp, 1, 2),
        wk_pad=wk_pad.astype(BF16), w_uv=w_uv2, w_uv_t=jnp.swapaxes(w_uv2, 1, 2),
        wkt=wkt.astype(BF16), wv_pad=wv_pad.astype(BF16),
        w_r=_block_diag(p['lru_w_r']).astype(BF16), w_i=_block_diag(p['lru_w_i']).astype(BF16),
        q_g=row(p['mla_q_norm_g']), kv_g=row(p['mla_kv_norm_g']),
        conv_w=p['lru_conv_w'], conv_b=row(p['lru_conv_b']),
        b_r=row(p['lru_b_r']), b_i=row(p['lru_b_i']), lam=row(p['lru_lam']),
        w_o=p['w_o'].astype(BF16),
        ln1_g=row(p['ln1_g']), ln1_b=row(p['ln1_b']), ln2_g=row(p['ln2_g']), ln2_b=row(p['ln2_b']),
        w_up=p['ffn_w_up'].astype(BF16),
        ffn_cw=jnp.pad(p['ffn_conv_w'], ((0, 0), (0, SUBLANE - FFN_CONV), (0, 0))),
        ffn_cb=row(p['ffn_conv_b']),
        w_down=p['ffn_w_down'].astype(BF16),
    )


def _gmlp_params(p, ln):
    w_s = p['gmlp_w_s'][:, :, :ln, :ln]
    b_s = jnp.repeat(jnp.swapaxes(p['gmlp_b_s'][:, :, :ln], 1, 2), A_HEAD, axis=2)
    return w_s, b_s


def _ffn_state_in(st, nb):
    n, steps, width = st.shape
    if nb == 1:
        return jnp.pad(st[:, :, None, :], ((0, 0), (0, 0), (0, SUBLANE - 1), (0, 0))).reshape(n, -1, width)
    return jnp.swapaxes(st.reshape(n // nb, nb, steps, width), 1, 2).reshape(n // nb, -1, width)


def _ffn_state_out(st, nb):
    n, _, width = st.shape
    st = st.reshape(n, FFN_CONV - 1, SUBLANE, width)
    if nb == 1:
        return st[:, :, 0, :]
    return jnp.swapaxes(st, 1, 2).reshape(n * nb, FFN_CONV - 1, width)


PROMPT_TT_IN = 512
PROMPT_TT_OUT = 256
PROMPT_SUB_OUT = 2
PROMPT_TQ = 512
PROMPT_TK = 512
SAMPLE_NB = 8
SAMPLE_TK = 1024


def kernel(x_prompt, x_sample, cache_kv_latent, cache_k_rope, state_lru_h, state_lru_conv, state_ffn_conv,
           ln1_g, ln1_b, ln2_g, ln2_b, w_in, w_o, gmlp_w_s, gmlp_b_s, mla_q_norm_g, mla_w_uq,
           mla_kv_norm_g, mla_w_uk, mla_w_uv, lru_conv_w, lru_conv_b, lru_w_r, lru_b_r, lru_w_i, lru_b_i,
           lru_lam, ffn_w_up, ffn_conv_w, ffn_conv_b, ffn_w_down):
    p = dict(ln1_g=ln1_g, ln1_b=ln1_b, ln2_g=ln2_g, ln2_b=ln2_b, w_in=w_in, w_o=w_o, gmlp_w_s=gmlp_w_s,
             gmlp_b_s=gmlp_b_s, mla_q_norm_g=mla_q_norm_g, mla_w_uq=mla_w_uq, mla_kv_norm_g=mla_kv_norm_g,
             mla_w_uk=mla_w_uk, mla_w_uv=mla_w_uv, lru_conv_w=lru_conv_w, lru_conv_b=lru_conv_b,
             lru_w_r=lru_w_r, lru_b_r=lru_b_r, lru_w_i=lru_w_i, lru_b_i=lru_b_i, lru_lam=lru_lam,
             ffn_w_up=ffn_w_up, ffn_conv_w=ffn_conv_w, ffn_conv_b=ffn_conv_b, ffn_w_down=ffn_w_down)
    bp, s_len, _ = x_prompt.shape
    bd, t_len, _ = x_sample.shape
    past = cache_kv_latent.shape[2]
    depth = w_in.shape[0]
    ln_p, ln_d = min(s_len, GMLP_CHUNK), min(t_len, GMLP_CHUNK)

    tabs_p = _rope_tables(jnp.arange(s_len), 1, True)
    tabs_d = _rope_tables(past + jnp.arange(t_len), SAMPLE_NB, False)
    zero_rows = jnp.zeros((bp, SUBLANE, D_C), F32)
    zero_ffn = jnp.zeros((bp, FFN_STATE_ROWS, 2 * D_FF), F32)

    xp = x_prompt.reshape(bp * s_len, D_MODEL)
    xd = x_sample.reshape(bd * t_len, D_MODEL)
    cache_kr_t = jnp.swapaxes(cache_k_rope, 2, 3)
    lw = _prep_weights(p)
    lw_p =dict(lw, **dict(zip(('w_s', 'b_s'), _gmlp_params(p, ln_p))))
    lw_d = dict(lw, **dict(zip(('w_s', 'b_s'), _gmlp_params(p, ln_d))))
    outs = {i: [] for i in (2, 3, 4, 7, 8, 9, 10)}
    stacked_p = stacked_d = None
    for l in range(depth):
        a, c, q, lat_p, kr_p, hst, cst, k, v = _mixer_in(
            xp, lw_p, l, tabs_p, zero_rows, zero_rows, stacked_p,
            n_seq=bp, seq_len=s_len, nb=1, tt=PROMPT_TT_IN, ln=ln_p, prompt=True)
        stacked_p = (lat_p, kr_p)
        b = _attn_prompt(q, k, v, n_seq=bp, seq_len=s_len, tq=PROMPT_TQ, tk=PROMPT_TK)
        xp, ffn_st = _mixer_out(xp, a, b, c, lw, l, zero_ffn,
                                n_seq=bp, seq_len=s_len, nb=1, tt=PROMPT_TT_OUT, n_sub=PROMPT_SUB_OUT)
        outs[2].append(hst[:, 0, :])
        outs[3].append(cst[:, SUBLANE - (LRU_CONV - 1):, :])
        outs[4].append(_ffn_state_out(ffn_st, 1))

        h0 = jnp.broadcast_to(state_lru_h[l][:, None, :], (bd, SUBLANE, D_C))
        conv0 = jnp.pad(state_lru_conv[l], ((0, 0), (SUBLANE - (LRU_CONV - 1), 0), (0, 0)))
        ffn_st0 = _ffn_state_in(state_ffn_conv[l], SAMPLE_NB)
        a, c, q, lat_d, kr_d, hst, cst, vg = _mixer_in(
            xd, lw_d, l, tabs_d, h0, conv0, stacked_d,
            n_seq=bd, seq_len=t_len, nb=SAMPLE_NB, tt=t_len, ln=ln_d, prompt=False)
        stacked_d = (lat_d, kr_d)
        b = _attn_sample(q, cache_kv_latent, cache_kr_t, lat_d, kr_d, lw['wkt'], lw['wv_pad'],
                         layer=l, n_seq=bd, t_new=t_len, tk=SAMPLE_TK)
        xd, ffn_st = _mixer_out(xd, a, b, c, lw, l, ffn_st0,
                                n_seq=bd, seq_len=t_len, nb=SAMPLE_NB, tt=t_len, n_sub=1)
        outs[7].append(vg.reshape(bd, t_len, D_A))
        outs[8].append(hst[:, 0, :])
        outs[9].append(cst[:, SUBLANE - (LRU_CONV - 1):, :])
        outs[10].append(_ffn_state_out(ffn_st, SAMPLE_NB))

    st = {i: jnp.stack(o) for i, o in outs.items()}
    return (xp.reshape(bp, s_len, D_MODEL), xd.reshape(bd, t_len, D_MODEL),
            lat_p.reshape(depth, bp, s_len, KV_LORA), kr_p.reshape(depth, bp, s_len, ROPE_DIM),
            st[2], st[3], st[4],
            lat_d.reshape(depth, bd, t_len, KV_LORA), kr_d.reshape(depth, bd, t_len, ROPE_DIM),
            st[7], st[8], st[9], st[10])
```

```python
import functools
import math

import jax
import jax.numpy as jnp
from jax import lax
from jax.experimental import pallas as pl
from jax.experimental.pallas import tpu as pltpu

F32 = jnp.float32
BF16 = jnp.bfloat16

D_MODEL = 1024
DEPTH = 4
CHUNK = 64
GMLP_CHUNK = 128
D_A = 256
N_A_HEADS = 4
A_HEAD = 64
D_B = 512
N_B_HEADS = 8
V_DIM = 64
NOPE_DIM = 64
ROPE_DIM = 32
Q_LORA = 384
KV_LORA = 256
ROPE_THETA = 10000.0
ATTN_SCALE = (NOPE_DIM + ROPE_DIM) ** -0.5
D_C = 256
N_C_BLOCKS = 4
C_HEAD = 64
LRU_CONV = 4
LRU_C = 8.0
D_FF = 2816
FFN_CONV = 3
ALPHA = (2.0 * DEPTH) ** 0.25
LN_EPS = 1e-5
RMS_EPS = 1e-6

LANE = 128
SUBLANE = 8
HEAD_BLK = 128
VMEM_LIMIT = 56 * 1024 * 1024

C_U, C_V, C_CQ, C_CKV, C_KR, C_XC, C_GATE, C_END = 0, 256, 512, 896, 1152, 1408, 1664, 1920

FF_CHUNK = 256
ROW_BLK = 64
DOWN_GROUP = 3
NEXT_PROLOGUE_AT = 1
N_FF = D_FF // FF_CHUNK


def _dot(a, b):
    return jnp.dot(a, b, preferred_element_type=F32)


def _dot_nt(a, b):
    return lax.dot_general(a, b, (((1,), (1,)), ((), ())), preferred_element_type=F32)


def _rms_norm(x, g):
    ms = jnp.mean(x * x, axis=-1, keepdims=True)
    return x * lax.rsqrt(ms + RMS_EPS) * g


def _layer_norm(x, g, b):
    mu = jnp.mean(x, axis=-1, keepdims=True)
    xc = x - mu
    var = jnp.mean(xc * xc, axis=-1, keepdims=True)
    return xc * lax.rsqrt(var + LN_EPS) * g + b


def _mixer_in_kernel(nb, tt, ln, prompt, n_aliased, x_ref, win_ref, wqb_ref, *rest):
    if prompt:
        wkp_ref, wuv_ref, rest = rest[0], rest[1], rest[2:]
    else:
        wqs_ref, rest = rest[0], rest[1:]
    (wr_ref, wi_ref, ws_ref, bs_ref, qg_ref, kvg_ref, cw_ref, cb_ref, br_ref, bi_ref, lam_ref,
     tqc_ref, tqs_ref, tkc_ref, tks_ref, h0_ref, conv0_ref) = rest[:17]
    rest = rest[17 + n_aliased:]
    if prompt:
        (a_ref, c_ref, q_ref, lat_ref, kr_ref, hst_ref, cst_ref, k_ref, v_ref,
         xb_ref, cv_ref, hc_ref) = rest
        vg_ref = None
    else:
        (a_ref, c_ref, q_ref, lat_ref, kr_ref, hst_ref, cst_ref, vg_ref,
         xb_ref, cv_ref, hc_ref) = rest
        k_ref = v_ref = None
    rows = nb * tt
    t_idx = pl.program_id(1)

    @pl.when(t_idx == 0)
    def _():
        cv_ref[:, 0:SUBLANE, :] = conv0_ref[...]
        hc_ref[...] = h0_ref[...]

    xb_ref[...] = x_ref[...].astype(BF16)

    def proj(c0, c1):
        return _dot(xb_ref[...], win_ref[:, c0:c1])

    xc_in = proj(C_XC, C_GATE)
    z_gate = proj(C_GATE, C_END)
    z_u = proj(C_U, C_V)
    z_v = proj(C_V, C_CQ)

    xcs = []
    for s in range(nb):
        cv_ref[s, SUBLANE:SUBLANE + tt, :] = xc_in[s * tt:(s + 1) * tt]
        acc = cb_ref[...]
        for j in range(LRU_CONV):
            off = SUBLANE - (LRU_CONV - 1) + j
            acc = acc + cv_ref[s, off:off + tt, :] * cw_ref[j:j + 1, :]
        xcs.append(acc)
        tail = cv_ref[s, tt:tt + SUBLANE, :]
        cst_ref[s] = tail
        cv_ref[s, 0:SUBLANE, :] = tail
    xc = xcs[0] if nb == 1 else jnp.concatenate(xcs, axis=0)
    xcb = xc.astype(BF16)
    z_r = _dot(xcb, wr_ref[...])
    z_i = _dot(xcb, wi_ref[...])
    z_cq = proj(C_CQ, C_CKV)
    z_ckv = proj(C_CKV, C_KR)
    kr2 = proj(C_KR, C_XC)

    gpb = ln // SUBLANE
    sub = lax.broadcasted_iota(jnp.int32, (gpb, SUBLANE, D_C), 1)
    soft_lam = jax.nn.softplus(-lam_ref[...])
    chain = {}

    def lru_block(c):
        r0 = c * ln
        seq, first = divmod(r0, tt)
        r = jax.nn.sigmoid(z_r[r0:r0 + ln] + br_ref[...])
        ig = jax.nn.sigmoid(z_i[r0:r0 + ln] + bi_ref[...])
        log_a = (-LRU_C) * r * soft_lam
        a = jnp.exp(log_a)
        b_in = jnp.sqrt(-jnp.tanh(log_a) * (a * a + 1.0)) * (ig * xc[r0:r0 + ln])
        a3 = a.reshape(gpb, SUBLANE, D_C)
        b3 = b_in.reshape(gpb, SUBLANE, D_C)
        for k in (1, 2, 4):
            keep = sub >= k
            a_sh = jnp.where(keep, pltpu.roll(a3, k, 1), 1.0)
            b_sh = jnp.where(keep, pltpu.roll(b3, k, 1), 0.0)
            b3 = a3 * b_sh + b3
            a3 = a3 * a_sh
        hb = hc_ref[seq] if first == 0 else chain['h']
        h_rows = []
        for g in range(gpb):
            hr = a3[g] * hb + b3[g]
            h_rows.append(hr)
            hb = jnp.broadcast_to(hr[SUBLANE - 1:SUBLANE, :], (SUBLANE, D_C))
        chain['h'] = hb
        if first + ln == tt:
            hc_ref[seq] = hb
            hst_ref[seq] = hb
        c_ref[r0:r0 + ln, :] = (jnp.concatenate(h_rows, axis=0)
                                * jax.nn.gelu(z_gate[r0:r0 + ln])).astype(BF16)

    cqn = _rms_norm(z_cq, qg_ref[...]).astype(BF16)
    tqc = tqc_ref[...]
    tqs = tqs_ref[...]
    ckvn = _rms_norm(z_ckv, kvg_ref[...])
    lat_ref[...] = ckvn
    ckvn_b = ckvn.astype(BF16)
    kr_rot = kr2[:, :HEAD_BLK] * tkc_ref[...] + kr2[:, HEAD_BLK:] * tks_ref[...]
    kr_ref[...] = kr_rot[:, :ROPE_DIM]

    v = jax.nn.gelu(z_v)
    if vg_ref is not None:
        vg_ref[...] = v
    vb = v.astype(BF16)
    row_i = lax.broadcasted_iota(jnp.int32, (ln, ln), 0)
    col_i = lax.broadcasted_iota(jnp.int32, (ln, ln), 1)
    w_tril = [jnp.where(row_i >= col_i, ws_ref[h], 0.0).astype(BF16) for h in range(N_A_HEADS)]
    head_of_lane = lax.broadcasted_iota(jnp.int32, (ln, D_A), 1) // A_HEAD

    def gmlp_chunk(c):
        r0 = c * ln
        vc = vb[r0:r0 + ln]
        gate = bs_ref[...]
        for h in range(N_A_HEADS):
            gate = gate + _dot(w_tril[h], jnp.where(head_of_lane == h, vc, jnp.zeros_like(vc)))
        a_ref[r0:r0 + ln, :] = (jax.nn.gelu(z_u[r0:r0 + ln]) * gate).astype(BF16)

    def q_pair(j):
        c0, c1 = 2 * j * HEAD_BLK, 2 * (j + 1) * HEAD_BLK
        if prompt:
            qb = _dot_nt(wqb_ref[c0:c1, :], cqn)
            half = ROPE_DIM // 2
            for i in range(2):
                r0 = i * HEAD_BLK
                r1, r2 = qb[r0:r0 + half], qb[r0 + half:r0 + ROPE_DIM]
                rest = qb[r0 + ROPE_DIM:r0 + HEAD_BLK] * (ATTN_SCALE * LOG2E)
                blk = jnp.concatenate([r1 * tqc - r2 * tqs, r1 * tqs + r2 * tqc, rest], axis=0)
                q_ref[c0 + r0:c0 + r0 + HEAD_BLK, :] = blk.astype(BF16)
        else:
            q_ref[:, c0:c1] = (_dot(cqn, wqb_ref[:, c0:c1]) * tqc
                               + _dot(cqn, wqs_ref[:, c0:c1]) * tqs).astype(BF16)

    def k_pair(j):
        c0 = 2 * j * HEAD_BLK
        kp = _dot(ckvn_b, wkp_ref[:, c0:c0 + 2 * HEAD_BLK])
        k_ref[:, c0:c0 + HEAD_BLK] = (kp[:, :HEAD_BLK] + kr_rot).astype(BF16)
        k_ref[:, c0 + HEAD_BLK:c0 + 2 * HEAD_BLK] = (kp[:, HEAD_BLK:] + kr_rot).astype(BF16)

    def v_all():
        v_t = _dot_nt(wuv_ref[...], ckvn_b).astype(BF16)
        for h in range(N_B_HEADS):
            v_ref[h * V_AUG:h * V_AUG + V_DIM, :] = v_t[h * V_DIM:(h + 1) * V_DIM]
            v_ref[h * V_AUG + V_DIM:(h + 1) * V_AUG, :] = jnp.ones((V_ONES, rows), BF16)

    n_blocks = rows // ln
    mla = [functools.partial(q_pair, j) for j in range(N_B_HEADS // 2)]
    if prompt:
        mla += [functools.partial(k_pair, j) for j in range(N_B_HEADS // 2)] + [v_all]
    per_block = -(-len(mla) // n_blocks)
    for c in range(n_blocks):
        for piece in mla[c * per_block:(c + 1) * per_block]:
            piece()
        gmlp_chunk(c)
        lru_block(c)


def _layer_spec(w, layer, single_buffer=False):
    tail = (0,) * (w.ndim - 1)
    mode = dict(pipeline_mode=pl.Buffered(1)) if single_buffer else {}
    return pl.BlockSpec((None,) + w.shape[1:], lambda *_: (layer,) + tail, **mode)


def _mixer_in(x2d, lw, layer, tabs, h0, conv0, stacked, *, n_seq, seq_len, nb, tt, ln, prompt):
    depth = lw['w_in'].shape[0]
    nt = seq_len // tt
    assert nb == 1 or nt == 1
    rows = nb * tt
    n_rows = n_seq * seq_len
    grid = (n_seq // nb, nt)
    row_map = lambda b, t: (b * nt + t, 0)
    layer_row_map = lambda b, t: (layer, b * nt + t, 0)
    tab_map = lambda b, t: (t, 0)
    seq_map = lambda b, t: (b, 0, 0)

    mode_weights = [lw['wq_blk_t'], lw['wk_pad'], lw['w_uv_t']] if prompt else [lw['wq_blk'], lw['wq_swp']]
    weights = [lw['w_in']] + mode_weights + [
        lw['w_r'], lw['w_i'], lw['w_s'], lw['b_s'], lw['q_g'], lw['kv_g'], lw['conv_w'], lw['conv_b'],
        lw['b_r'], lw['b_i'], lw['lam']]
    col_map = lambda b, t: (0, b * nt + t)
    if prompt:
        q_tab_spec = pl.BlockSpec((ROPE_DIM // 2, rows), lambda b, t: (0, t))
        q_shape = jax.ShapeDtypeStruct((N_B_HEADS * HEAD_BLK, n_rows), BF16)
        q_spec = pl.BlockSpec((N_B_HEADS * HEAD_BLK, rows), col_map)
    else:
        q_tab_spec = pl.BlockSpec((rows, 2 * HEAD_BLK), tab_map)
        q_shape = jax.ShapeDtypeStruct((n_rows, N_B_HEADS * HEAD_BLK), BF16)
        q_spec = pl.BlockSpec((rows, N_B_HEADS * HEAD_BLK), row_map)
    in_specs = ([pl.BlockSpec((rows, D_MODEL), row_map)]
                + [_layer_spec(w, layer) for w in weights]
                + [q_tab_spec, q_tab_spec,
                   pl.BlockSpec((rows, HEAD_BLK), tab_map), pl.BlockSpec((rows, HEAD_BLK), tab_map),
                   pl.BlockSpec((nb, SUBLANE, D_C), seq_map), pl.BlockSpec((nb, SUBLANE, D_C), seq_map)])
    out_shape = [jax.ShapeDtypeStruct((n_rows, D_A), BF16),
                 jax.ShapeDtypeStruct((n_rows, D_C), BF16),
                 q_shape,
                 jax.ShapeDtypeStruct((depth, n_rows, KV_LORA), F32),
                 jax.ShapeDtypeStruct((depth, n_rows, ROPE_DIM), F32),
                 jax.ShapeDtypeStruct((n_seq, SUBLANE, D_C), F32),
                 jax.ShapeDtypeStruct((n_seq, SUBLANE, D_C), F32)]
    out_specs = [pl.BlockSpec((rows, D_A), row_map), pl.BlockSpec((rows, D_C), row_map),
                 q_spec, pl.BlockSpec((None, rows, KV_LORA), layer_row_map),
                 pl.BlockSpec((None, rows, ROPE_DIM), layer_row_map),
                 pl.BlockSpec((nb, SUBLANE, D_C), seq_map), pl.BlockSpec((nb, SUBLANE, D_C), seq_map)]
    if prompt:
        out_shape += [jax.ShapeDtypeStruct((n_rows, N_B_HEADS * HEAD_BLK), BF16),
                      jax.ShapeDtypeStruct((N_B_HEADS * V_AUG, n_rows), BF16)]
        out_specs += [pl.BlockSpec((rows, N_B_HEADS * HEAD_BLK), row_map),
                      pl.BlockSpec((N_B_HEADS * V_AUG, rows), col_map)]
    else:
        out_shape += [jax.ShapeDtypeStruct((n_rows, D_A), F32)]
        out_specs += [pl.BlockSpec((rows, D_A), row_map)]
    scratch = [pltpu.VMEM((rows, D_MODEL), BF16),
               pltpu.VMEM((nb, tt + SUBLANE, D_C), F32),
               pltpu.VMEM((nb, SUBLANE, D_C), F32)]
    stacked = tuple(stacked or ())
    n_in = len(in_specs)
    in_specs += [pl.BlockSpec(memory_space=pl.ANY)] * len(stacked)
    aliases = {n_in + i: 3 + i for i in range(len(stacked))}
    return pl.pallas_call(
        functools.partial(_mixer_in_kernel, nb, tt, ln, prompt, len(stacked)),
        grid=grid, in_specs=in_specs, out_specs=out_specs, out_shape=out_shape,
        scratch_shapes=scratch, input_output_aliases=aliases,
        compiler_params=pltpu.CompilerParams(dimension_semantics=("arbitrary", "arbitrary"),
                                             vmem_limit_bytes=VMEM_LIMIT),
        name="mixer_in_prompt" if prompt else "mixer_in_sample",
    )(x2d, *weights, *tabs, h0, conv0, *stacked)


NEG_BIG = -1e30
LOG2E = math.log2(math.e)
V_ONES = 16
V_AUG = V_DIM + V_ONES
ATTN_HEAD_GROUP = 2


def _last_kv_tile(qi, tq, tk):
    return ((qi + 1) * tq - 1) // tk


def _kv_steps(nq, tq, tk):
    pairs = [(i, j) for i in range(nq) for j in range(_last_kv_tile(i, tq, tk) + 1)]
    return (jnp.asarray([p[0] for p in pairs], jnp.int32), jnp.asarray([p[1] for p in pairs], jnp.int32))


def _attn_prompt_kernel(tq, tk, qi_tab, ki_tab, q_ref, k_ref, v_ref, o_ref, m_ref, acc_ref):
    step_idx = pl.program_id(1)
    qi = qi_tab[step_idx]
    ki = ki_tab[step_idx]
    last = _last_kv_tile(qi, tq, tk)

    @pl.when(ki == 0)
    def _():
        m_ref[...] = jnp.full(m_ref.shape, NEG_BIG, F32)
        acc_ref[...] = jnp.zeros(acc_ref.shape, F32)

    half = tq // 2

    def step(diag):
        if diag:
            local = (lax.broadcasted_iota(jnp.int32, (half, half), 0) // CHUNK
                     <= lax.broadcasted_iota(jnp.int32, (half, half), 1) // CHUNK)
            parts = ((0, half, half), (half, tq, tk))
        else:
            parts = ((0, tq, tk),)

        def scores(h):
            blk = slice(h * HEAD_BLK, (h + 1) * HEAD_BLK)
            return [_dot(k_ref[0:nk, blk], q_ref[blk, c0:c1]) for (c0, c1, nk) in parts]

        def softmax(h, s_parts):
            out = []
            for (c0, c1, nk), s in zip(parts, s_parts):
                if diag:
                    low = jnp.where(local, s[nk - half:], NEG_BIG)
                    s = low if nk == half else jnp.concatenate([s[:nk - half], low], axis=0)
                m_prev = m_ref[h, :, c0:c1]
                m_new = jnp.maximum(m_prev, jnp.max(s, axis=0, keepdims=True))
                m_ref[h, :, c0:c1] = m_new
                out.append((jnp.exp2(m_prev - m_new), jnp.exp2(s - m_new).astype(BF16)))
            return out

        group = ATTN_HEAD_GROUP
        s_next = [scores(h) for h in range(group)]
        for h0 in range(0, N_B_HEADS, group):
            s_cur = s_next
            if h0 + group < N_B_HEADS:
                s_next = [scores(h) for h in range(h0 + group, h0 + 2 * group)]
            probs = [softmax(h0 + i, s_cur[i]) for i in range(group)]
            for i, prob_parts in enumerate(probs):
                rows = slice((h0 + i) * V_AUG, (h0 + i + 1) * V_AUG)
                for (c0, c1, nk), (alpha, p) in zip(parts, prob_parts):
                    acc_ref[rows, c0:c1] = alpha * acc_ref[rows, c0:c1] + _dot(v_ref[rows, 0:nk], p)

    @pl.when(ki == qi)
    def _():
        step(True)

    @pl.when(ki != qi)
    def _():
        step(False)

    @pl.when(ki == last)
    def _():
        for j in range(N_B_HEADS // 2):
            halves = []
            for h in (2 * j, 2 * j + 1):
                denom = acc_ref[h * V_AUG + V_DIM:h * V_AUG + V_DIM + 1, :]
                halves.append(acc_ref[h * V_AUG:h * V_AUG + V_DIM, :] / denom)
            o_ref[:, j * LANE:(j + 1) * LANE] = jnp.concatenate(halves, axis=0).T.astype(BF16)


def _attn_prompt(q_t, k, v_t, *, n_seq, seq_len, tq, tk):
    assert tq == tk and (tq // 2) % CHUNK == 0
    nq, nk = seq_len // tq, seq_len // tk
    qi_tab, ki_tab = _kv_steps(nq, tq, tk)
    grid_spec = pltpu.PrefetchScalarGridSpec(
        num_scalar_prefetch=2,
        grid=(n_seq, qi_tab.shape[0]),
        in_specs=[pl.BlockSpec((N_B_HEADS * HEAD_BLK, tq), lambda b, s, qt, kt: (0, b * nq + qt[s])),
                  pl.BlockSpec((tk, N_B_HEADS * HEAD_BLK), lambda b, s, qt, kt: (b * nk + kt[s], 0)),
                  pl.BlockSpec((N_B_HEADS * V_AUG, tk), lambda b, s, qt, kt: (0, b * nk + kt[s]))],
        out_specs=pl.BlockSpec((tq, D_B), lambda b, s, qt, kt: (b * nq + qt[s], 0)),
        scratch_shapes=[pltpu.VMEM((N_B_HEADS, 1, tq), F32),
                        pltpu.VMEM((N_B_HEADS * V_AUG, tq), F32)])
    return pl.pallas_call(
        functools.partial(_attn_prompt_kernel, tq, tk),
        grid_spec=grid_spec,
        out_shape=jax.ShapeDtypeStruct((n_seq * seq_len, D_B), BF16),
        compiler_params=pltpu.CompilerParams(dimension_semantics=("arbitrary", "arbitrary"),
                                             vmem_limit_bytes=VMEM_LIMIT),
        name="attn_prompt",
    )(qi_tab, ki_tab, q_t, k, v_t)


def _attn_sample_kernel(t_new, tk, q_ref, clat_ref, ckr_ref, nlat_ref, nkr_ref, wkt_ref, wv_ref, o_ref):
    past = clat_ref.shape[0]
    ql, qr = [], []
    for h in range(N_B_HEADS):
        q_h = q_ref[:, h * HEAD_BLK:(h + 1) * HEAD_BLK]
        ql.append(_dot(q_h, wkt_ref[h]).astype(BF16))
        qr.append(q_h[:, :ROPE_DIM])
    ql = jnp.concatenate(ql, axis=0)
    qr = jnp.concatenate(qr, axis=0)

    def scores(i):
        if i == 0:
            c_b = nlat_ref[...].astype(BF16)
            return _dot_nt(ql, c_b) + _dot_nt(qr, nkr_ref[...].astype(BF16)), c_b
        r0 = (i - 1) * tk
        c_b = clat_ref[r0:r0 + tk, :].astype(BF16)
        return _dot_nt(ql, c_b) + _dot(qr, ckr_ref[:, r0:r0 + tk].astype(BF16)), c_b

    rows = N_B_HEADS * t_new
    m = jnp.full((rows, 1), NEG_BIG, F32)
    l = jnp.zeros((rows, 1), F32)
    acc = jnp.zeros((rows, KV_LORA), F32)
    n_tiles = 1 + past // tk
    nxt = scores(0)
    for i in range(n_tiles):
        s, c_b = nxt
        if i + 1 < n_tiles:
            nxt = scores(i + 1)
        m_new = jnp.maximum(m, jnp.max(s, axis=1, keepdims=True))
        alpha = jnp.exp2(m - m_new)
        p = jnp.exp2(s - m_new)
        l = alpha * l + jnp.sum(p, axis=1, keepdims=True)
        acc = alpha * acc + _dot(p.astype(BF16), c_b)
        m = m_new
    o_lat = (acc / l).astype(BF16)
    out = _dot(o_lat[0:t_new], wv_ref[0])
    for h in range(1, N_B_HEADS):
        out = out + _dot(o_lat[h * t_new:(h + 1) * t_new], wv_ref[h])
    o_ref[...] = out.astype(BF16)


def _attn_sample(q, cache_lat, cache_kr, new_lat, new_kr, wkt, wv_pad, *, layer, n_seq, t_new, tk):
    past = cache_lat.shape[2]
    assert past % tk == 0
    row_map = lambda b: (b, 0)
    return pl.pallas_call(
        functools.partial(_attn_sample_kernel, t_new, tk),
        grid=(n_seq,),
        in_specs=[pl.BlockSpec((t_new, N_B_HEADS * HEAD_BLK), row_map),
                  pl.BlockSpec((None, None, past, KV_LORA), lambda b: (layer, b, 0, 0)),
                  pl.BlockSpec((None, None, ROPE_DIM, past), lambda b: (layer, b, 0, 0)),
                  pl.BlockSpec((None, t_new, KV_LORA), lambda b: (layer, b, 0)),
                  pl.BlockSpec((None, t_new, ROPE_DIM), lambda b: (layer, b, 0)),
                  _layer_spec(wkt, layer), _layer_spec(wv_pad, layer)],
        out_specs=pl.BlockSpec((t_new, D_B), row_map),
        out_shape=jax.ShapeDtypeStruct((n_seq * t_new, D_B), BF16),
        compiler_params=pltpu.CompilerParams(dimension_semantics=("arbitrary",),
                                             vmem_limit_bytes=VMEM_LIMIT),
        name="attn_sample",
    )(q, cache_lat, cache_kr, new_lat, new_kr, wkt, wv_pad)


def _mixer_out_kernel(nb, tt, n_sub,
                      x_ref, a_ref, b_ref, c_ref, wo_ref, g1_ref, b1_ref,
                      wu_ref, cw_ref, cb_ref, wd_ref, g2_ref, b2_ref, s_ref,
                      y_ref, o_ref,
                      x1_ref, xp_ref, x1b_ref, acc_ref, bufg0_ref, bufg1_ref, bufv0_ref, bufv1_ref,
                      car_ref, act0_ref, act1_ref):
    bufg_ref = (bufg0_ref, bufg1_ref)
    bufv_ref = (bufv0_ref, bufv1_ref)
    act_ref = (act0_ref, act1_ref)
    t_idx = pl.program_id(1)
    rows = nb * tt
    n_tiles = rows // SUBLANE
    halo = (FFN_CONV - 1) * SUBLANE

    @pl.when(t_idx == 0)
    def _():
        car_ref[...] = s_ref[...]

    n_items = n_sub * N_FF

    def prologue(t):
        rs = slice(t * rows, (t + 1) * rows)
        mix = (_dot(a_ref[rs, :], wo_ref[0:D_A, :]) + _dot(b_ref[rs, :], wo_ref[D_A:D_A + D_B, :])
               + _dot(c_ref[rs, :], wo_ref[D_A + D_B:, :]))
        x1 = _layer_norm(ALPHA * x_ref[rs, :] + mix, g1_ref[...], b1_ref[...])
        x1_ref[rs, :] = x1
        for c in range(D_MODEL // LANE):
            for k in range(n_tiles):
                xp_ref[c, pl.ds(_interleave_start(k, n_tiles), SUBLANE, stride=SUBLANE), :] = (
                    x1[k * SUBLANE:(k + 1) * SUBLANE, c * LANE:(c + 1) * LANE])
            x1b_ref[t, :, c * LANE:(c + 1) * LANE] = xp_ref[c].astype(BF16)

    def epilogue(t):
        rs = slice(t * rows, (t + 1) * rows)
        ffn = jnp.concatenate(
            [jnp.concatenate([acc_ref[t, c, pl.ds(_interleave_start(k, n_tiles), SUBLANE, stride=SUBLANE), :]
                              for k in range(n_tiles)], axis=0)
             for c in range(D_MODEL // LANE)], axis=1)
        y_ref[rs, :] = _layer_norm(ALPHA * x1_ref[rs, :] + ffn, g2_ref[...], b2_ref[...])

    def cols(j, value_half):
        c0 = (D_FF if value_half else 0) + j * FF_CHUNK
        return slice(c0, c0 + FF_CHUNK)

    first_sublane = lax.broadcasted_iota(jnp.int32, (SUBLANE, FF_CHUNK), 0) == 0

    def stage(up, buf, cs):
        buf[halo:halo + rows, :] = up
        for i in range(FFN_CONV - 1):
            carry_rows = slice(i * SUBLANE, (i + 1) * SUBLANE)
            last = up[rows - halo + i * SUBLANE:rows - halo + (i + 1) * SUBLANE]
            if nb == 1:
                last = pltpu.roll(last, 1, 0)
                prev = jnp.where(first_sublane, car_ref[carry_rows, cs], last)
            else:
                prev = car_ref[carry_rows, cs]
            buf[carry_rows, :] = prev
            car_ref[carry_rows, cs] = last
            o_ref[carry_rows, cs] = last

    def conv_rows(buf, cs, r0, nr):
        acc = cb_ref[:, cs]
        for i in range(FFN_CONV):
            acc = acc + buf[r0 + i * SUBLANE:r0 + i * SUBLANE + nr, :] * cw_ref[i:i + 1, cs]
        return acc

    def up_proj(i):
        t, j = divmod(i, N_FF)
        xb = x1b_ref[t]
        stage(_dot(xb, wu_ref[:, cols(j, False)]), bufg_ref[i % 2], cols(j, False))
        stage(_dot(xb, wu_ref[:, cols(j, True)]), bufv_ref[i % 2], cols(j, True))

    groups_per_tile = -(-N_FF // DOWN_GROUP)

    def act_slot(t, grp):
        return act_ref[(t * groups_per_tile + grp) % 2]

    def elementwise(i):
        t, j = divmod(i, N_FF)
        grp, pos = divmod(j, DOWN_GROUP)
        for r0 in range(0, rows, ROW_BLK):
            nr = min(ROW_BLK, rows - r0)
            gate = conv_rows(bufg_ref[i % 2], cols(j, False), r0, nr)
            val = conv_rows(bufv_ref[i % 2], cols(j, True), r0, nr)
            act_slot(t, grp)[r0:r0 + nr, pos * FF_CHUNK:(pos + 1) * FF_CHUNK] = (
                jax.nn.gelu(gate) * val).astype(BF16)

    def down_proj(t, grp):
        n_chunks = min(DOWN_GROUP, N_FF - grp * DOWN_GROUP)
        k0, width = grp * DOWN_GROUP * FF_CHUNK, n_chunks * FF_CHUNK
        part = _dot(act_slot(t, grp)[:, 0:width], wd_ref[k0:k0 + width, :])
        for c in range(D_MODEL // LANE):
            if grp == 0:
                acc_ref[t, c] = part[:, c * LANE:(c + 1) * LANE]
            else:
                acc_ref[t, c] += part[:, c * LANE:(c + 1) * LANE]

    prologue(0)
    up_proj(0)
    pending = None
    for i in range(n_items + 1):
        if i + 1 < n_items:
            up_proj(i + 1)
        if i % N_FF == NEXT_PROLOGUE_AT and i // N_FF + 1 < n_sub:
            prologue(i // N_FF + 1)
        if pending is not None:
            down_proj(*pending)
            if pending[1] == groups_per_tile - 1:
                epilogue(pending[0])
            pending = None
        if i < n_items:
            elementwise(i)
            t, j = divmod(i, N_FF)
            if (j + 1) % DOWN_GROUP == 0 or j == N_FF - 1:
                pending = (t, j // DOWN_GROUP)


def _interleave_start(k, n_tiles):
    group, part = divmod(k, n_tiles // SUBLANE)
    return part * SUBLANE * SUBLANE + group


FFN_STATE_ROWS = (FFN_CONV - 1) * SUBLANE


def _mixer_out(x2d, a, b, c, lw, layer, state, *, n_seq, seq_len, nb, tt, n_sub):
    nt = seq_len // (tt * n_sub)
    assert nb == 1 or (nt == 1 and n_sub == 1)
    rows = nb * tt
    blk_rows = rows * n_sub
    assert nb in (1, SUBLANE) and rows % (SUBLANE * SUBLANE) == 0
    n_rows = n_seq * seq_len
    row_map = lambda bi, t: (bi * nt + t, 0)
    weights_a = [lw['w_o'], lw['ln1_g'], lw['ln1_b'], lw['w_up'], lw['ffn_cw'], lw['ffn_cb'],
                 lw['w_down'], lw['ln2_g'], lw['ln2_b']]
    st_spec = pl.BlockSpec((None, FFN_STATE_ROWS, 2 * D_FF), lambda bi, t: (bi, 0, 0))
    in_specs = ([pl.BlockSpec((blk_rows, D_MODEL), row_map), pl.BlockSpec((blk_rows, D_A), row_map),
                 pl.BlockSpec((blk_rows, D_B), row_map), pl.BlockSpec((blk_rows, D_C), row_map)]
                + [_layer_spec(w, layer, single_buffer=True) for w in weights_a]
                + [st_spec])
    st_shape = jax.ShapeDtypeStruct((n_seq // nb, FFN_STATE_ROWS, 2 * D_FF), F32)
    buf_rows = FFN_STATE_ROWS + rows
    return pl.pallas_call(
        functools.partial(_mixer_out_kernel, nb, tt, n_sub),
        grid=(n_seq // nb, nt),
        in_specs=in_specs,
        out_specs=[pl.BlockSpec((blk_rows, D_MODEL), row_map), st_spec],
        out_shape=[jax.ShapeDtypeStruct((n_rows, D_MODEL), F32), st_shape],
        scratch_shapes=[pltpu.VMEM((blk_rows, D_MODEL), F32),
                        pltpu.VMEM((D_MODEL // LANE, rows, LANE), F32),
                        pltpu.VMEM((n_sub, rows, D_MODEL), BF16),
                        pltpu.VMEM((n_sub, D_MODEL // LANE, rows, LANE), F32),
                        pltpu.VMEM((buf_rows, FF_CHUNK), F32),
                        pltpu.VMEM((buf_rows, FF_CHUNK), F32),
                        pltpu.VMEM((buf_rows, FF_CHUNK), F32),
                        pltpu.VMEM((buf_rows, FF_CHUNK), F32),
                        pltpu.VMEM((FFN_STATE_ROWS, 2 * D_FF), F32),
                        pltpu.VMEM((rows, DOWN_GROUP * FF_CHUNK), BF16),
                        pltpu.VMEM((rows, DOWN_GROUP * FF_CHUNK), BF16)],
        compiler_params=pltpu.CompilerParams(dimension_semantics=("arbitrary", "arbitrary"),
                                             vmem_limit_bytes=VMEM_LIMIT),
        name="mixer_out",
    )(x2d, a, b, c, *weights_a, state)


def _rope_tables(pos, reps, feature_major_q):
    half = ROPE_DIM // 2
    inv = 1.0 / (ROPE_THETA ** (jnp.arange(0, ROPE_DIM, 2, dtype=F32) / ROPE_DIM))
    ang = pos.astype(F32)[:, None] * inv[None, :]
    cos, sin = jnp.cos(ang), jnp.sin(ang)
    n = pos.shape[0]
    zeros = jnp.zeros((n, HEAD_BLK - ROPE_DIM), F32)
    tkc = jnp.concatenate([cos, cos, zeros], axis=1)
    tks = jnp.concatenate([sin, sin, zeros], axis=1)
    tqc = jnp.concatenate([cos, cos, jnp.ones((n, NOPE_DIM), F32),
                           jnp.zeros((n, HEAD_BLK - ROPE_DIM - NOPE_DIM), F32)], axis=1)
    tqc = jnp.concatenate([tqc, tqc], axis=1) * (ATTN_SCALE * LOG2E)
    tqs = jnp.concatenate([tks, tks], axis=1) * (ATTN_SCALE * LOG2E)
    if feature_major_q:
        return ((cos * (ATTN_SCALE * LOG2E)).T, (sin * (ATTN_SCALE * LOG2E)).T, tkc, tks)
    return tuple(jnp.tile(t, (reps, 1)) for t in (tqc, tqs, tkc, tks))


def _block_diag(w):
    depth, n, d, e = w.shape
    return jnp.einsum('lnde,nm->lndme', w, jnp.eye(n, dtype=w.dtype)).reshape(depth, n * d, n * e)


def _prep_weights(p):
    half = ROPE_DIM // 2
    depth = p['w_in'].shape[0]
    u, v, cq, ckv, kr, xc, gate = jnp.split(p['w_in'], [256, 512, 896, 1152, 1184, 1440], axis=2)
    kr1, kr2 = kr[..., :half], kr[..., half:]
    z = jnp.zeros((depth, D_MODEL, HEAD_BLK - ROPE_DIM), F32)
    w_in = jnp.concatenate([u, v, cq, ckv, kr1, kr2, z, -kr2, kr1, z, xc, gate], axis=2)

    wq = p['mla_w_uq'].reshape(depth, Q_LORA, N_B_HEADS, NOPE_DIM + ROPE_DIM)
    nope, r1, r2 = wq[..., :NOPE_DIM], wq[..., NOPE_DIM:NOPE_DIM + half], wq[..., NOPE_DIM + half:]
    zq = lambda k: jnp.zeros((depth, Q_LORA, N_B_HEADS, k), F32)
    wq_blk = jnp.concatenate([r1, r2, nope, zq(HEAD_BLK - ROPE_DIM - NOPE_DIM)], axis=-1)
    wq_blk = wq_blk.reshape(depth, Q_LORA, -1).astype(BF16)
    wq_swp = jnp.concatenate([-r2, r1, zq(HEAD_BLK - ROPE_DIM)], axis=-1).reshape(depth, Q_LORA, -1).astype(BF16)

    w_uk, w_uv = p['mla_w_uk'], p['mla_w_uv']
    zk = jnp.zeros((depth, KV_LORA, N_B_HEADS, ROPE_DIM), F32)
    wk_pad = jnp.concatenate([zk, w_uk, zk], axis=-1).reshape(depth, KV_LORA, -1)
    zt = jnp.zeros((depth, N_B_HEADS, ROPE_DIM, KV_LORA), F32)
    wkt = jnp.concatenate([zt, jnp.transpose(w_uk, (0, 2, 3, 1)), zt], axis=2)
    wv_pad = jnp.einsum('lchd,hg->lhcgd', w_uv, jnp.eye(N_B_HEADS, dtype=F32)).reshape(
        depth, N_B_HEADS, KV_LORA, D_B)
    w_uv2 = w_uv.reshape(depth, KV_LORA, -1).astype(BF16)

    row = lambda a: a.reshape(depth, 1, -1)
    return dict(
        w_in=w_in.astype(BF16),
        wq_blk=wq_blk, wq_swp=wq_swp,
        wq_blk_t=jnp.swapaxes(wq_blk, 1, 2),
        wk_pad=wk_pad.astype(BF16), w_uv_t=jnp.swapaxes(w_uv2, 1, 2),
        wkt=wkt.astype(BF16), wv_pad=wv_pad.astype(BF16),
        w_r=_block_diag(p['lru_w_r']).astype(BF16), w_i=_block_diag(p['lru_w_i']).astype(BF16),
        q_g=row(p['mla_q_norm_g']), kv_g=row(p['mla_kv_norm_g']),
        conv_w=p['lru_conv_w'], conv_b=row(p['lru_conv_b']),
        b_r=row(p['lru_b_r']), b_i=row(p['lru_b_i']), lam=row(p['lru_lam']),
        w_o=p['w_o'].astype(BF16),
        ln1_g=row(p['ln1_g']), ln1_b=row(p['ln1_b']), ln2_g=row(p['ln2_g']), ln2_b=row(p['ln2_b']),
        w_up=p['ffn_w_up'].astype(BF16),
        ffn_cw=jnp.pad(p['ffn_conv_w'], ((0, 0), (0, SUBLANE - FFN_CONV), (0, 0))),
        ffn_cb=row(p['ffn_conv_b']),
        w_down=p['ffn_w_down'].astype(BF16),
    )


def _gmlp_params(p, ln):
    w_s = p['gmlp_w_s'][:, :, :ln, :ln]
    b_s = jnp.repeat(jnp.swapaxes(p['gmlp_b_s'][:, :, :ln], 1, 2), A_HEAD, axis=2)
    return w_s, b_s


def _ffn_state_in(st, nb):
    n, steps, width = st.shape
    if nb == 1:
        return jnp.pad(st[:, :, None, :], ((0, 0), (0, 0), (0, SUBLANE - 1), (0, 0))).reshape(n, -1, width)
    return jnp.swapaxes(st.reshape(n // nb, nb, steps, width), 1, 2).reshape(n // nb, -1, width)


def _ffn_state_out(st, nb):
    n, _, width = st.shape
    st = st.reshape(n, FFN_CONV - 1, SUBLANE, width)
    if nb == 1:
        return st[:, :, 0, :]
    return jnp.swapaxes(st, 1, 2).reshape(n * nb, FFN_CONV - 1, width)


PROMPT_TT_IN = 512
PROMPT_TT_OUT = 256
PROMPT_SUB_OUT = 2
PROMPT_TQ = 512
PROMPT_TK = 512
SAMPLE_NB = 8
SAMPLE_TK = 1024


def kernel(x_prompt, x_sample, cache_kv_latent, cache_k_rope, state_lru_h, state_lru_conv, state_ffn_conv,
           ln1_g, ln1_b, ln2_g, ln2_b, w_in, w_o, gmlp_w_s, gmlp_b_s, mla_q_norm_g, mla_w_uq,
           mla_kv_norm_g, mla_w_uk, mla_w_uv, lru_conv_w, lru_conv_b, lru_w_r, lru_b_r, lru_w_i, lru_b_i,
           lru_lam, ffn_w_up, ffn_conv_w, ffn_conv_b, ffn_w_down):
    p = dict(ln1_g=ln1_g, ln1_b=ln1_b, ln2_g=ln2_g, ln2_b=ln2_b, w_in=w_in, w_o=w_o, gmlp_w_s=gmlp_w_s,
             gmlp_b_s=gmlp_b_s, mla_q_norm_g=mla_q_norm_g, mla_w_uq=mla_w_uq, mla_kv_norm_g=mla_kv_norm_g,
             mla_w_uk=mla_w_uk, mla_w_uv=mla_w_uv, lru_conv_w=lru_conv_w, lru_conv_b=lru_conv_b,
             lru_w_r=lru_w_r, lru_b_r=lru_b_r, lru_w_i=lru_w_i, lru_b_i=lru_b_i, lru_lam=lru_lam,
             ffn_w_up=ffn_w_up, ffn_conv_w=ffn_conv_w, ffn_conv_b=ffn_conv_b, ffn_w_down=ffn_w_down)
    bp, s_len, _ = x_prompt.shape
    bd, t_len, _ = x_sample.shape
    past = cache_kv_latent.shape[2]
    depth = w_in.shape[0]
    ln_p, ln_d = min(s_len, GMLP_CHUNK), min(t_len, GMLP_CHUNK)

    tabs_p = _rope_tables(jnp.arange(s_len), 1, True)
    tabs_d = _rope_tables(past + jnp.arange(t_len), SAMPLE_NB, False)
    zero_rows = jnp.zeros((bp, SUBLANE, D_C), F32)
    zero_ffn = jnp.zeros((bp, FFN_STATE_ROWS, 2 * D_FF), F32)

    xp = x_prompt.reshape(bp * s_len, D_MODEL)
    xd = x_sample.reshape(bd * t_len, D_MODEL)
    cache_kr_t = jnp.swapaxes(cache_k_rope, 2, 3)
    lw = _prep_weights(p)
    lw_p = dict(lw, **dict(zip(('w_s', 'b_s'), _gmlp_params(p, ln_p))))
    lw_d = dict(lw, **dict(zip(('w_s', 'b_s'), _gmlp_params(p, ln_d))))
    outs = {i: [] for i in (2, 3, 4, 7, 8, 9, 10)}
    stacked_p = (jnp.zeros((depth, bp * s_len, KV_LORA), F32), jnp.zeros((depth, bp * s_len, ROPE_DIM), F32))
    stacked_d = (jnp.zeros((depth, bd * t_len, KV_LORA), F32), jnp.zeros((depth, bd * t_len, ROPE_DIM), F32))
    for l in range(depth):
        a, c, q, lat_p, kr_p, hst, cst, k, v = _mixer_in(
            xp, lw_p, l, tabs_p, zero_rows, zero_rows, stacked_p,
            n_seq=bp, seq_len=s_len, nb=1, tt=PROMPT_TT_IN, ln=ln_p, prompt=True)
        stacked_p = (lat_p, kr_p)
        b = _attn_prompt(q, k, v, n_seq=bp, seq_len=s_len, tq=PROMPT_TQ, tk=PROMPT_TK)
        xp, ffn_st = _mixer_out(xp, a, b, c, lw, l, zero_ffn,
                                n_seq=bp, seq_len=s_len, nb=1, tt=PROMPT_TT_OUT, n_sub=PROMPT_SUB_OUT)
        outs[2].append(hst[:, 0, :])
        outs[3].append(cst[:, SUBLANE - (LRU_CONV - 1):, :])
        outs[4].append(_ffn_state_out(ffn_st, 1))

        h0 = jnp.broadcast_to(state_lru_h[l][:, None, :], (bd, SUBLANE, D_C))
        conv0 = jnp.pad(state_lru_conv[l], ((0, 0), (SUBLANE - (LRU_CONV - 1), 0), (0, 0)))
        ffn_st0 = _ffn_state_in(state_ffn_conv[l], SAMPLE_NB)
        a, c, q, lat_d, kr_d, hst, cst, vg = _mixer_in(
            xd, lw_d, l, tabs_d, h0, conv0, stacked_d,
            n_seq=bd, seq_len=t_len, nb=SAMPLE_NB, tt=t_len, ln=ln_d, prompt=False)
        stacked_d = (lat_d, kr_d)
        b = _attn_sample(q, cache_kv_latent, cache_kr_t, lat_d, kr_d, lw['wkt'], lw['wv_pad'],
                         layer=l, n_seq=bd, t_new=t_len, tk=SAMPLE_TK)
        xd, ffn_st = _mixer_out(xd, a, b, c, lw, l, ffn_st0,
                                n_seq=bd, seq_len=t_len, nb=SAMPLE_NB, tt=t_len, n_sub=1)
        outs[7].append(vg.reshape(bd, t_len, D_A))
        outs[8].append(hst[:, 0, :])
        outs[9].append(cst[:, SUBLANE - (LRU_CONV - 1):, :])
        outs[10].append(_ffn_state_out(ffn_st, SAMPLE_NB))

    st = {i: jnp.stack(o) for i, o in outs.items()}
    return (xp.reshape(bp, s_len, D_MODEL), xd.reshape(bd, t_len, D_MODEL),
            lat_p.reshape(depth, bp, s_len, KV_LORA), kr_p.reshape(depth, bp, s_len, ROPE_DIM),
            st[2], st[3], st[4],
            lat_d.reshape(depth, bd, t_len, KV_LORA), kr_d.reshape(depth, bd, t_len, ROPE_DIM),
            st[7], st[8], st[9], st[10])
```

```python
import functools
import math

import jax
import jax.numpy as jnp
from jax import lax
from jax.experimental import pallas as pl
from jax.experimental.pallas import tpu as pltpu

F32 = jnp.float32
BF16 = jnp.bfloat16

D_MODEL = 1024
DEPTH = 4
CHUNK = 64
GMLP_CHUNK = 128
D_A = 256
N_A_HEADS = 4
A_HEAD = 64
D_B = 512
N_B_HEADS = 8
V_DIM = 64
NOPE_DIM = 64
ROPE_DIM = 32
Q_LORA = 384
KV_LORA = 256
ROPE_THETA = 10000.0
ATTN_SCALE = (NOPE_DIM + ROPE_DIM) ** -0.5
D_C = 256
N_C_BLOCKS = 4
C_HEAD = 64
LRU_CONV = 4
LRU_C = 8.0
D_FF = 2816
FFN_CONV = 3
ALPHA = (2.0 * DEPTH) ** 0.25
LN_EPS = 1e-5
RMS_EPS = 1e-6

LANE = 128
SUBLANE = 8
HEAD_BLK = 128
VMEM_LIMIT = 56 * 1024 * 1024

C_U, C_V, C_CQ, C_CKV, C_KR, C_XC, C_GATE, C_END = 0, 256, 512, 896, 1152, 1408, 1664, 1920

FF_CHUNK = 256
ROW_BLK = 64
DOWN_GROUP = 3
NEXT_PROLOGUE_AT = 1
N_FF = D_FF // FF_CHUNK


def _dot(a, b):
    return jnp.dot(a, b, preferred_element_type=F32)


def _dot_nt(a, b):
    return lax.dot_general(a, b, (((1,), (1,)), ((), ())), preferred_element_type=F32)


def _rms_norm(x, g):
    ms = jnp.mean(x * x, axis=-1, keepdims=True)
    return x * lax.rsqrt(ms + RMS_EPS) * g


def _layer_norm(x, g, b):
    mu = jnp.mean(x, axis=-1, keepdims=True)
    xc = x - mu
    var = jnp.mean(xc * xc, axis=-1, keepdims=True)
    return xc * lax.rsqrt(var + LN_EPS) * g + b


def _mixer_in_kernel(nb, tt, ln, prompt, n_aliased, x_ref, win_ref, wqb_ref, *rest):
    if prompt:
        wkp_ref, wuv_ref, rest = rest[0], rest[1], rest[2:]
    else:
        wqs_ref, rest = rest[0], rest[1:]
    (wr_ref, wi_ref, ws_ref, bs_ref, qg_ref, kvg_ref, cw_ref, cb_ref, br_ref, bi_ref, lam_ref,
     tqc_ref, tqs_ref, tkc_ref, tks_ref, h0_ref, conv0_ref) = rest[:17]
    rest = rest[17 + n_aliased:]
    if prompt:
        (a_ref, c_ref, q_ref, lat_ref, kr_ref, hst_ref, cst_ref, k_ref, v_ref,
         xb_ref, cv_ref, hc_ref) = rest
        vg_ref = None
    else:
        (a_ref, c_ref, q_ref, lat_ref, kr_ref, hst_ref, cst_ref, vg_ref,
         xb_ref, cv_ref, hc_ref) = rest
        k_ref = v_ref = None
    rows = nb * tt
    t_idx = pl.program_id(1)

    @pl.when(t_idx == 0)
    def _():
        cv_ref[:, 0:SUBLANE, :] = conv0_ref[...]
        hc_ref[...] = h0_ref[...]

    xb_ref[...] = x_ref[...].astype(BF16)

    def proj(c0, c1):
        return _dot(xb_ref[...], win_ref[:, c0:c1])

    xc_in = proj(C_XC, C_GATE)
    z_gate = proj(C_GATE, C_END)
    z_u = proj(C_U, C_V)
    z_v = proj(C_V, C_CQ)

    xcs = []
    for s in range(nb):
        cv_ref[s, SUBLANE:SUBLANE + tt, :] = xc_in[s * tt:(s + 1) * tt]
        acc = cb_ref[...]
        for j in range(LRU_CONV):
            off = SUBLANE - (LRU_CONV - 1) + j
            acc = acc + cv_ref[s, off:off + tt, :] * cw_ref[j:j + 1, :]
        xcs.append(acc)
        tail = cv_ref[s, tt:tt + SUBLANE, :]
        cst_ref[s] = tail
        cv_ref[s, 0:SUBLANE, :] = tail
    xc = xcs[0] if nb == 1 else jnp.concatenate(xcs, axis=0)
    xcb = xc.astype(BF16)
    z_r = _dot(xcb, wr_ref[...])
    z_i = _dot(xcb, wi_ref[...])
    z_cq = proj(C_CQ, C_CKV)
    z_ckv = proj(C_CKV, C_KR)
    kr2 = proj(C_KR, C_XC)

    gpb = ln // SUBLANE
    sub = lax.broadcasted_iota(jnp.int32, (gpb, SUBLANE, D_C), 1)
    soft_lam = jax.nn.softplus(-lam_ref[...])
    chain = {}

    def lru_block(c):
        r0 = c * ln
        seq, first = divmod(r0, tt)
        r = jax.nn.sigmoid(z_r[r0:r0 + ln] + br_ref[...])
        ig = jax.nn.sigmoid(z_i[r0:r0 + ln] + bi_ref[...])
        log_a = (-LRU_C) * r * soft_lam
        a = jnp.exp(log_a)
        b_in = jnp.sqrt(-jnp.tanh(log_a) * (a * a + 1.0)) * (ig * xc[r0:r0 + ln])
        a3 = a.reshape(gpb, SUBLANE, D_C)
        b3 = b_in.reshape(gpb, SUBLANE, D_C)
        for k in (1, 2, 4):
            keep = sub >= k
            a_sh = jnp.where(keep, pltpu.roll(a3, k, 1), 1.0)
            b_sh = jnp.where(keep, pltpu.roll(b3, k, 1), 0.0)
            b3 = a3 * b_sh + b3
            a3 = a3 * a_sh
        hb = hc_ref[seq] if first == 0 else chain['h']
        h_rows = []
        for g in range(gpb):
            hr = a3[g] * hb + b3[g]
            h_rows.append(hr)
            hb = jnp.broadcast_to(hr[SUBLANE - 1:SUBLANE, :], (SUBLANE, D_C))
        chain['h'] = hb
        if first + ln == tt:
            hc_ref[seq] = hb
            hst_ref[seq] = hb
        c_ref[r0:r0 + ln, :] = (jnp.concatenate(h_rows, axis=0)
                                * jax.nn.gelu(z_gate[r0:r0 + ln])).astype(BF16)

    cqn = _rms_norm(z_cq, qg_ref[...]).astype(BF16)
    tqc = tqc_ref[...]
    tqs = tqs_ref[...]
    ckvn = _rms_norm(z_ckv, kvg_ref[...])
    def put_layer(ref, val):
        if n_aliased:
            ref[...] = val
        else:
            ref[0] = val
            ref[1:] = jnp.zeros((ref.shape[0] - 1,) + val.shape, val.dtype)

    put_layer(lat_ref, ckvn)
    ckvn_b = ckvn.astype(BF16)
    kr_rot = kr2[:, :HEAD_BLK] * tkc_ref[...] + kr2[:, HEAD_BLK:] * tks_ref[...]
    put_layer(kr_ref, kr_rot[:, :ROPE_DIM])

    v = jax.nn.gelu(z_v)
    if vg_ref is not None:
        vg_ref[...] = v
    vb = v.astype(BF16)
    row_i = lax.broadcasted_iota(jnp.int32, (ln, ln), 0)
    col_i = lax.broadcasted_iota(jnp.int32, (ln, ln), 1)
    w_tril = [jnp.where(row_i >= col_i, ws_ref[h], 0.0).astype(BF16) for h in range(N_A_HEADS)]
    head_of_lane = lax.broadcasted_iota(jnp.int32, (ln, D_A), 1) // A_HEAD

    def gmlp_chunk(c):
        r0 = c * ln
        vc = vb[r0:r0 + ln]
        gate = bs_ref[...]
        for h in range(N_A_HEADS):
            gate = gate + _dot(w_tril[h], jnp.where(head_of_lane == h, vc, jnp.zeros_like(vc)))
        a_ref[r0:r0 + ln, :] = (jax.nn.gelu(z_u[r0:r0 + ln]) * gate).astype(BF16)

    def q_pair(j):
        c0, c1 = 2 * j * HEAD_BLK, 2 * (j + 1) * HEAD_BLK
        if prompt:
            qb = _dot_nt(wqb_ref[c0:c1, :], cqn)
            half = ROPE_DIM // 2
            for i in range(2):
                r0 = i * HEAD_BLK
                r1, r2 = qb[r0:r0 + half], qb[r0 + half:r0 + ROPE_DIM]
                rest = qb[r0 + ROPE_DIM:r0 + HEAD_BLK] * (ATTN_SCALE * LOG2E)
                blk = jnp.concatenate([r1 * tqc - r2 * tqs, r1 * tqs + r2 * tqc, rest], axis=0)
                q_ref[c0 + r0:c0 + r0 + HEAD_BLK, :] = blk.astype(BF16)
        else:
            q_ref[:, c0:c1] = (_dot(cqn, wqb_ref[:, c0:c1]) * tqc
                               + _dot(cqn, wqs_ref[:, c0:c1]) * tqs).astype(BF16)

    def k_pair(j):
        c0 = 2 * j * HEAD_BLK
        kp = _dot(ckvn_b, wkp_ref[:, c0:c0 + 2 * HEAD_BLK])
        k_ref[:, c0:c0 + HEAD_BLK] = (kp[:, :HEAD_BLK] + kr_rot).astype(BF16)
        k_ref[:, c0 + HEAD_BLK:c0 + 2 * HEAD_BLK] = (kp[:, HEAD_BLK:] + kr_rot).astype(BF16)

    def v_all():
        v_t = _dot_nt(wuv_ref[...], ckvn_b).astype(BF16)
        for h in range(N_B_HEADS):
            v_ref[h * V_AUG:h * V_AUG + V_DIM, :] = v_t[h * V_DIM:(h + 1) * V_DIM]
            v_ref[h * V_AUG + V_DIM:(h + 1) * V_AUG, :] = jnp.ones((V_ONES, rows), BF16)

    n_blocks = rows // ln
    mla = [functools.partial(q_pair, j) for j in range(N_B_HEADS // 2)]
    if prompt:
        mla += [functools.partial(k_pair, j) for j in range(N_B_HEADS // 2)] + [v_all]
    per_block = -(-len(mla) // n_blocks)
    for c in range(n_blocks):
        for piece in mla[c * per_block:(c + 1) * per_block]:
            piece()
        gmlp_chunk(c)
        lru_block(c)


def _layer_spec(w, layer, single_buffer=False):
    tail = (0,) * (w.ndim - 1)
    mode = dict(pipeline_mode=pl.Buffered(1)) if single_buffer else {}
    return pl.BlockSpec((None,) + w.shape[1:], lambda *_: (layer,) + tail, **mode)


def _mixer_in(x2d, lw, layer, tabs, h0, conv0, stacked, *, n_seq, seq_len, nb, tt, ln, prompt):
    depth = lw['w_in'].shape[0]
    nt = seq_len // tt
    assert nb == 1 or nt == 1
    rows = nb * tt
    n_rows = n_seq * seq_len
    grid = (n_seq // nb, nt)
    row_map = lambda b, t: (b * nt + t, 0)
    stacked = tuple(stacked or ())

    def layer_out_spec(width):
        if stacked:
            return pl.BlockSpec((None, rows, width), lambda b, t: (layer, b * nt + t, 0))
        assert layer == 0
        return pl.BlockSpec((depth, rows, width), lambda b, t: (0, b * nt + t, 0))

    tab_map = lambda b, t: (t, 0)
    seq_map = lambda b, t: (b, 0, 0)

    mode_weights = [lw['wq_blk_t'], lw['wk_pad'], lw['w_uv_t']] if prompt else [lw['wq_blk'], lw['wq_swp']]
    weights = [lw['w_in']] + mode_weights + [
        lw['w_r'], lw['w_i'], lw['w_s'], lw['b_s'], lw['q_g'], lw['kv_g'], lw['conv_w'], lw['conv_b'],
        lw['b_r'], lw['b_i'], lw['lam']]
    col_map = lambda b, t: (0, b * nt + t)
    if prompt:
        q_tab_spec = pl.BlockSpec((ROPE_DIM // 2, rows), lambda b, t: (0, t))
        q_shape = jax.ShapeDtypeStruct((N_B_HEADS * HEAD_BLK, n_rows), BF16)
        q_spec = pl.BlockSpec((N_B_HEADS * HEAD_BLK, rows), col_map)
    else:
        q_tab_spec = pl.BlockSpec((rows, 2 * HEAD_BLK), tab_map)
        q_shape = jax.ShapeDtypeStruct((n_rows, N_B_HEADS * HEAD_BLK), BF16)
        q_spec = pl.BlockSpec((rows, N_B_HEADS * HEAD_BLK), row_map)
    in_specs = ([pl.BlockSpec((rows, D_MODEL), row_map)]
                + [_layer_spec(w, layer) for w in weights]
                + [q_tab_spec, q_tab_spec,
                   pl.BlockSpec((rows, HEAD_BLK), tab_map), pl.BlockSpec((rows, HEAD_BLK), tab_map),
                   pl.BlockSpec((nb, SUBLANE, D_C), seq_map), pl.BlockSpec((nb, SUBLANE, D_C), seq_map)])
    out_shape = [jax.ShapeDtypeStruct((n_rows, D_A), BF16),
                 jax.ShapeDtypeStruct((n_rows, D_C), BF16),
                 q_shape,
                 jax.ShapeDtypeStruct((depth, n_rows, KV_LORA), F32),
                 jax.ShapeDtypeStruct((depth, n_rows, ROPE_DIM), F32),
                 jax.ShapeDtypeStruct((n_seq, SUBLANE, D_C), F32),
                 jax.ShapeDtypeStruct((n_seq, SUBLANE, D_C), F32)]
    out_specs = [pl.BlockSpec((rows, D_A), row_map), pl.BlockSpec((rows, D_C), row_map),
                 q_spec, layer_out_spec(KV_LORA), layer_out_spec(ROPE_DIM),
                 pl.BlockSpec((nb, SUBLANE, D_C), seq_map), pl.BlockSpec((nb, SUBLANE, D_C), seq_map)]
    if prompt:
        out_shape += [jax.ShapeDtypeStruct((n_rows, N_B_HEADS * HEAD_BLK), BF16),
                      jax.ShapeDtypeStruct((N_B_HEADS * V_AUG, n_rows), BF16)]
        out_specs += [pl.BlockSpec((rows, N_B_HEADS * HEAD_BLK), row_map),
                      pl.BlockSpec((N_B_HEADS * V_AUG, rows), col_map)]
    else:
        out_shape += [jax.ShapeDtypeStruct((n_rows, D_A), F32)]
        out_specs += [pl.BlockSpec((rows, D_A), row_map)]
    scratch = [pltpu.VMEM((rows, D_MODEL), BF16),
               pltpu.VMEM((nb, tt + SUBLANE, D_C), F32),
               pltpu.VMEM((nb, SUBLANE, D_C), F32)]
    n_in = len(in_specs)
    in_specs += [pl.BlockSpec(memory_space=pl.ANY)] * len(stacked)
    aliases = {n_in + i: 3 + i for i in range(len(stacked))}
    return pl.pallas_call(
        functools.partial(_mixer_in_kernel, nb, tt, ln, prompt, len(stacked)),
        grid=grid, in_specs=in_specs, out_specs=out_specs, out_shape=out_shape,
        scratch_shapes=scratch, input_output_aliases=aliases,
        compiler_params=pltpu.CompilerParams(dimension_semantics=("arbitrary", "arbitrary"),
                                             vmem_limit_bytes=VMEM_LIMIT),
        name="mixer_in_prompt" if prompt else "mixer_in_sample",
    )(x2d, *weights, *tabs, h0, conv0, *stacked)


NEG_BIG = -1e30
LOG2E = math.log2(math.e)
V_ONES = 16
V_AUG = V_DIM + V_ONES
ATTN_HEAD_GROUP = 2


def _last_kv_tile(qi, tq, tk):
    return ((qi + 1) * tq - 1) // tk


def _kv_steps(nq, tq, tk):
    pairs = [(i, j) for i in range(nq) for j in range(_last_kv_tile(i, tq, tk) + 1)]
    return (jnp.asarray([p[0] for p in pairs], jnp.int32), jnp.asarray([p[1] for p in pairs], jnp.int32))


def _attn_prompt_kernel(tq, tk, qi_tab, ki_tab, q_ref, k_ref, v_ref, o_ref, m_ref, acc_ref):
    step_idx = pl.program_id(1)
    qi = qi_tab[step_idx]
    ki = ki_tab[step_idx]
    last = _last_kv_tile(qi, tq, tk)

    @pl.when(ki == 0)
    def _():
        m_ref[...] = jnp.full(m_ref.shape, NEG_BIG, F32)
        acc_ref[...] = jnp.zeros(acc_ref.shape, F32)

    half = tq // 2

    def step(diag):
        if diag:
            local = (lax.broadcasted_iota(jnp.int32, (half, half), 0) // CHUNK
                     <= lax.broadcasted_iota(jnp.int32, (half, half), 1) // CHUNK)
            parts = ((0, half, half), (half, tq, tk))
        else:
            parts = ((0, tq, tk),)

        def scores(h):
            blk = slice(h * HEAD_BLK, (h + 1) * HEAD_BLK)
            return [_dot(k_ref[0:nk, blk], q_ref[blk, c0:c1]) for (c0, c1, nk) in parts]

        def softmax(h, s_parts):
            out = []
            for (c0, c1, nk), s in zip(parts, s_parts):
                if diag:
                    low = jnp.where(local, s[nk - half:], NEG_BIG)
                    s = low if nk == half else jnp.concatenate([s[:nk - half], low], axis=0)
                m_prev = m_ref[h, :, c0:c1]
                m_new = jnp.maximum(m_prev, jnp.max(s, axis=0, keepdims=True))
                m_ref[h, :, c0:c1] = m_new
                out.append((jnp.exp2(m_prev - m_new), jnp.exp2(s - m_new).astype(BF16)))
            return out

        group = ATTN_HEAD_GROUP
        s_next = [scores(h) for h in range(group)]
        for h0 in range(0, N_B_HEADS, group):
            s_cur = s_next
            if h0 + group < N_B_HEADS:
                s_next = [scores(h) for h in range(h0 + group, h0 + 2 * group)]
            probs = [softmax(h0 + i, s_cur[i]) for i in range(group)]
            for i, prob_parts in enumerate(probs):
                rows = slice((h0 + i) * V_AUG, (h0 + i + 1) * V_AUG)
                for (c0, c1, nk), (alpha, p) in zip(parts, prob_parts):
                    acc_ref[rows, c0:c1] = alpha * acc_ref[rows, c0:c1] + _dot(v_ref[rows, 0:nk], p)

    @pl.when(ki == qi)
    def _():
        step(True)

    @pl.when(ki != qi)
    def _():
        step(False)

    @pl.when(ki == last)
    def _():
        for j in range(N_B_HEADS // 2):
            halves = []
            for h in (2 * j, 2 * j + 1):
                denom = acc_ref[h * V_AUG + V_DIM:h * V_AUG + V_DIM + 1, :]
                halves.append(acc_ref[h * V_AUG:h * V_AUG + V_DIM, :] / denom)
            o_ref[:, j * LANE:(j + 1) * LANE] = jnp.concatenate(halves, axis=0).T.astype(BF16)


def _attn_prompt(q_t, k, v_t, *, n_seq, seq_len, tq, tk):
    assert tq == tk and (tq // 2) % CHUNK == 0
    nq, nk = seq_len // tq, seq_len // tk
    qi_tab, ki_tab = _kv_steps(nq, tq, tk)
    grid_spec = pltpu.PrefetchScalarGridSpec(
        num_scalar_prefetch=2,
        grid=(n_seq, qi_tab.shape[0]),
        in_specs=[pl.BlockSpec((N_B_HEADS * HEAD_BLK, tq), lambda b, s, qt, kt: (0, b * nq + qt[s])),
                  pl.BlockSpec((tk, N_B_HEADS * HEAD_BLK), lambda b, s, qt, kt: (b * nk + kt[s], 0)),
                  pl.BlockSpec((N_B_HEADS * V_AUG, tk), lambda b, s, qt, kt: (0, b * nk + kt[s]))],
        out_specs=pl.BlockSpec((tq, D_B), lambda b, s, qt, kt: (b * nq + qt[s], 0)),
        scratch_shapes=[pltpu.VMEM((N_B_HEADS, 1, tq), F32),
                        pltpu.VMEM((N_B_HEADS * V_AUG, tq), F32)])
    return pl.pallas_call(
        functools.partial(_attn_prompt_kernel, tq, tk),
        grid_spec=grid_spec,
        out_shape=jax.ShapeDtypeStruct((n_seq * seq_len, D_B), BF16),
        compiler_params=pltpu.CompilerParams(dimension_semantics=("arbitrary", "arbitrary"),
                                             vmem_limit_bytes=VMEM_LIMIT),
        name="attn_prompt",
    )(qi_tab, ki_tab, q_t, k, v_t)


def _attn_sample_kernel(t_new, tk, q_ref, clat_ref, ckr_ref, nlat_ref, nkr_ref, wkt_ref, wv_ref, o_ref):
    past = clat_ref.shape[0]
    ql, qr = [], []
    for h in range(N_B_HEADS):
        q_h = q_ref[:, h * HEAD_BLK:(h + 1) * HEAD_BLK]
        ql.append(_dot(q_h, wkt_ref[h]).astype(BF16))
        qr.append(q_h[:, :ROPE_DIM])
    ql = jnp.concatenate(ql, axis=0)
    qr = jnp.concatenate(qr, axis=0)

    def scores(i):
        if i == 0:
            c_b = nlat_ref[...].astype(BF16)
            return _dot_nt(ql, c_b) + _dot_nt(qr, nkr_ref[...].astype(BF16)), c_b
        r0 = (i - 1) * tk
        c_b = clat_ref[r0:r0 + tk, :].astype(BF16)
        return _dot_nt(ql, c_b) + _dot(qr, ckr_ref[:, r0:r0 + tk].astype(BF16)), c_b

    rows = N_B_HEADS * t_new
    m = jnp.full((rows, 1), NEG_BIG, F32)
    l = jnp.zeros((rows, 1), F32)
    acc = jnp.zeros((rows, KV_LORA), F32)
    n_tiles = 1 + past // tk
    nxt = scores(0)
    for i in range(n_tiles):
        s, c_b = nxt
        if i + 1 < n_tiles:
            nxt = scores(i + 1)
        m_new = jnp.maximum(m, jnp.max(s, axis=1, keepdims=True))
        alpha = jnp.exp2(m - m_new)
        p = jnp.exp2(s - m_new)
        l = alpha * l + jnp.sum(p, axis=1, keepdims=True)
        acc = alpha * acc + _dot(p.astype(BF16), c_b)
        m = m_new
    o_lat = (acc / l).astype(BF16)
    out = _dot(o_lat[0:t_new], wv_ref[0])
    for h in range(1, N_B_HEADS):
        out = out + _dot(o_lat[h * t_new:(h + 1) * t_new], wv_ref[h])
    o_ref[...] = out.astype(BF16)


def _attn_sample(q, cache_lat, cache_kr, new_lat, new_kr, wkt, wv_pad, *, layer, n_seq, t_new, tk):
    past = cache_lat.shape[2]
    assert past % tk == 0
    row_map = lambda b: (b, 0)
    return pl.pallas_call(
        functools.partial(_attn_sample_kernel, t_new, tk),
        grid=(n_seq,),
        in_specs=[pl.BlockSpec((t_new, N_B_HEADS * HEAD_BLK), row_map),
                  pl.BlockSpec((None, None, past, KV_LORA), lambda b: (layer, b, 0, 0)),
                  pl.BlockSpec((None, None, ROPE_DIM, past), lambda b: (layer, b, 0, 0)),
                  pl.BlockSpec((None, t_new, KV_LORA), lambda b: (layer, b, 0)),
                  pl.BlockSpec((None, t_new, ROPE_DIM), lambda b: (layer, b, 0)),
                  _layer_spec(wkt, layer), _layer_spec(wv_pad, layer)],
        out_specs=pl.BlockSpec((t_new, D_B), row_map),
        out_shape=jax.ShapeDtypeStruct((n_seq * t_new, D_B), BF16),
        compiler_params=pltpu.CompilerParams(dimension_semantics=("arbitrary",),
                                             vmem_limit_bytes=VMEM_LIMIT),
        name="attn_sample",
    )(q, cache_lat, cache_kr, new_lat, new_kr, wkt, wv_pad)


def _mixer_out_kernel(nb, tt, n_sub,
                      x_ref, a_ref, b_ref, c_ref, wo_ref, g1_ref, b1_ref,
                      wu_ref, cw_ref, cb_ref, wd_ref, g2_ref, b2_ref, s_ref,
                      y_ref, o_ref,
                      x1_ref, xp_ref, x1b_ref, acc_ref, bufg0_ref, bufg1_ref, bufv0_ref, bufv1_ref,
                      car_ref, act0_ref, act1_ref):
    bufg_ref = (bufg0_ref, bufg1_ref)
    bufv_ref = (bufv0_ref, bufv1_ref)
    act_ref = (act0_ref, act1_ref)
    t_idx = pl.program_id(1)
    rows = nb * tt
    n_tiles = rows // SUBLANE
    halo = (FFN_CONV - 1) * SUBLANE

    @pl.when(t_idx == 0)
    def _():
        car_ref[...] = s_ref[...]

    n_items = n_sub * N_FF

    def prologue(t):
        rs = slice(t * rows, (t + 1) * rows)
        mix = (_dot(a_ref[rs, :], wo_ref[0:D_A, :]) + _dot(b_ref[rs, :], wo_ref[D_A:D_A + D_B, :])
               + _dot(c_ref[rs, :], wo_ref[D_A + D_B:, :]))
        x1 = _layer_norm(ALPHA * x_ref[rs, :] + mix, g1_ref[...], b1_ref[...])
        x1_ref[rs, :] = x1
        for c in range(D_MODEL // LANE):
            for k in range(n_tiles):
                xp_ref[c, pl.ds(_interleave_start(k, n_tiles), SUBLANE, stride=SUBLANE), :] = (
                    x1[k * SUBLANE:(k + 1) * SUBLANE, c * LANE:(c + 1) * LANE])
            x1b_ref[t, :, c * LANE:(c + 1) * LANE] = xp_ref[c].astype(BF16)

    def epilogue(t):
        rs = slice(t * rows, (t + 1) * rows)
        ffn = jnp.concatenate(
            [jnp.concatenate([acc_ref[t, c, pl.ds(_interleave_start(k, n_tiles), SUBLANE, stride=SUBLANE), :]
                              for k in range(n_tiles)], axis=0)
             for c in range(D_MODEL // LANE)], axis=1)
        y_ref[rs, :] = _layer_norm(ALPHA * x1_ref[rs, :] + ffn, g2_ref[...], b2_ref[...])

    def cols(j, value_half):
        c0 = (D_FF if value_half else 0) + j * FF_CHUNK
        return slice(c0, c0 + FF_CHUNK)

    first_sublane = lax.broadcasted_iota(jnp.int32, (SUBLANE, FF_CHUNK), 0) == 0

    def stage(up, buf, cs):
        buf[halo:halo + rows, :] = up
        for i in range(FFN_CONV - 1):
            carry_rows = slice(i * SUBLANE, (i + 1) * SUBLANE)
            last = up[rows - halo + i * SUBLANE:rows - halo + (i + 1) * SUBLANE]
            if nb == 1:
                last = pltpu.roll(last, 1, 0)
                prev = jnp.where(first_sublane, car_ref[carry_rows, cs], last)
            else:
                prev = car_ref[carry_rows, cs]
            buf[carry_rows, :] = prev
            car_ref[carry_rows, cs] = last
            o_ref[carry_rows, cs] = last

    def conv_rows(buf, cs, r0, nr):
        acc = cb_ref[:, cs]
        for i in range(FFN_CONV):
            acc = acc + buf[r0 + i * SUBLANE:r0 + i * SUBLANE + nr, :] * cw_ref[i:i + 1, cs]
        return acc

    def up_proj(i):
        t, j = divmod(i, N_FF)
        xb = x1b_ref[t]
        stage(_dot(xb, wu_ref[:, cols(j, False)]), bufg_ref[i % 2], cols(j, False))
        stage(_dot(xb, wu_ref[:, cols(j, True)]), bufv_ref[i % 2], cols(j, True))

    groups_per_tile = -(-N_FF // DOWN_GROUP)

    def act_slot(t, grp):
        return act_ref[(t * groups_per_tile + grp) % 2]

    def elementwise(i):
        t, j = divmod(i, N_FF)
        grp, pos = divmod(j, DOWN_GROUP)
        for r0 in range(0, rows, ROW_BLK):
            nr = min(ROW_BLK, rows - r0)
            gate = conv_rows(bufg_ref[i % 2], cols(j, False), r0, nr)
            val = conv_rows(bufv_ref[i % 2], cols(j, True), r0, nr)
            act_slot(t, grp)[r0:r0 + nr, pos * FF_CHUNK:(pos + 1) * FF_CHUNK] = (
                jax.nn.gelu(gate) * val).astype(BF16)

    def down_proj(t, grp):
        n_chunks = min(DOWN_GROUP, N_FF - grp * DOWN_GROUP)
        k0, width = grp * DOWN_GROUP * FF_CHUNK, n_chunks * FF_CHUNK
        part = _dot(act_slot(t, grp)[:, 0:width], wd_ref[k0:k0 + width, :])
        for c in range(D_MODEL // LANE):
            if grp == 0:
                acc_ref[t, c] = part[:, c * LANE:(c + 1) * LANE]
            else:
                acc_ref[t, c] += part[:, c * LANE:(c + 1) * LANE]

    prologue(0)
    up_proj(0)
    pending = None
    for i in range(n_items + 1):
        if i + 1 < n_items:
            up_proj(i + 1)
        if i % N_FF == NEXT_PROLOGUE_AT and i // N_FF + 1 < n_sub:
            prologue(i // N_FF + 1)
        if pending is not None:
            down_proj(*pending)
            if pending[1] == groups_per_tile - 1:
                epilogue(pending[0])
            pending = None
        if i < n_items:
            elementwise(i)
            t, j = divmod(i, N_FF)
            if (j + 1) % DOWN_GROUP == 0 or j == N_FF - 1:
                pending = (t, j // DOWN_GROUP)


def _interleave_start(k, n_tiles):
    group, part = divmod(k, n_tiles // SUBLANE)
    return part * SUBLANE * SUBLANE + group


FFN_STATE_ROWS = (FFN_CONV - 1) * SUBLANE


def _mixer_out(x2d, a, b, c, lw, layer, state, *, n_seq, seq_len, nb, tt, n_sub):
    nt = seq_len // (tt * n_sub)
    assert nb == 1 or (nt == 1 and n_sub == 1)
    rows = nb * tt
    blk_rows = rows * n_sub
    assert nb in (1, SUBLANE) and rows % (SUBLANE * SUBLANE) == 0
    n_rows = n_seq * seq_len
    row_map = lambda bi, t: (bi * nt + t, 0)
    weights_a = [lw['w_o'], lw['ln1_g'], lw['ln1_b'], lw['w_up'], lw['ffn_cw'], lw['ffn_cb'],
                 lw['w_down'], lw['ln2_g'], lw['ln2_b']]
    st_spec = pl.BlockSpec((None, FFN_STATE_ROWS, 2 * D_FF), lambda bi, t: (bi, 0, 0))
    in_specs = ([pl.BlockSpec((blk_rows, D_MODEL), row_map), pl.BlockSpec((blk_rows, D_A), row_map),
                 pl.BlockSpec((blk_rows, D_B), row_map), pl.BlockSpec((blk_rows, D_C), row_map)]
                + [_layer_spec(w, layer, single_buffer=True) for w in weights_a]
                + [st_spec])
    st_shape = jax.ShapeDtypeStruct((n_seq // nb, FFN_STATE_ROWS, 2 * D_FF), F32)
    buf_rows = FFN_STATE_ROWS + rows
    return pl.pallas_call(
        functools.partial(_mixer_out_kernel, nb, tt, n_sub),
        grid=(n_seq // nb, nt),
        in_specs=in_specs,
        out_specs=[pl.BlockSpec((blk_rows, D_MODEL), row_map), st_spec],
        out_shape=[jax.ShapeDtypeStruct((n_rows, D_MODEL), F32), st_shape],
        scratch_shapes=[pltpu.VMEM((blk_rows, D_MODEL), F32),
                        pltpu.VMEM((D_MODEL // LANE, rows, LANE), F32),
                        pltpu.VMEM((n_sub, rows, D_MODEL), BF16),
                        pltpu.VMEM((n_sub, D_MODEL // LANE, rows, LANE), F32),
                        pltpu.VMEM((buf_rows, FF_CHUNK), F32),
                        pltpu.VMEM((buf_rows, FF_CHUNK), F32),
                        pltpu.VMEM((buf_rows, FF_CHUNK), F32),
                        pltpu.VMEM((buf_rows, FF_CHUNK), F32),
                        pltpu.VMEM((FFN_STATE_ROWS, 2 * D_FF), F32),
                        pltpu.VMEM((rows, DOWN_GROUP * FF_CHUNK), BF16),
                        pltpu.VMEM((rows, DOWN_GROUP * FF_CHUNK), BF16)],
        compiler_params=pltpu.CompilerParams(dimension_semantics=("arbitrary", "arbitrary"),
                                             vmem_limit_bytes=VMEM_LIMIT),
        name="mixer_out",
    )(x2d, a, b, c, *weights_a, state)


def _rope_tables(pos, reps, feature_major_q):
    half = ROPE_DIM // 2
    inv = 1.0 / (ROPE_THETA ** (jnp.arange(0, ROPE_DIM, 2, dtype=F32) / ROPE_DIM))
    ang = pos.astype(F32)[:, None] * inv[None, :]
    cos, sin = jnp.cos(ang), jnp.sin(ang)
    n = pos.shape[0]
    zeros = jnp.zeros((n, HEAD_BLK - ROPE_DIM), F32)
    tkc = jnp.concatenate([cos, cos, zeros], axis=1)
    tks = jnp.concatenate([sin, sin, zeros], axis=1)
    tqc = jnp.concatenate([cos, cos, jnp.ones((n, NOPE_DIM), F32),
                           jnp.zeros((n, HEAD_BLK - ROPE_DIM - NOPE_DIM), F32)], axis=1)
    tqc = jnp.concatenate([tqc, tqc], axis=1) * (ATTN_SCALE * LOG2E)
    tqs = jnp.concatenate([tks, tks], axis=1) * (ATTN_SCALE * LOG2E)
    if feature_major_q:
        return ((cos * (ATTN_SCALE * LOG2E)).T, (sin * (ATTN_SCALE * LOG2E)).T, tkc, tks)
    return tuple(jnp.tile(t, (reps, 1)) for t in (tqc, tqs, tkc, tks))


def _block_diag(w):
    depth, n, d, e = w.shape
    return jnp.einsum('lnde,nm->lndme', w, jnp.eye(n, dtype=w.dtype)).reshape(depth, n * d, n * e)


def _prep_weights(p):
    half = ROPE_DIM // 2
    depth = p['w_in'].shape[0]
    u, v, cq, ckv, kr, xc, gate = jnp.split(p['w_in'], [256, 512, 896, 1152, 1184, 1440], axis=2)
    kr1, kr2 = kr[..., :half], kr[..., half:]
    z = jnp.zeros((depth, D_MODEL, HEAD_BLK - ROPE_DIM), F32)
    w_in = jnp.concatenate([u, v, cq, ckv, kr1, kr2, z, -kr2, kr1, z, xc, gate], axis=2)

    wq = p['mla_w_uq'].reshape(depth, Q_LORA, N_B_HEADS, NOPE_DIM + ROPE_DIM)
    nope, r1, r2 = wq[..., :NOPE_DIM], wq[..., NOPE_DIM:NOPE_DIM + half], wq[..., NOPE_DIM + half:]
    zq = lambda k: jnp.zeros((depth, Q_LORA, N_B_HEADS, k), F32)
    wq_blk = jnp.concatenate([r1, r2, nope, zq(HEAD_BLK - ROPE_DIM - NOPE_DIM)], axis=-1)
    wq_blk = wq_blk.reshape(depth, Q_LORA, -1).astype(BF16)
    wq_swp = jnp.concatenate([-r2, r1, zq(HEAD_BLK - ROPE_DIM)], axis=-1).reshape(depth, Q_LORA, -1).astype(BF16)

    w_uk, w_uv = p['mla_w_uk'], p['mla_w_uv']
    zk = jnp.zeros((depth, KV_LORA, N_B_HEADS, ROPE_DIM), F32)
    wk_pad = jnp.concatenate([zk, w_uk, zk], axis=-1).reshape(depth, KV_LORA, -1)
    zt = jnp.zeros((depth, N_B_HEADS, ROPE_DIM, KV_LORA), F32)
    wkt = jnp.concatenate([zt, jnp.transpose(w_uk, (0, 2, 3, 1)), zt], axis=2)
    wv_pad = jnp.einsum('lchd,hg->lhcgd', w_uv, jnp.eye(N_B_HEADS, dtype=F32)).reshape(
        depth, N_B_HEADS, KV_LORA, D_B)
    w_uv2 = w_uv.reshape(depth, KV_LORA, -1).astype(BF16)

    row = lambda a: a.reshape(depth, 1, -1)
    return dict(
        w_in=w_in.astype(BF16),
        wq_blk=wq_blk, wq_swp=wq_swp,
        wq_blk_t=jnp.swapaxes(wq_blk, 1, 2),
        wk_pad=wk_pad.astype(BF16), w_uv_t=jnp.swapaxes(w_uv2, 1, 2),
        wkt=wkt.astype(BF16), wv_pad=wv_pad.astype(BF16),
        w_r=_block_diag(p['lru_w_r']).astype(BF16), w_i=_block_diag(p['lru_w_i']).astype(BF16),
        q_g=row(p['mla_q_norm_g']), kv_g=row(p['mla_kv_norm_g']),
        conv_w=p['lru_conv_w'], conv_b=row(p['lru_conv_b']),
        b_r=row(p['lru_b_r']), b_i=row(p['lru_b_i']), lam=row(p['lru_lam']),
        w_o=p['w_o'].astype(BF16),
        ln1_g=row(p['ln1_g']), ln1_b=row(p['ln1_b']), ln2_g=row(p['ln2_g']), ln2_b=row(p['ln2_b']),
        w_up=p['ffn_w_up'].astype(BF16),
        ffn_cw=jnp.pad(p['ffn_conv_w'], ((0, 0), (0, SUBLANE - FFN_CONV), (0, 0))),
        ffn_cb=row(p['ffn_conv_b']),
        w_down=p['ffn_w_down'].astype(BF16),
    )


def _gmlp_params(p, ln):
    w_s = p['gmlp_w_s'][:, :, :ln, :ln]
    b_s = jnp.repeat(jnp.swapaxes(p['gmlp_b_s'][:, :, :ln], 1, 2), A_HEAD, axis=2)
    return w_s, b_s


def _ffn_state_in(st, nb):
    n, steps, width = st.shape
    if nb == 1:
        return jnp.pad(st[:, :, None, :], ((0, 0), (0, 0), (0, SUBLANE - 1), (0, 0))).reshape(n, -1, width)
    return jnp.swapaxes(st.reshape(n // nb, nb, steps, width), 1, 2).reshape(n // nb, -1, width)


def _ffn_state_out(st, nb):
    n, _, width = st.shape
    st = st.reshape(n, FFN_CONV - 1, SUBLANE, width)
    if nb == 1:
        return st[:, :, 0, :]
    return jnp.swapaxes(st, 1, 2).reshape(n * nb, FFN_CONV - 1, width)


PROMPT_TT_IN = 512
PROMPT_TT_OUT = 256
PROMPT_SUB_OUT = 2
PROMPT_TQ = 512
PROMPT_TK = 512
SAMPLE_NB = 8
SAMPLE_TK = 1024


def kernel(x_prompt, x_sample, cache_kv_latent, cache_k_rope, state_lru_h, state_lru_conv, state_ffn_conv,
           ln1_g, ln1_b, ln2_g, ln2_b, w_in, w_o, gmlp_w_s, gmlp_b_s, mla_q_norm_g, mla_w_uq,
           mla_kv_norm_g, mla_w_uk, mla_w_uv, lru_conv_w, lru_conv_b, lru_w_r, lru_b_r, lru_w_i, lru_b_i,
           lru_lam, ffn_w_up, ffn_conv_w, ffn_conv_b, ffn_w_down):
    p = dict(ln1_g=ln1_g, ln1_b=ln1_b, ln2_g=ln2_g, ln2_b=ln2_b, w_in=w_in, w_o=w_o, gmlp_w_s=gmlp_w_s,
             gmlp_b_s=gmlp_b_s, mla_q_norm_g=mla_q_norm_g, mla_w_uq=mla_w_uq, mla_kv_norm_g=mla_kv_norm_g,
             mla_w_uk=mla_w_uk, mla_w_uv=mla_w_uv, lru_conv_w=lru_conv_w, lru_conv_b=lru_conv_b,
             lru_w_r=lru_w_r, lru_b_r=lru_b_r, lru_w_i=lru_w_i, lru_b_i=lru_b_i, lru_lam=lru_lam,
             ffn_w_up=ffn_w_up, ffn_conv_w=ffn_conv_w, ffn_conv_b=ffn_conv_b, ffn_w_down=ffn_w_down)
    bp, s_len, _ = x_prompt.shape
    bd, t_len, _ = x_sample.shape
    past = cache_kv_latent.shape[2]
    depth = w_in.shape[0]
    ln_p, ln_d = min(s_len, GMLP_CHUNK), min(t_len, GMLP_CHUNK)

    tabs_p = _rope_tables(jnp.arange(s_len), 1, True)
    tabs_d = _rope_tables(past + jnp.arange(t_len), SAMPLE_NB, False)
    zero_rows = jnp.zeros((bp, SUBLANE, D_C), F32)
    zero_ffn = jnp.zeros((bp, FFN_STATE_ROWS, 2 * D_FF), F32)

    xp = x_prompt.reshape(bp * s_len, D_MODEL)
    xd = x_sample.reshape(bd * t_len, D_MODEL)
    cache_kr_t = jnp.swapaxes(cache_k_rope, 2, 3)
    lw = _prep_weights(p)
    lw_p = dict(lw, **dict(zip(('w_s', 'b_s'), _gmlp_params(p, ln_p))))
    lw_d = dict(lw, **dict(zip(('w_s', 'b_s'), _gmlp_params(p, ln_d))))
    outs = {i: [] for i in (2, 3, 4, 7, 8, 9, 10)}
    stacked_p = stacked_d = None
    for l in range(depth):
        a, c, q, lat_p, kr_p, hst, cst, k, v = _mixer_in(
            xp, lw_p, l, tabs_p, zero_rows, zero_rows, stacked_p,
            n_seq=bp, seq_len=s_len, nb=1, tt=PROMPT_TT_IN, ln=ln_p, prompt=True)
        stacked_p = (lat_p, kr_p)
        b = _attn_prompt(q, k, v, n_seq=bp, seq_len=s_len, tq=PROMPT_TQ, tk=PROMPT_TK)
        xp, ffn_st = _mixer_out(xp, a, b, c, lw, l, zero_ffn,
                                n_seq=bp, seq_len=s_len, nb=1, tt=PROMPT_TT_OUT, n_sub=PROMPT_SUB_OUT)
        outs[2].append(hst[:, 0, :])
        outs[3].append(cst[:, SUBLANE - (LRU_CONV - 1):, :])
        outs[4].append(_ffn_state_out(ffn_st, 1))

        h0 = jnp.broadcast_to(state_lru_h[l][:, None, :], (bd, SUBLANE, D_C))
        conv0 = jnp.pad(state_lru_conv[l], ((0, 0), (SUBLANE - (LRU_CONV - 1), 0), (0, 0)))
        ffn_st0 = _ffn_state_in(state_ffn_conv[l], SAMPLE_NB)
        a, c, q, lat_d, kr_d, hst, cst, vg = _mixer_in(
            xd, lw_d, l, tabs_d, h0, conv0, stacked_d,
            n_seq=bd, seq_len=t_len, nb=SAMPLE_NB, tt=t_len, ln=ln_d, prompt=False)
        stacked_d = (lat_d, kr_d)
        b = _attn_sample(q, cache_kv_latent, cache_kr_t, lat_d, kr_d, lw['wkt'], lw['wv_pad'],
                         layer=l, n_seq=bd, t_new=t_len, tk=SAMPLE_TK)
        xd, ffn_st = _mixer_out(xd, a, b, c, lw, l, ffn_st0,
                                n_seq=bd, seq_len=t_len, nb=SAMPLE_NB, tt=t_len, n_sub=1)
        outs[7].append(vg.reshape(bd, t_len, D_A))
        outs[8].append(hst[:, 0, :])
        outs[9].append(cst[:, SUBLANE - (LRU_CONV - 1):, :])
        outs[10].append(_ffn_state_out(ffn_st, SAMPLE_NB))

    st = {i: jnp.stack(o) for i, o in outs.items()}
    return (xp.reshape(bp, s_len, D_MODEL), xd.reshape(bd, t_len, D_MODEL),
            lat_p.reshape(depth, bp, s_len, KV_LORA), kr_p.reshape(depth, bp, s_len, ROPE_DIM),
            st[2], st[3], st[4],
            lat_d.reshape(depth, bd, t_len, KV_LORA), kr_d.reshape(depth, bd, t_len, ROPE_DIM),
            st[7], st[8], st[9], st[10])
```

```python
import functools
import math

import jax
import jax.numpy as jnp
from jax import lax
from jax.experimental import pallas as pl
from jax.experimental.pallas import tpu as pltpu

F32 = jnp.float32
BF16 = jnp.bfloat16

D_MODEL = 1024
DEPTH = 4
CHUNK = 64
GMLP_CHUNK = 128
D_A = 256
N_A_HEADS = 4
A_HEAD = 64
D_B = 512
N_B_HEADS = 8
V_DIM = 64
NOPE_DIM = 64
ROPE_DIM = 32
Q_LORA = 384
KV_LORA = 256
ROPE_THETA = 10000.0
ATTN_SCALE = (NOPE_DIM + ROPE_DIM) ** -0.5
D_C = 256
LRU_CONV = 4
LRU_C = 8.0
D_FF = 2816
FFN_CONV = 3
ALPHA = (2.0 * DEPTH) ** 0.25
LN_EPS = 1e-5
RMS_EPS = 1e-6

LANE = 128
SUBLANE = 8
HEAD_BLK = 128
VMEM_LIMIT = 56 * 1024 * 1024

C_U = 0
C_V = C_U + D_A
C_CQ = C_V + D_A
C_CKV = C_CQ + Q_LORA
C_KR = C_CKV + KV_LORA
C_XC = C_KR + 2 * HEAD_BLK
C_GATE = C_XC + D_C
C_END = C_GATE + D_C
IN_SPLITS = (D_A, D_A, Q_LORA, KV_LORA, ROPE_DIM, D_C, D_C)

FF_CHUNK = 256
ROW_BLK = 64
DOWN_GROUP = 3
NEXT_PROLOGUE_AT = 1
N_FF = D_FF // FF_CHUNK


def _dot(a, b):
    return jnp.dot(a, b, preferred_element_type=F32)


def _dot_nt(a, b):
    return lax.dot_general(a, b, (((1,), (1,)), ((), ())), preferred_element_type=F32)


def _rms_norm(x, g):
    ms = jnp.mean(x * x, axis=-1, keepdims=True)
    return x * lax.rsqrt(ms + RMS_EPS) * g


def _layer_norm(x, g, b):
    mu = jnp.mean(x, axis=-1, keepdims=True)
    xc = x - mu
    var = jnp.mean(xc * xc, axis=-1, keepdims=True)
    return xc * lax.rsqrt(var + LN_EPS) * g + b


def _mixer_in_kernel(nb, tt, ln, prompt, n_aliased, x_ref, win_ref, wqb_ref, *rest):
    if prompt:
        wkp_ref, wuv_ref, rest = rest[0], rest[1], rest[2:]
    else:
        wqs_ref, rest = rest[0], rest[1:]
    (wr_ref, wi_ref, ws_ref, bs_ref, qg_ref, kvg_ref, cw_ref, cb_ref, br_ref, bi_ref, lam_ref,
     tqc_ref, tqs_ref, tkc_ref, tks_ref, h0_ref, conv0_ref) = rest[:17]
    rest = rest[17 + n_aliased:]
    if prompt:
        (a_ref, c_ref, q_ref, lat_ref, kr_ref, hst_ref, cst_ref, k_ref, v_ref,
         xb_ref, cv_ref, hc_ref) = rest
        vg_ref = None
    else:
        (a_ref, c_ref, q_ref, lat_ref, kr_ref, hst_ref, cst_ref, vg_ref,
         xb_ref, cv_ref, hc_ref) = rest
        k_ref = v_ref = None
    rows = nb * tt
    t_idx = pl.program_id(1)

    @pl.when(t_idx == 0)
    def _():
        cv_ref[:, 0:SUBLANE, :] = conv0_ref[...]
        hc_ref[...] = h0_ref[...]

    xb_ref[...] = x_ref[...].astype(BF16)

    def proj(c0, c1):
        return _dot(xb_ref[...], win_ref[:, c0:c1])

    xc_in = proj(C_XC, C_GATE)
    z_gate = proj(C_GATE, C_END)
    z_u = proj(C_U, C_V)
    z_v = proj(C_V, C_CQ)

    xcs = []
    for s in range(nb):
        cv_ref[s, SUBLANE:SUBLANE + tt, :] = xc_in[s * tt:(s + 1) * tt]
        acc = cb_ref[...]
        for j in range(LRU_CONV):
            off = SUBLANE - (LRU_CONV - 1) + j
            acc = acc + cv_ref[s, off:off + tt, :] * cw_ref[j:j + 1, :]
        xcs.append(acc)
        tail = cv_ref[s, tt:tt + SUBLANE, :]
        cst_ref[s] = tail
        cv_ref[s, 0:SUBLANE, :] = tail
    xc = xcs[0] if nb == 1 else jnp.concatenate(xcs, axis=0)
    xcb = xc.astype(BF16)
    z_r = _dot(xcb, wr_ref[...])
    z_i = _dot(xcb, wi_ref[...])
    z_cq = proj(C_CQ, C_CKV)
    z_ckv = proj(C_CKV, C_KR)
    kr2 = proj(C_KR, C_XC)

    gpb = ln // SUBLANE
    sub = lax.broadcasted_iota(jnp.int32, (gpb, SUBLANE, D_C), 1)
    soft_lam = jax.nn.softplus(-lam_ref[...])
    chain = {}

    def lru_block(c):
        r0 = c * ln
        seq, first = divmod(r0, tt)
        r = jax.nn.sigmoid(z_r[r0:r0 + ln] + br_ref[...])
        ig = jax.nn.sigmoid(z_i[r0:r0 + ln] + bi_ref[...])
        log_a = (-LRU_C) * r * soft_lam
        a = jnp.exp(log_a)
        b_in = jnp.sqrt(-jnp.tanh(log_a) * (a * a + 1.0)) * (ig * xc[r0:r0 + ln])
        a3 = a.reshape(gpb, SUBLANE, D_C)
        b3 = b_in.reshape(gpb, SUBLANE, D_C)
        for k in (1, 2, 4):
            keep = sub >= k
            a_sh = jnp.where(keep, pltpu.roll(a3, k, 1), 1.0)
            b_sh = jnp.where(keep, pltpu.roll(b3, k, 1), 0.0)
            b3 = a3 * b_sh + b3
            a3 = a3 * a_sh
        hb = hc_ref[seq] if first == 0 else chain['h']
        h_rows = []
        for g in range(gpb):
            hr = a3[g] * hb + b3[g]
            h_rows.append(hr)
            hb = jnp.broadcast_to(hr[SUBLANE - 1:SUBLANE, :], (SUBLANE, D_C))
        chain['h'] = hb
        if first + ln == tt:
            hc_ref[seq] = hb
            hst_ref[seq] = hb
        c_ref[r0:r0 + ln, :] = (jnp.concatenate(h_rows, axis=0)
                                * jax.nn.gelu(z_gate[r0:r0 + ln])).astype(BF16)

    cqn = _rms_norm(z_cq, qg_ref[...]).astype(BF16)
    tqc = tqc_ref[...]
    tqs = tqs_ref[...]
    ckvn = _rms_norm(z_ckv, kvg_ref[...])
    def put_layer(ref, val):
        if n_aliased:
            ref[...] = val
        else:
            ref[0] = val
            ref[1:] = jnp.zeros((ref.shape[0] - 1,) + val.shape, val.dtype)

    put_layer(lat_ref, ckvn)
    ckvn_b = ckvn.astype(BF16)
    kr_rot = kr2[:, :HEAD_BLK] * tkc_ref[...] + kr2[:, HEAD_BLK:] * tks_ref[...]
    put_layer(kr_ref, kr_rot[:, :ROPE_DIM])

    v = jax.nn.gelu(z_v)
    if vg_ref is not None:
        vg_ref[...] = v
    vb = v.astype(BF16)
    row_i = lax.broadcasted_iota(jnp.int32, (ln, ln), 0)
    col_i = lax.broadcasted_iota(jnp.int32, (ln, ln), 1)
    w_tril = [jnp.where(row_i >= col_i, ws_ref[h], 0.0).astype(BF16) for h in range(N_A_HEADS)]
    head_of_lane = lax.broadcasted_iota(jnp.int32, (ln, D_A), 1) // A_HEAD

    def gmlp_chunk(c):
        r0 = c * ln
        vc = vb[r0:r0 + ln]
        gate = bs_ref[...]
        for h in range(N_A_HEADS):
            gate = gate + _dot(w_tril[h], jnp.where(head_of_lane == h, vc, jnp.zeros_like(vc)))
        a_ref[r0:r0 + ln, :] = (jax.nn.gelu(z_u[r0:r0 + ln]) * gate).astype(BF16)

    def q_pair(j):
        c0, c1 = 2 * j * HEAD_BLK, 2 * (j + 1) * HEAD_BLK
        if prompt:
            qb = _dot_nt(wqb_ref[c0:c1, :], cqn)
            half = ROPE_DIM // 2
            for i in range(2):
                r0 = i * HEAD_BLK
                r1, r2 = qb[r0:r0 + half], qb[r0 + half:r0 + ROPE_DIM]
                rest = qb[r0 + ROPE_DIM:r0 + HEAD_BLK] * (ATTN_SCALE * LOG2E)
                blk = jnp.concatenate([r1 * tqc - r2 * tqs, r1 * tqs + r2 * tqc, rest], axis=0)
                q_ref[c0 + r0:c0 + r0 + HEAD_BLK, :] = blk.astype(BF16)
        else:
            q_ref[:, c0:c1] = (_dot(cqn, wqb_ref[:, c0:c1]) * tqc
                               + _dot(cqn, wqs_ref[:, c0:c1]) * tqs).astype(BF16)

    def k_pair(j):
        c0 = 2 * j * HEAD_BLK
        kp = _dot(ckvn_b, wkp_ref[:, c0:c0 + 2 * HEAD_BLK])
        k_ref[:, c0:c0 + HEAD_BLK] = (kp[:, :HEAD_BLK] + kr_rot).astype(BF16)
        k_ref[:, c0 + HEAD_BLK:c0 + 2 * HEAD_BLK] = (kp[:, HEAD_BLK:] + kr_rot).astype(BF16)

    def v_all():
        v_t = _dot_nt(wuv_ref[...], ckvn_b).astype(BF16)
        for h in range(N_B_HEADS):
            v_ref[h * V_AUG:h * V_AUG + V_DIM, :] = v_t[h * V_DIM:(h + 1) * V_DIM]
            v_ref[h * V_AUG + V_DIM:(h + 1) * V_AUG, :] = jnp.ones((V_ONES, rows), BF16)

    n_blocks = rows // ln
    mla = [functools.partial(q_pair, j) for j in range(N_B_HEADS // 2)]
    if prompt:
        mla += [functools.partial(k_pair, j) for j in range(N_B_HEADS // 2)] + [v_all]
    per_block = -(-len(mla) // n_blocks)
    for c in range(n_blocks):
        for piece in mla[c * per_block:(c + 1) * per_block]:
            piece()
        gmlp_chunk(c)
        lru_block(c)


def _layer_spec(w, layer, single_buffer=False):
    tail = (0,) * (w.ndim - 1)
    mode = dict(pipeline_mode=pl.Buffered(1)) if single_buffer else {}
    return pl.BlockSpec((None,) + w.shape[1:], lambda *_: (layer,) + tail, **mode)


def _mixer_in(x2d, lw, layer, tabs, h0, conv0, stacked, *, n_seq, seq_len, nb, tt, ln, prompt):
    depth = lw['w_in'].shape[0]
    nt = seq_len // tt
    assert nb == 1 or nt == 1
    rows = nb * tt
    n_rows = n_seq * seq_len
    grid = (n_seq // nb, nt)
    row_map = lambda b, t: (b * nt + t, 0)
    stacked = tuple(stacked or ())

    def layer_out_spec(width):
        if stacked:
            return pl.BlockSpec((None, rows, width), lambda b, t: (layer, b * nt + t, 0))
        assert layer == 0
        return pl.BlockSpec((depth, rows, width), lambda b, t: (0, b * nt + t, 0))

    tab_map = lambda b, t: (t, 0)
    seq_map = lambda b, t: (b, 0, 0)

    mode_weights = [lw['wq_blk_t'], lw['wk_pad'], lw['w_uv_t']] if prompt else [lw['wq_blk'], lw['wq_swp']]
    weights = [lw['w_in']] + mode_weights + [
        lw['w_r'], lw['w_i'], lw['w_s'], lw['b_s'], lw['q_g'], lw['kv_g'], lw['conv_w'], lw['conv_b'],
        lw['b_r'], lw['b_i'], lw['lam']]
    col_map = lambda b, t: (0, b * nt + t)
    if prompt:
        q_tab_spec = pl.BlockSpec((ROPE_DIM // 2, rows), lambda b, t: (0, t))
        q_shape = jax.ShapeDtypeStruct((N_B_HEADS * HEAD_BLK, n_rows), BF16)
        q_spec = pl.BlockSpec((N_B_HEADS * HEAD_BLK, rows), col_map)
    else:
        q_tab_spec = pl.BlockSpec((rows, 2 * HEAD_BLK), tab_map)
        q_shape = jax.ShapeDtypeStruct((n_rows, N_B_HEADS * HEAD_BLK), BF16)
        q_spec = pl.BlockSpec((rows, N_B_HEADS * HEAD_BLK), row_map)
    in_specs = ([pl.BlockSpec((rows, D_MODEL), row_map)]
                + [_layer_spec(w, layer) for w in weights]
                + [q_tab_spec, q_tab_spec,
                   pl.BlockSpec((rows, HEAD_BLK), tab_map), pl.BlockSpec((rows, HEAD_BLK), tab_map),
                   pl.BlockSpec((nb, SUBLANE, D_C), seq_map), pl.BlockSpec((nb, SUBLANE, D_C), seq_map)])
    out_shape = [jax.ShapeDtypeStruct((n_rows, D_A), BF16),
                 jax.ShapeDtypeStruct((n_rows, D_C), BF16),
                 q_shape,
                 jax.ShapeDtypeStruct((depth, n_rows, KV_LORA), F32),
                 jax.ShapeDtypeStruct((depth, n_rows, ROPE_DIM), F32),
                 jax.ShapeDtypeStruct((n_seq, SUBLANE, D_C), F32),
                 jax.ShapeDtypeStruct((n_seq, SUBLANE, D_C), F32)]
    out_specs = [pl.BlockSpec((rows, D_A), row_map), pl.BlockSpec((rows, D_C), row_map),
                 q_spec, layer_out_spec(KV_LORA), layer_out_spec(ROPE_DIM),
                 pl.BlockSpec((nb, SUBLANE, D_C), seq_map), pl.BlockSpec((nb, SUBLANE, D_C), seq_map)]
    if prompt:
        out_shape += [jax.ShapeDtypeStruct((n_rows, N_B_HEADS * HEAD_BLK), BF16),
                      jax.ShapeDtypeStruct((N_B_HEADS * V_AUG, n_rows), BF16)]
        out_specs += [pl.BlockSpec((rows, N_B_HEADS * HEAD_BLK), row_map),
                      pl.BlockSpec((N_B_HEADS * V_AUG, rows), col_map)]
    else:
        out_shape += [jax.ShapeDtypeStruct((n_rows, D_A), F32)]
        out_specs += [pl.BlockSpec((rows, D_A), row_map)]
    scratch = [pltpu.VMEM((rows, D_MODEL), BF16),
               pltpu.VMEM((nb, tt + SUBLANE, D_C), F32),
               pltpu.VMEM((nb, SUBLANE, D_C), F32)]
    n_in = len(in_specs)
    in_specs += [pl.BlockSpec(memory_space=pl.ANY)] * len(stacked)
    aliases = {n_in + i: 3 + i for i in range(len(stacked))}
    return pl.pallas_call(
        functools.partial(_mixer_in_kernel, nb, tt, ln, prompt, len(stacked)),
        grid=grid, in_specs=in_specs, out_specs=out_specs, out_shape=out_shape,
        scratch_shapes=scratch, input_output_aliases=aliases,
        compiler_params=pltpu.CompilerParams(dimension_semantics=("arbitrary", "arbitrary"),
                                             vmem_limit_bytes=VMEM_LIMIT),
        name="mixer_in_prompt" if prompt else "mixer_in_sample",
    )(x2d, *weights, *tabs, h0, conv0, *stacked)


NEG_BIG = -1e30
LOG2E = math.log2(math.e)
V_ONES = 16
V_AUG = V_DIM + V_ONES
ATTN_HEAD_GROUP = 2


def _last_kv_tile(qi, tq, tk):
    return ((qi + 1) * tq - 1) // tk


def _kv_steps(nq, tq, tk):
    pairs = [(i, j) for i in range(nq) for j in range(_last_kv_tile(i, tq, tk) + 1)]
    return (jnp.asarray([p[0] for p in pairs], jnp.int32), jnp.asarray([p[1] for p in pairs], jnp.int32))


def _attn_prompt_kernel(tq, tk, qi_tab, ki_tab, q_ref, k_ref, v_ref, o_ref, m_ref, acc_ref):
    step_idx = pl.program_id(1)
    qi = qi_tab[step_idx]
    ki = ki_tab[step_idx]

    @pl.when(ki == 0)
    def _():
        m_ref[...] = jnp.full(m_ref.shape, NEG_BIG, F32)
        acc_ref[...] = jnp.zeros(acc_ref.shape, F32)

    half = tq // 2

    def step(diag):
        if diag:
            local = (lax.broadcasted_iota(jnp.int32, (half, half), 0) // CHUNK
                     <= lax.broadcasted_iota(jnp.int32, (half, half), 1) // CHUNK)
            parts = ((0, half, half), (half, tq, tk))
        else:
            parts = ((0, tq, tk),)

        def scores(h):
            blk = slice(h * HEAD_BLK, (h + 1) * HEAD_BLK)
            return [_dot(k_ref[0:nk, blk], q_ref[blk, c0:c1]) for (c0, c1, nk) in parts]

        def softmax(h, s_parts):
            out = []
            for (c0, c1, nk), s in zip(parts, s_parts):
                if diag:
                    low = jnp.where(local, s[nk - half:], NEG_BIG)
                    s = low if nk == half else jnp.concatenate([s[:nk - half], low], axis=0)
                m_prev = m_ref[h, :, c0:c1]
                m_new = jnp.maximum(m_prev, jnp.max(s, axis=0, keepdims=True))
                m_ref[h, :, c0:c1] = m_new
                out.append((jnp.exp2(m_prev - m_new), jnp.exp2(s - m_new).astype(BF16)))
            return out

        group = ATTN_HEAD_GROUP
        s_next = [scores(h) for h in range(group)]
        for h0 in range(0, N_B_HEADS, group):
            s_cur = s_next
            if h0 + group < N_B_HEADS:
                s_next = [scores(h) for h in range(h0 + group, h0 + 2 * group)]
            probs = [softmax(h0 + i, s_cur[i]) for i in range(group)]
            for i, prob_parts in enumerate(probs):
                rows = slice((h0 + i) * V_AUG, (h0 + i + 1) * V_AUG)
                for (c0, c1, nk), (alpha, p) in zip(parts, prob_parts):
                    acc_ref[rows, c0:c1] = alpha * acc_ref[rows, c0:c1] + _dot(v_ref[rows, 0:nk], p)

    def finalize():
        for j in range(N_B_HEADS // 2):
            halves = []
            for h in (2 * j, 2 * j + 1):
                denom = acc_ref[h * V_AUG + V_DIM:h * V_AUG + V_DIM + 1, :]
                halves.append(acc_ref[h * V_AUG:h * V_AUG + V_DIM, :] / denom)
            o_ref[:, j * LANE:(j + 1) * LANE] = jnp.concatenate(halves, axis=0).T.astype(BF16)

    @pl.when(ki == qi)
    def _():
        step(True)
        finalize()

    @pl.when(ki != qi)
    def _():
        step(False)


def _attn_prompt(q_t, k, v_t, *, n_seq, seq_len, tq, tk):
    assert tq == tk and (tq // 2) % CHUNK == 0
    nq, nk = seq_len // tq, seq_len // tk
    qi_tab, ki_tab = _kv_steps(nq, tq, tk)
    grid_spec = pltpu.PrefetchScalarGridSpec(
        num_scalar_prefetch=2,
        grid=(n_seq, qi_tab.shape[0]),
        in_specs=[pl.BlockSpec((N_B_HEADS * HEAD_BLK, tq), lambda b, s, qt, kt: (0, b * nq + qt[s])),
                  pl.BlockSpec((tk, N_B_HEADS * HEAD_BLK), lambda b, s, qt, kt: (b * nk + kt[s], 0)),
                  pl.BlockSpec((N_B_HEADS * V_AUG, tk), lambda b, s, qt, kt: (0, b * nk + kt[s]))],
        out_specs=pl.BlockSpec((tq, D_B), lambda b, s, qt, kt: (b * nq + qt[s], 0)),
        scratch_shapes=[pltpu.VMEM((N_B_HEADS, 1, tq), F32),
                        pltpu.VMEM((N_B_HEADS * V_AUG, tq), F32)])
    return pl.pallas_call(
        functools.partial(_attn_prompt_kernel, tq, tk),
        grid_spec=grid_spec,
        out_shape=jax.ShapeDtypeStruct((n_seq * seq_len, D_B), BF16),
        compiler_params=pltpu.CompilerParams(dimension_semantics=("arbitrary", "arbitrary"),
                                             vmem_limit_bytes=VMEM_LIMIT),
        name="attn_prompt",
    )(qi_tab, ki_tab, q_t, k, v_t)


def _attn_sample_kernel(t_new, tk, q_ref, clat_ref, ckr_ref, nlat_ref, nkr_ref, wkt_ref, wv_ref, o_ref):
    past = clat_ref.shape[0]
    ql, qr = [], []
    for h in range(N_B_HEADS):
        q_h = q_ref[:, h * HEAD_BLK:(h + 1) * HEAD_BLK]
        ql.append(_dot(q_h, wkt_ref[h]).astype(BF16))
        qr.append(q_h[:, :ROPE_DIM])
    ql = jnp.concatenate(ql, axis=0)
    qr = jnp.concatenate(qr, axis=0)

    def scores(i):
        if i == 0:
            c_b = nlat_ref[...].astype(BF16)
            return _dot_nt(ql, c_b) + _dot_nt(qr, nkr_ref[...].astype(BF16)), c_b
        r0 = (i - 1) * tk
        c_b = clat_ref[r0:r0 + tk, :].astype(BF16)
        return _dot_nt(ql, c_b) + _dot(qr, ckr_ref[:, r0:r0 + tk].astype(BF16)), c_b

    rows = N_B_HEADS * t_new
    m = jnp.full((rows, 1), NEG_BIG, F32)
    l = jnp.zeros((rows, 1), F32)
    acc = jnp.zeros((rows, KV_LORA), F32)
    n_tiles = 1 + past // tk
    nxt = scores(0)
    for i in range(n_tiles):
        s, c_b = nxt
        if i + 1 < n_tiles:
            nxt = scores(i + 1)
        m_new = jnp.maximum(m, jnp.max(s, axis=1, keepdims=True))
        alpha = jnp.exp2(m - m_new)
        p = jnp.exp2(s - m_new)
        l = alpha * l + jnp.sum(p, axis=1, keepdims=True)
        acc = alpha * acc + _dot(p.astype(BF16), c_b)
        m = m_new
    o_lat = (acc / l).astype(BF16)
    out = _dot(o_lat[0:t_new], wv_ref[0])
    for h in range(1, N_B_HEADS):
        out = out + _dot(o_lat[h * t_new:(h + 1) * t_new], wv_ref[h])
    o_ref[...] = out.astype(BF16)


def _attn_sample(q, cache_lat, cache_kr, new_lat, new_kr, wkt, wv_pad, *, layer, n_seq, t_new, tk):
    past = cache_lat.shape[2]
    assert past % tk == 0
    row_map = lambda b: (b, 0)
    return pl.pallas_call(
        functools.partial(_attn_sample_kernel, t_new, tk),
        grid=(n_seq,),
        in_specs=[pl.BlockSpec((t_new, N_B_HEADS * HEAD_BLK), row_map),
                  pl.BlockSpec((None, None, past, KV_LORA), lambda b: (layer, b, 0, 0)),
                  pl.BlockSpec((None, None, ROPE_DIM, past), lambda b: (layer, b, 0, 0)),
                  pl.BlockSpec((None, t_new, KV_LORA), lambda b: (layer, b, 0)),
                  pl.BlockSpec((None, t_new, ROPE_DIM), lambda b: (layer, b, 0)),
                  _layer_spec(wkt, layer), _layer_spec(wv_pad, layer)],
        out_specs=pl.BlockSpec((t_new, D_B), row_map),
        out_shape=jax.ShapeDtypeStruct((n_seq * t_new, D_B), BF16),
        compiler_params=pltpu.CompilerParams(dimension_semantics=("arbitrary",),
                                             vmem_limit_bytes=VMEM_LIMIT),
        name="attn_sample",
    )(q, cache_lat, cache_kr, new_lat, new_kr, wkt, wv_pad)


def _mixer_out_kernel(nb, tt, n_sub,
                      x_ref, a_ref, b_ref, c_ref, wo_ref, g1_ref, b1_ref,
                      wu_ref, cw_ref, cb_ref, wd_ref, g2_ref, b2_ref, s_ref,
                      y_ref, o_ref,
                      x1_ref, xp_ref, x1b_ref, acc_ref, bufg0_ref, bufg1_ref, bufv0_ref, bufv1_ref,
                      car_ref, act0_ref, act1_ref):
    bufg_ref = (bufg0_ref, bufg1_ref)
    bufv_ref = (bufv0_ref, bufv1_ref)
    act_ref = (act0_ref, act1_ref)
    t_idx = pl.program_id(1)
    rows = nb * tt
    n_tiles = rows // SUBLANE
    halo = (FFN_CONV - 1) * SUBLANE

    @pl.when(t_idx == 0)
    def _():
        car_ref[...] = s_ref[...]

    n_items = n_sub * N_FF

    def prologue(t):
        rs = slice(t * rows, (t + 1) * rows)
        mix = (_dot(a_ref[rs, :], wo_ref[0:D_A, :]) + _dot(b_ref[rs, :], wo_ref[D_A:D_A + D_B, :])
               + _dot(c_ref[rs, :], wo_ref[D_A + D_B:, :]))
        x1 = _layer_norm(ALPHA * x_ref[rs, :] + mix, g1_ref[...], b1_ref[...])
        x1_ref[rs, :] = x1
        for c in range(D_MODEL // LANE):
            for k in range(n_tiles):
                xp_ref[c, pl.ds(_interleave_start(k, n_tiles), SUBLANE, stride=SUBLANE), :] = (
                    x1[k * SUBLANE:(k + 1) * SUBLANE, c * LANE:(c + 1) * LANE])
            x1b_ref[t, :, c * LANE:(c + 1) * LANE] = xp_ref[c].astype(BF16)

    def epilogue(t):
        rs = slice(t * rows, (t + 1) * rows)
        ffn = jnp.concatenate(
            [jnp.concatenate([acc_ref[t, c, pl.ds(_interleave_start(k, n_tiles), SUBLANE, stride=SUBLANE), :]
                              for k in range(n_tiles)], axis=0)
             for c in range(D_MODEL // LANE)], axis=1)
        y_ref[rs, :] = _layer_norm(ALPHA * x1_ref[rs, :] + ffn, g2_ref[...], b2_ref[...])

    def cols(j, value_half):
        c0 = (D_FF if value_half else 0) + j * FF_CHUNK
        return slice(c0, c0 + FF_CHUNK)

    first_sublane = lax.broadcasted_iota(jnp.int32, (SUBLANE, FF_CHUNK), 0) == 0

    def stage(up, buf, cs):
        buf[halo:halo + rows, :] = up
        for i in range(FFN_CONV - 1):
            carry_rows = slice(i * SUBLANE, (i + 1) * SUBLANE)
            last = up[rows - halo + i * SUBLANE:rows - halo + (i + 1) * SUBLANE]
            if nb == 1:
                last = pltpu.roll(last, 1, 0)
                prev = jnp.where(first_sublane, car_ref[carry_rows, cs], last)
            else:
                prev = car_ref[carry_rows, cs]
            buf[carry_rows, :] = prev
            car_ref[carry_rows, cs] = last
            o_ref[carry_rows, cs] = last

    def conv_rows(buf, cs, r0, nr):
        acc = cb_ref[:, cs]
        for i in range(FFN_CONV):
            acc = acc + buf[r0 + i * SUBLANE:r0 + i * SUBLANE + nr, :] * cw_ref[i:i + 1, cs]
        return acc

    def up_proj(i):
        t, j = divmod(i, N_FF)
        xb = x1b_ref[t]
        stage(_dot(xb, wu_ref[:, cols(j, False)]), bufg_ref[i % 2], cols(j, False))
        stage(_dot(xb, wu_ref[:, cols(j, True)]), bufv_ref[i % 2], cols(j, True))

    groups_per_tile = -(-N_FF // DOWN_GROUP)

    def act_slot(t, grp):
        return act_ref[(t * groups_per_tile + grp) % 2]

    def elementwise(i):
        t, j = divmod(i, N_FF)
        grp, pos = divmod(j, DOWN_GROUP)
        for r0 in range(0, rows, ROW_BLK):
            nr = min(ROW_BLK, rows - r0)
            gate = conv_rows(bufg_ref[i % 2], cols(j, False), r0, nr)
            val = conv_rows(bufv_ref[i % 2], cols(j, True), r0, nr)
            act_slot(t, grp)[r0:r0 + nr, pos * FF_CHUNK:(pos + 1) * FF_CHUNK] = (
                jax.nn.gelu(gate) * val).astype(BF16)

    def down_proj(t, grp):
        n_chunks = min(DOWN_GROUP, N_FF - grp * DOWN_GROUP)
        k0, width = grp * DOWN_GROUP * FF_CHUNK, n_chunks * FF_CHUNK
        part = _dot(act_slot(t, grp)[:, 0:width], wd_ref[k0:k0 + width, :])
        for c in range(D_MODEL // LANE):
            if grp == 0:
                acc_ref[t, c] = part[:, c * LANE:(c + 1) * LANE]
            else:
                acc_ref[t, c] += part[:, c * LANE:(c + 1) * LANE]

    prologue(0)
    up_proj(0)
    pending = None
    for i in range(n_items + 1):
        if i + 1 < n_items:
            up_proj(i + 1)
        if i % N_FF == NEXT_PROLOGUE_AT and i // N_FF + 1 < n_sub:
            prologue(i // N_FF + 1)
        if pending is not None:
            down_proj(*pending)
            if pending[1] == groups_per_tile - 1:
                epilogue(pending[0])
            pending = None
        if i < n_items:
            elementwise(i)
            t, j = divmod(i, N_FF)
            if (j + 1) % DOWN_GROUP == 0 or j == N_FF - 1:
                pending = (t, j // DOWN_GROUP)


def _interleave_start(k, n_tiles):
    group, part = divmod(k, n_tiles // SUBLANE)
    return part * SUBLANE * SUBLANE + group


FFN_STATE_ROWS = (FFN_CONV - 1) * SUBLANE


def _mixer_out(x2d, a, b, c, lw, layer, state, *, n_seq, seq_len, nb, tt, n_sub):
    nt = seq_len // (tt * n_sub)
    assert nb == 1 or (nt == 1 and n_sub == 1)
    rows = nb * tt
    blk_rows = rows * n_sub
    assert nb in (1, SUBLANE) and rows % (SUBLANE * SUBLANE) == 0
    n_rows = n_seq * seq_len
    row_map = lambda bi, t: (bi * nt + t, 0)
    weights_a = [lw['w_o'], lw['ln1_g'], lw['ln1_b'], lw['w_up'], lw['ffn_cw'], lw['ffn_cb'],
                 lw['w_down'], lw['ln2_g'], lw['ln2_b']]
    st_spec = pl.BlockSpec((None, FFN_STATE_ROWS, 2 * D_FF), lambda bi, t: (bi, 0, 0))
    in_specs = ([pl.BlockSpec((blk_rows, D_MODEL), row_map), pl.BlockSpec((blk_rows, D_A), row_map),
                 pl.BlockSpec((blk_rows, D_B), row_map), pl.BlockSpec((blk_rows, D_C), row_map)]
                + [_layer_spec(w, layer, single_buffer=True) for w in weights_a]
                + [st_spec])
    st_shape = jax.ShapeDtypeStruct((n_seq // nb, FFN_STATE_ROWS, 2 * D_FF), F32)
    buf_rows = FFN_STATE_ROWS + rows
    return pl.pallas_call(
        functools.partial(_mixer_out_kernel, nb, tt, n_sub),
        grid=(n_seq // nb, nt),
        in_specs=in_specs,
        out_specs=[pl.BlockSpec((blk_rows, D_MODEL), row_map), st_spec],
        out_shape=[jax.ShapeDtypeStruct((n_rows, D_MODEL), F32), st_shape],
        scratch_shapes=[pltpu.VMEM((blk_rows, D_MODEL), F32),
                        pltpu.VMEM((D_MODEL // LANE, rows, LANE), F32),
                        pltpu.VMEM((n_sub, rows, D_MODEL), BF16),
                        pltpu.VMEM((n_sub, D_MODEL // LANE, rows, LANE), F32),
                        pltpu.VMEM((buf_rows, FF_CHUNK), F32),
                        pltpu.VMEM((buf_rows, FF_CHUNK), F32),
                        pltpu.VMEM((buf_rows, FF_CHUNK), F32),
                        pltpu.VMEM((buf_rows, FF_CHUNK), F32),
                        pltpu.VMEM((FFN_STATE_ROWS, 2 * D_FF), F32),
                        pltpu.VMEM((rows, DOWN_GROUP * FF_CHUNK), BF16),
                        pltpu.VMEM((rows, DOWN_GROUP * FF_CHUNK), BF16)],
        compiler_params=pltpu.CompilerParams(dimension_semantics=("arbitrary", "arbitrary"),
                                             vmem_limit_bytes=VMEM_LIMIT),
        name="mixer_out",
    )(x2d, a, b, c, *weights_a, state)


def _rope_tables(pos, reps, feature_major_q):
    half = ROPE_DIM // 2
    inv = 1.0 / (ROPE_THETA ** (jnp.arange(0, ROPE_DIM, 2, dtype=F32) / ROPE_DIM))
    ang = pos.astype(F32)[:, None] * inv[None, :]
    cos, sin = jnp.cos(ang), jnp.sin(ang)
    n = pos.shape[0]
    zeros = jnp.zeros((n, HEAD_BLK - ROPE_DIM), F32)
    tkc = jnp.concatenate([cos, cos, zeros], axis=1)
    tks = jnp.concatenate([sin, sin, zeros], axis=1)
    tqc = jnp.concatenate([cos, cos, jnp.ones((n, NOPE_DIM), F32),
                           jnp.zeros((n, HEAD_BLK - ROPE_DIM - NOPE_DIM), F32)], axis=1)
    tqc = jnp.concatenate([tqc, tqc], axis=1) * (ATTN_SCALE * LOG2E)
    tqs = jnp.concatenate([tks, tks], axis=1) * (ATTN_SCALE * LOG2E)
    if feature_major_q:
        return ((cos * (ATTN_SCALE * LOG2E)).T, (sin * (ATTN_SCALE * LOG2E)).T, tkc, tks)
    return tuple(jnp.tile(t, (reps, 1)) for t in (tqc, tqs, tkc, tks))


def _block_diag(w):
    depth, n, d, e = w.shape
    return jnp.einsum('lnde,nm->lndme', w, jnp.eye(n, dtype=w.dtype)).reshape(depth, n * d, n * e)


def _prep_weights(p):
    half = ROPE_DIM // 2
    depth = p['w_in'].shape[0]
    bounds = [sum(IN_SPLITS[:i + 1]) for i in range(len(IN_SPLITS) - 1)]
    u, v, cq, ckv, kr, xc, gate = jnp.split(p['w_in'], bounds, axis=2)
    kr1, kr2 = kr[..., :half], kr[..., half:]
    z = jnp.zeros((depth, D_MODEL, HEAD_BLK - ROPE_DIM), F32)
    w_in = jnp.concatenate([u, v, cq, ckv, kr1, kr2, z, -kr2, kr1, z, xc, gate], axis=2)

    wq = p['mla_w_uq'].reshape(depth, Q_LORA, N_B_HEADS, NOPE_DIM + ROPE_DIM)
    nope, r1, r2 = wq[..., :NOPE_DIM], wq[..., NOPE_DIM:NOPE_DIM + half], wq[..., NOPE_DIM + half:]
    zq = lambda k: jnp.zeros((depth, Q_LORA, N_B_HEADS, k), F32)
    wq_blk = jnp.concatenate([r1, r2, nope, zq(HEAD_BLK - ROPE_DIM - NOPE_DIM)], axis=-1)
    wq_blk = wq_blk.reshape(depth, Q_LORA, -1).astype(BF16)
    wq_swp = jnp.concatenate([-r2, r1, zq(HEAD_BLK - ROPE_DIM)], axis=-1).reshape(depth, Q_LORA, -1).astype(BF16)

    w_uk, w_uv = p['mla_w_uk'], p['mla_w_uv']
    zk = jnp.zeros((depth, KV_LORA, N_B_HEADS, ROPE_DIM), F32)
    wk_pad = jnp.concatenate([zk, w_uk, zk], axis=-1).reshape(depth, KV_LORA, -1)
    zt = jnp.zeros((depth, N_B_HEADS, ROPE_DIM, KV_LORA), F32)
    wkt = jnp.concatenate([zt, jnp.transpose(w_uk, (0, 2, 3, 1)), zt], axis=2)
    wv_pad = jnp.einsum('lchd,hg->lhcgd', w_uv, jnp.eye(N_B_HEADS, dtype=F32)).reshape(
        depth, N_B_HEADS, KV_LORA, D_B)
    w_uv2 = w_uv.reshape(depth, KV_LORA, -1).astype(BF16)

    row = lambda a: a.reshape(depth, 1, -1)
    return dict(
        w_in=w_in.astype(BF16),
        wq_blk=wq_blk, wq_swp=wq_swp,
        wq_blk_t=jnp.swapaxes(wq_blk, 1, 2),
        wk_pad=wk_pad.astype(BF16), w_uv_t=jnp.swapaxes(w_uv2, 1, 2),
        wkt=wkt.astype(BF16), wv_pad=wv_pad.astype(BF16),
        w_r=_block_diag(p['lru_w_r']).astype(BF16), w_i=_block_diag(p['lru_w_i']).astype(BF16),
        q_g=row(p['mla_q_norm_g']), kv_g=row(p['mla_kv_norm_g']),
        conv_w=p['lru_conv_w'], conv_b=row(p['lru_conv_b']),
        b_r=row(p['lru_b_r']), b_i=row(p['lru_b_i']), lam=row(p['lru_lam']),
        w_o=p['w_o'].astype(BF16),
        ln1_g=row(p['ln1_g']), ln1_b=row(p['ln1_b']), ln2_g=row(p['ln2_g']), ln2_b=row(p['ln2_b']),
        w_up=p['ffn_w_up'].astype(BF16),
        ffn_cw=jnp.pad(p['ffn_conv_w'], ((0, 0), (0, SUBLANE - FFN_CONV), (0, 0))),
        ffn_cb=row(p['ffn_conv_b']),
        w_down=p['ffn_w_down'].astype(BF16),
    )


def _gmlp_params(p, ln):
    w_s = p['gmlp_w_s'][:, :, :ln, :ln]
    b_s = jnp.repeat(jnp.swapaxes(p['gmlp_b_s'][:, :, :ln], 1, 2), A_HEAD, axis=2)
    return w_s, b_s


def _ffn_state_in(st, nb):
    n, steps, width = st.shape
    if nb == 1:
        return jnp.pad(st[:, :, None, :], ((0, 0), (0, 0), (0, SUBLANE - 1), (0, 0))).reshape(n, -1, width)
    return jnp.swapaxes(st.reshape(n // nb, nb, steps, width), 1, 2).reshape(n // nb, -1, width)


def _ffn_state_out(st, nb):
    n, _, width = st.shape
    st = st.reshape(n, FFN_CONV - 1, SUBLANE, width)
    if nb == 1:
        return st[:, :, 0, :]
    return jnp.swapaxes(st, 1, 2).reshape(n * nb, FFN_CONV - 1, width)


PROMPT_TT_IN = 512
PROMPT_TT_OUT = 256
PROMPT_SUB_OUT = 2
PROMPT_TQ = 512
PROMPT_TK = 512
SAMPLE_NB = 8
SAMPLE_TK = 1024


def kernel(x_prompt, x_sample, cache_kv_latent, cache_k_rope, state_lru_h, state_lru_conv, state_ffn_conv,
           ln1_g, ln1_b, ln2_g, ln2_b, w_in, w_o, gmlp_w_s, gmlp_b_s, mla_q_norm_g, mla_w_uq,
           mla_kv_norm_g, mla_w_uk, mla_w_uv, lru_conv_w, lru_conv_b, lru_w_r, lru_b_r, lru_w_i, lru_b_i,
           lru_lam, ffn_w_up, ffn_conv_w, ffn_conv_b, ffn_w_down):
    p = dict(ln1_g=ln1_g, ln1_b=ln1_b, ln2_g=ln2_g, ln2_b=ln2_b, w_in=w_in, w_o=w_o, gmlp_w_s=gmlp_w_s,
             gmlp_b_s=gmlp_b_s, mla_q_norm_g=mla_q_norm_g, mla_w_uq=mla_w_uq, mla_kv_norm_g=mla_kv_norm_g,
             mla_w_uk=mla_w_uk, mla_w_uv=mla_w_uv, lru_conv_w=lru_conv_w, lru_conv_b=lru_conv_b,
             lru_w_r=lru_w_r, lru_b_r=lru_b_r, lru_w_i=lru_w_i, lru_b_i=lru_b_i, lru_lam=lru_lam,
             ffn_w_up=ffn_w_up, ffn_conv_w=ffn_conv_w, ffn_conv_b=ffn_conv_b, ffn_w_down=ffn_w_down)
    bp, s_len, _ = x_prompt.shape
    bd, t_len, _ = x_sample.shape
    past = cache_kv_latent.shape[2]
    depth = w_in.shape[0]
    ln_p, ln_d = min(s_len, GMLP_CHUNK), min(t_len, GMLP_CHUNK)

    tabs_p = _rope_tables(jnp.arange(s_len), 1, True)
    tabs_d = _rope_tables(past + jnp.arange(t_len), SAMPLE_NB, False)
    zero_rows = jnp.zeros((bp, SUBLANE, D_C), F32)
    zero_ffn = jnp.zeros((bp, FFN_STATE_ROWS, 2 * D_FF), F32)

    xp = x_prompt.reshape(bp * s_len, D_MODEL)
    xd = x_sample.reshape(bd * t_len, D_MODEL)
    cache_kr_t = jnp.swapaxes(cache_k_rope, 2, 3)
    lw = _prep_weights(p)
    lw_p = dict(lw, **dict(zip(('w_s', 'b_s'), _gmlp_params(p, ln_p))))
    lw_d = dict(lw, **dict(zip(('w_s', 'b_s'), _gmlp_params(p, ln_d))))
    outs = {i: [] for i in (2, 3, 4, 7, 8, 9, 10)}
    stacked_p = stacked_d = None
    for l in range(depth):
        a, c, q, lat_p, kr_p, hst, cst, k, v = _mixer_in(
            xp, lw_p, l, tabs_p, zero_rows, zero_rows, stacked_p,
            n_seq=bp, seq_len=s_len, nb=1, tt=PROMPT_TT_IN, ln=ln_p, prompt=True)
        stacked_p = (lat_p, kr_p)
        b = _attn_prompt(q, k, v, n_seq=bp, seq_len=s_len, tq=PROMPT_TQ, tk=PROMPT_TK)
        xp, ffn_st = _mixer_out(xp, a, b, c, lw, l, zero_ffn,
                                n_seq=bp, seq_len=s_len, nb=1, tt=PROMPT_TT_OUT, n_sub=PROMPT_SUB_OUT)
        outs[2].append(hst[:, 0, :])
        outs[3].append(cst[:, SUBLANE - (LRU_CONV - 1):, :])
        outs[4].append(_ffn_state_out(ffn_st, 1))

        h0 = jnp.broadcast_to(state_lru_h[l][:, None, :], (bd, SUBLANE, D_C))
        conv0 = jnp.pad(state_lru_conv[l], ((0, 0), (SUBLANE - (LRU_CONV - 1), 0), (0, 0)))
        ffn_st0 = _ffn_state_in(state_ffn_conv[l], SAMPLE_NB)
        a, c, q, lat_d, kr_d, hst, cst, vg = _mixer_in(
            xd, lw_d, l, tabs_d, h0, conv0, stacked_d,
            n_seq=bd, seq_len=t_len, nb=SAMPLE_NB, tt=t_len, ln=ln_d, prompt=False)
        stacked_d = (lat_d, kr_d)
        b = _attn_sample(q, cache_kv_latent, cache_kr_t, lat_d, kr_d, lw['wkt'], lw['wv_pad'],
                         layer=l, n_seq=bd, t_new=t_len, tk=SAMPLE_TK)
        xd, ffn_st = _mixer_out(xd, a, b, c, lw, l, ffn_st0,
                                n_seq=bd, seq_len=t_len, nb=SAMPLE_NB, tt=t_len, n_sub=1)
        outs[7].append(vg.reshape(bd, t_len, D_A))
        outs[8].append(hst[:, 0, :])
        outs[9].append(cst[:, SUBLANE - (LRU_CONV - 1):, :])
        outs[10].append(_ffn_state_out(ffn_st, SAMPLE_NB))

    st = {i: jnp.stack(o) for i, o in outs.items()}
    return (xp.reshape(bp, s_len, D_MODEL), xd.reshape(bd, t_len, D_MODEL),
            lat_p.reshape(depth, bp, s_len, KV_LORA), kr_p.reshape(depth, bp, s_len, ROPE_DIM),
            st[2], st[3], st[4],
            lat_d.reshape(depth, bd, t_len, KV_LORA), kr_d.reshape(depth, bd, t_len, ROPE_DIM),
            st[7], st[8], st[9], st[10])
```

```python
import functools
import math

import jax
import jax.numpy as jnp
from jax import lax
from jax.experimental import pallas as pl
from jax.experimental.pallas import tpu as pltpu

F32 = jnp.float32
BF16 = jnp.bfloat16

D_MODEL = 1024
DEPTH = 4
CHUNK = 64
GMLP_CHUNK = 128
D_A = 256
N_A_HEADS = 4
A_HEAD = 64
D_B = 512
N_B_HEADS = 8
V_DIM = 64
NOPE_DIM = 64
ROPE_DIM = 32
Q_LORA = 384
KV_LORA = 256
ROPE_THETA = 10000.0
ATTN_SCALE = (NOPE_DIM + ROPE_DIM) ** -0.5
D_C = 256
LRU_CONV = 4
LRU_C = 8.0
D_FF = 2816
FFN_CONV = 3
ALPHA = (2.0 * DEPTH) ** 0.25
LN_EPS = 1e-5
RMS_EPS = 1e-6

LANE = 128
SUBLANE = 8
HEAD_BLK = 128
VMEM_LIMIT = 56 * 1024 * 1024

C_U = 0
C_V = C_U + D_A
C_CQ = C_V + D_A
C_CKV = C_CQ + Q_LORA
C_KR = C_CKV + KV_LORA
C_XC = C_KR + 2 * HEAD_BLK
C_GATE = C_XC + D_C
C_END = C_GATE + D_C
IN_SPLITS = (D_A, D_A, Q_LORA, KV_LORA, ROPE_DIM, D_C, D_C)

FF_CHUNK = 256
ROW_BLK = 64
DOWN_GROUP = 3
NEXT_PROLOGUE_AT = 1
N_FF = D_FF // FF_CHUNK


def _dot(a, b):
    return jnp.dot(a, b, preferred_element_type=F32)


def _dot_nt(a, b):
    return lax.dot_general(a, b, (((1,), (1,)), ((), ())), preferred_element_type=F32)


def _rms_norm(x, g):
    ms = jnp.mean(x * x, axis=-1, keepdims=True)
    return x * lax.rsqrt(ms + RMS_EPS) * g


def _layer_norm(x, g, b):
    mu = jnp.mean(x, axis=-1, keepdims=True)
    xc = x - mu
    var = jnp.mean(xc * xc, axis=-1, keepdims=True)
    return xc * lax.rsqrt(var + LN_EPS) * g + b


def _mixer_in_kernel(nb, tt, ln, prompt, n_aliased, x_ref, win_ref, wqb_ref, *rest):
    if prompt:
        wkp_ref, wuv_ref, rest = rest[0], rest[1], rest[2:]
    else:
        wqs_ref, rest = rest[0], rest[1:]
    (wr_ref, wi_ref, ws_ref, bs_ref, qg_ref, kvg_ref, cw_ref, cb_ref, br_ref, bi_ref, lam_ref,
     tqc_ref, tqs_ref, tkc_ref, tks_ref, h0_ref, conv0_ref) = rest[:17]
    rest = rest[17 + n_aliased:]
    if prompt:
        (a_ref, c_ref, q_ref, lat_ref, kr_ref, hst_ref, cst_ref, k_ref, v_ref,
         xb_ref, cv_ref, hc_ref) = rest
        vg_ref = None
    else:
        (a_ref, c_ref, q_ref, lat_ref, kr_ref, hst_ref, cst_ref, vg_ref,
         xb_ref, cv_ref, hc_ref) = rest
        k_ref = v_ref = None
    rows = nb * tt
    t_idx = pl.program_id(1)

    @pl.when(t_idx == 0)
    def _():
        cv_ref[:, 0:SUBLANE, :] = conv0_ref[...]
        hc_ref[...] = h0_ref[...]

    xb_ref[...] = x_ref[...].astype(BF16)

    def proj(c0, c1):
        return _dot(xb_ref[...], win_ref[:, c0:c1])

    xc_in = proj(C_XC, C_GATE)
    z_gate = proj(C_GATE, C_END)
    z_u = proj(C_U, C_V)
    z_v = proj(C_V, C_CQ)

    xcs = []
    for s in range(nb):
        cv_ref[s, SUBLANE:SUBLANE + tt, :] = xc_in[s * tt:(s + 1) * tt]
        acc = cb_ref[...]
        for j in range(LRU_CONV):
            off = SUBLANE - (LRU_CONV - 1) + j
            acc = acc + cv_ref[s, off:off + tt, :] * cw_ref[j:j + 1, :]
        xcs.append(acc)
        tail = cv_ref[s, tt:tt + SUBLANE, :]
        cst_ref[s] = tail
        cv_ref[s, 0:SUBLANE, :] = tail
    xc = xcs[0] if nb == 1 else jnp.concatenate(xcs, axis=0)
    xcb = xc.astype(BF16)
    z_r = _dot(xcb, wr_ref[...])
    z_i = _dot(xcb, wi_ref[...])
    z_cq = proj(C_CQ, C_CKV)
    z_ckv = proj(C_CKV, C_KR)
    kr2 = proj(C_KR, C_XC)

    gpb = ln // SUBLANE
    sub = lax.broadcasted_iota(jnp.int32, (gpb, SUBLANE, D_C), 1)
    soft_lam = jax.nn.softplus(-lam_ref[...])
    chain = {}

    def lru_block(c):
        r0 = c * ln
        seq, first = divmod(r0, tt)
        r = jax.nn.sigmoid(z_r[r0:r0 + ln] + br_ref[...])
        ig = jax.nn.sigmoid(z_i[r0:r0 + ln] + bi_ref[...])
        log_a = (-LRU_C) * r * soft_lam
        a = jnp.exp(log_a)
        b_in = jnp.sqrt(-jnp.tanh(log_a) * (a * a + 1.0)) * (ig * xc[r0:r0 + ln])
        a3 = a.reshape(gpb, SUBLANE, D_C)
        b3 = b_in.reshape(gpb, SUBLANE, D_C)
        for k in (1, 2, 4):
            keep = sub >= k
            a_sh = jnp.where(keep, pltpu.roll(a3, k, 1), 1.0)
            b_sh = jnp.where(keep, pltpu.roll(b3, k, 1), 0.0)
            b3 = a3 * b_sh + b3
            a3 = a3 * a_sh
        hb = hc_ref[seq] if first == 0 else chain['h']
        h_rows = []
        for g in range(gpb):
            hr = a3[g] * hb + b3[g]
            h_rows.append(hr)
            hb = jnp.broadcast_to(hr[SUBLANE - 1:SUBLANE, :], (SUBLANE, D_C))
        chain['h'] = hb
        if first + ln == tt:
            hc_ref[seq] = hb
            hst_ref[seq] = hb
        c_ref[r0:r0 + ln, :] = (jnp.concatenate(h_rows, axis=0)
                                * jax.nn.gelu(z_gate[r0:r0 + ln])).astype(BF16)

    cqn = _rms_norm(z_cq, qg_ref[...]).astype(BF16)
    tqc = tqc_ref[...]
    tqs = tqs_ref[...]
    ckvn = _rms_norm(z_ckv, kvg_ref[...])
    def put_layer(ref, val):
        if n_aliased:
            ref[...] = val
        else:
            ref[0] = val
            if ref.shape[0] > 1:
                ref[1:] = jnp.zeros((ref.shape[0] - 1,) + val.shape, val.dtype)

    put_layer(lat_ref, ckvn)
    ckvn_b = ckvn.astype(BF16)
    kr_rot = kr2[:, :HEAD_BLK] * tkc_ref[...] + kr2[:, HEAD_BLK:] * tks_ref[...]
    put_layer(kr_ref, kr_rot[:, :ROPE_DIM])

    v = jax.nn.gelu(z_v)
    if vg_ref is not None:
        vg_ref[...] = v
    vb = v.astype(BF16)
    row_i = lax.broadcasted_iota(jnp.int32, (ln, ln), 0)
    col_i = lax.broadcasted_iota(jnp.int32, (ln, ln), 1)
    w_tril = [jnp.where(row_i >= col_i, ws_ref[h], 0.0).astype(BF16) for h in range(N_A_HEADS)]
    head_of_lane = lax.broadcasted_iota(jnp.int32, (ln, D_A), 1) // A_HEAD

    def gmlp_chunk(c):
        r0 = c * ln
        vc = vb[r0:r0 + ln]
        gate = bs_ref[...]
        for h in range(N_A_HEADS):
            gate = gate + _dot(w_tril[h], jnp.where(head_of_lane == h, vc, jnp.zeros_like(vc)))
        a_ref[r0:r0 + ln, :] = (jax.nn.gelu(z_u[r0:r0 + ln]) * gate).astype(BF16)

    def q_pair(j):
        c0, c1 = 2 * j * HEAD_BLK, 2 * (j + 1) * HEAD_BLK
        if prompt:
            qb = _dot_nt(wqb_ref[c0:c1, :], cqn)
            half = ROPE_DIM // 2
            for i in range(2):
                r0 = i * HEAD_BLK
                r1, r2 = qb[r0:r0 + half], qb[r0 + half:r0 + ROPE_DIM]
                rest = qb[r0 + ROPE_DIM:r0 + HEAD_BLK] * (ATTN_SCALE * LOG2E)
                blk = jnp.concatenate([r1 * tqc - r2 * tqs, r1 * tqs + r2 * tqc, rest], axis=0)
                q_ref[c0 + r0:c0 + r0 + HEAD_BLK, :] = blk.astype(BF16)
        else:
            q_ref[:, c0:c1] = (_dot(cqn, wqb_ref[:, c0:c1]) * tqc
                               + _dot(cqn, wqs_ref[:, c0:c1]) * tqs).astype(BF16)

    def k_pair(j):
        c0 = 2 * j * HEAD_BLK
        kp = _dot(ckvn_b, wkp_ref[:, c0:c0 + 2 * HEAD_BLK])
        k_ref[:, c0:c0 + HEAD_BLK] = (kp[:, :HEAD_BLK] + kr_rot).astype(BF16)
        k_ref[:, c0 + HEAD_BLK:c0 + 2 * HEAD_BLK] = (kp[:, HEAD_BLK:] + kr_rot).astype(BF16)

    def v_all():
        v_t = _dot_nt(wuv_ref[...], ckvn_b).astype(BF16)
        for h in range(N_B_HEADS):
            v_ref[h * V_AUG:h * V_AUG + V_DIM, :] = v_t[h * V_DIM:(h + 1) * V_DIM]
            v_ref[h * V_AUG + V_DIM:(h + 1) * V_AUG, :] = jnp.ones((V_ONES, rows), BF16)

    n_blocks = rows // ln
    mla = [functools.partial(q_pair, j) for j in range(N_B_HEADS // 2)]
    if prompt:
        mla += [functools.partial(k_pair, j) for j in range(N_B_HEADS // 2)] + [v_all]
    per_block = -(-len(mla) // n_blocks)
    for c in range(n_blocks):
        for piece in mla[c * per_block:(c + 1) * per_block]:
            piece()
        gmlp_chunk(c)
        lru_block(c)


def _layer_spec(w, layer, single_buffer=False):
    tail = (0,) * (w.ndim - 1)
    mode = dict(pipeline_mode=pl.Buffered(1)) if single_buffer else {}
    return pl.BlockSpec((None,) + w.shape[1:], lambda *_: (layer,) + tail, **mode)


def _mixer_in(x2d, lw, layer, tabs, h0, conv0, stacked, *, n_seq, seq_len, nb, tt, ln, prompt):
    depth = lw['w_in'].shape[0]
    nt = seq_len // tt
    assert nb == 1 or nt == 1
    rows = nb * tt
    n_rows = n_seq * seq_len
    grid = (n_seq // nb, nt)
    row_map = lambda b, t: (b * nt + t, 0)
    stacked = tuple(stacked or ())

    def layer_out_spec(width):
        if stacked:
            return pl.BlockSpec((None, rows, width), lambda b, t: (layer, b * nt + t, 0))
        assert layer == 0
        return pl.BlockSpec((depth, rows, width), lambda b, t: (0, b * nt + t, 0))

    tab_map = lambda b, t: (t, 0)
    seq_map = lambda b, t: (b, 0, 0)

    mode_weights = [lw['wq_blk_t'], lw['wk_pad'], lw['w_uv_t']] if prompt else [lw['wq_blk'], lw['wq_swp']]
    weights = [lw['w_in']] + mode_weights + [
        lw['w_r'], lw['w_i'], lw['w_s'], lw['b_s'], lw['q_g'], lw['kv_g'], lw['conv_w'], lw['conv_b'],
        lw['b_r'], lw['b_i'], lw['lam']]
    col_map = lambda b, t: (0, b * nt + t)
    if prompt:
        q_tab_spec = pl.BlockSpec((ROPE_DIM // 2, rows), lambda b, t: (0, t))
        q_shape = jax.ShapeDtypeStruct((N_B_HEADS * HEAD_BLK, n_rows), BF16)
        q_spec = pl.BlockSpec((N_B_HEADS * HEAD_BLK, rows), col_map)
    else:
        q_tab_spec = pl.BlockSpec((rows, 2 * HEAD_BLK), tab_map)
        q_shape = jax.ShapeDtypeStruct((n_rows, N_B_HEADS * HEAD_BLK), BF16)
        q_spec = pl.BlockSpec((rows, N_B_HEADS * HEAD_BLK), row_map)
    in_specs = ([pl.BlockSpec((rows, D_MODEL), row_map)]
                + [_layer_spec(w, layer) for w in weights]
                + [q_tab_spec, q_tab_spec,
                   pl.BlockSpec((rows, HEAD_BLK), tab_map), pl.BlockSpec((rows, HEAD_BLK), tab_map),
                   pl.BlockSpec((nb, SUBLANE, D_C), seq_map), pl.BlockSpec((nb, SUBLANE, D_C), seq_map)])
    out_shape = [jax.ShapeDtypeStruct((n_rows, D_A), BF16),
                 jax.ShapeDtypeStruct((n_rows, D_C), BF16),
                 q_shape,
                 jax.ShapeDtypeStruct((depth, n_rows, KV_LORA), F32),
                 jax.ShapeDtypeStruct((depth, n_rows, ROPE_DIM), F32),
                 jax.ShapeDtypeStruct((n_seq, SUBLANE, D_C), F32),
                 jax.ShapeDtypeStruct((n_seq, SUBLANE, D_C), F32)]
    out_specs = [pl.BlockSpec((rows, D_A), row_map), pl.BlockSpec((rows, D_C), row_map),
                 q_spec, layer_out_spec(KV_LORA), layer_out_spec(ROPE_DIM),
                 pl.BlockSpec((nb, SUBLANE, D_C), seq_map), pl.BlockSpec((nb, SUBLANE, D_C), seq_map)]
    if prompt:
        out_shape += [jax.ShapeDtypeStruct((n_rows, N_B_HEADS * HEAD_BLK), BF16),
                      jax.ShapeDtypeStruct((N_B_HEADS * V_AUG, n_rows), BF16)]
        out_specs += [pl.BlockSpec((rows, N_B_HEADS * HEAD_BLK), row_map),
                      pl.BlockSpec((N_B_HEADS * V_AUG, rows), col_map)]
    else:
        out_shape += [jax.ShapeDtypeStruct((n_rows, D_A), F32)]
        out_specs += [pl.BlockSpec((rows, D_A), row_map)]
    scratch = [pltpu.VMEM((rows, D_MODEL), BF16),
               pltpu.VMEM((nb, tt + SUBLANE, D_C), F32),
               pltpu.VMEM((nb, SUBLANE, D_C), F32)]
    n_in = len(in_specs)
    in_specs += [pl.BlockSpec(memory_space=pl.ANY)] * len(stacked)
    aliases = {n_in + i: 3 + i for i in range(len(stacked))}
    return pl.pallas_call(
        functools.partial(_mixer_in_kernel, nb, tt, ln, prompt, len(stacked)),
        grid=grid, in_specs=in_specs, out_specs=out_specs, out_shape=out_shape,
        scratch_shapes=scratch, input_output_aliases=aliases,
        compiler_params=pltpu.CompilerParams(dimension_semantics=("arbitrary", "arbitrary"),
                                             vmem_limit_bytes=VMEM_LIMIT),
        name="mixer_in_prompt" if prompt else "mixer_in_sample",
    )(x2d, *weights, *tabs, h0, conv0, *stacked)


NEG_BIG = -1e30
LOG2E = math.log2(math.e)
V_ONES = 16
V_AUG = V_DIM + V_ONES
ATTN_HEAD_GROUP = 2
ATTN_DIAG_PARTS = 4


def _last_kv_tile(qi, tq, tk):
    return ((qi + 1) * tq - 1) // tk


def _kv_steps(nq, tq, tk):
    pairs = [(i, j) for i in range(nq) for j in range(_last_kv_tile(i, tq, tk) + 1)]
    return (jnp.asarray([p[0] for p in pairs], jnp.int32), jnp.asarray([p[1] for p in pairs], jnp.int32))


def _attn_prompt_kernel(tq, tk, qi_tab, ki_tab, q_ref, k_ref, v_ref, o_ref, m_ref, acc_ref):
    step_idx = pl.program_id(1)
    qi = qi_tab[step_idx]
    ki = ki_tab[step_idx]

    @pl.when(ki == 0)
    def _():
        m_ref[...] = jnp.full(m_ref.shape, NEG_BIG, F32)
        acc_ref[...] = jnp.zeros(acc_ref.shape, F32)

    pw = tq // ATTN_DIAG_PARTS

    def step(diag):
        if diag:
            local = (lax.broadcasted_iota(jnp.int32, (pw, pw), 0) // CHUNK
                     <= lax.broadcasted_iota(jnp.int32, (pw, pw), 1) // CHUNK)
            parts = tuple((c * pw, (c + 1) * pw, (c + 1) * pw) for c in range(ATTN_DIAG_PARTS))
        else:
            parts = ((0, tq, tk),)

        def scores(h):
            blk = slice(h * HEAD_BLK, (h + 1) * HEAD_BLK)
            return [_dot(k_ref[0:nk, blk], q_ref[blk, c0:c1]) for (c0, c1, nk) in parts]

        def softmax(h, s_parts):
            out = []
            for (c0, c1, nk), s in zip(parts, s_parts):
                if diag:
                    low = jnp.where(local, s[nk - pw:], NEG_BIG)
                    s = low if nk == pw else jnp.concatenate([s[:nk - pw], low], axis=0)
                m_prev = m_ref[h, :, c0:c1]
                m_new = jnp.maximum(m_prev, jnp.max(s, axis=0, keepdims=True))
                m_ref[h, :, c0:c1] = m_new
                out.append((jnp.exp2(m_prev - m_new), jnp.exp2(s - m_new).astype(BF16)))
            return out

        group = ATTN_HEAD_GROUP
        s_next = [scores(h) for h in range(group)]
        for h0 in range(0, N_B_HEADS, group):
            s_cur = s_next
            if h0 + group < N_B_HEADS:
                s_next = [scores(h) for h in range(h0 + group, h0 + 2 * group)]
            probs = [softmax(h0 + i, s_cur[i]) for i in range(group)]
            for i, prob_parts in enumerate(probs):
                rows = slice((h0 + i) * V_AUG, (h0 + i + 1) * V_AUG)
                for (c0, c1, nk), (alpha, p) in zip(parts, prob_parts):
                    acc_ref[rows, c0:c1] = alpha * acc_ref[rows, c0:c1] + _dot(v_ref[rows, 0:nk], p)

    def finalize():
        for j in range(N_B_HEADS // 2):
            halves = []
            for h in (2 * j, 2 * j + 1):
                denom = acc_ref[h * V_AUG + V_DIM:h * V_AUG + V_DIM + 1, :]
                halves.append(acc_ref[h * V_AUG:h * V_AUG + V_DIM, :] / denom)
            o_ref[:, j * LANE:(j + 1) * LANE] = jnp.concatenate(halves, axis=0).T.astype(BF16)

    @pl.when(ki == qi)
    def _():
        step(True)
        finalize()

    @pl.when(ki != qi)
    def _():
        step(False)


def _attn_prompt(q_t, k, v_t, *, n_seq, seq_len, tq, tk):
    assert tq == tk and (tq // ATTN_DIAG_PARTS) % max(CHUNK, LANE) == 0
    nq, nk = seq_len // tq, seq_len // tk
    qi_tab, ki_tab = _kv_steps(nq, tq, tk)
    grid_spec = pltpu.PrefetchScalarGridSpec(
        num_scalar_prefetch=2,
        grid=(n_seq, qi_tab.shape[0]),
        in_specs=[pl.BlockSpec((N_B_HEADS * HEAD_BLK, tq), lambda b, s, qt, kt: (0, b * nq + qt[s])),
                  pl.BlockSpec((tk, N_B_HEADS * HEAD_BLK), lambda b, s, qt, kt: (b * nk + kt[s], 0)),
                  pl.BlockSpec((N_B_HEADS * V_AUG, tk), lambda b, s, qt, kt: (0, b * nk + kt[s]))],
        out_specs=pl.BlockSpec((tq, D_B), lambda b, s, qt, kt: (b * nq + qt[s], 0)),
        scratch_shapes=[pltpu.VMEM((N_B_HEADS, 1, tq), F32),
                        pltpu.VMEM((N_B_HEADS * V_AUG, tq), F32)])
    return pl.pallas_call(
        functools.partial(_attn_prompt_kernel, tq, tk),
        grid_spec=grid_spec,
        out_shape=jax.ShapeDtypeStruct((n_seq * seq_len, D_B), BF16),
        compiler_params=pltpu.CompilerParams(dimension_semantics=("arbitrary", "arbitrary"),
                                             vmem_limit_bytes=VMEM_LIMIT),
        name="attn_prompt",
    )(qi_tab, ki_tab, q_t, k, v_t)


def _attn_sample_kernel(t_new, tk, q_ref, clat_ref, ckr_ref, nlat_ref, nkr_ref, wkt_ref, wv_ref, o_ref):
    past = clat_ref.shape[0]
    ql, qr = [], []
    for h in range(N_B_HEADS):
        q_h = q_ref[:, h * HEAD_BLK:(h + 1) * HEAD_BLK]
        ql.append(_dot(q_h, wkt_ref[h]).astype(BF16))
        qr.append(q_h[:, :ROPE_DIM])
    ql = jnp.concatenate(ql, axis=0)
    qr = jnp.concatenate(qr, axis=0)

    def scores(i):
        if i == 0:
            c_b = nlat_ref[...].astype(BF16)
            return _dot_nt(ql, c_b) + _dot_nt(qr, nkr_ref[...].astype(BF16)), c_b
        r0 = (i - 1) * tk
        c_b = clat_ref[r0:r0 + tk, :].astype(BF16)
        return _dot_nt(ql, c_b) + _dot(qr, ckr_ref[:, r0:r0 + tk].astype(BF16)), c_b

    rows = N_B_HEADS * t_new
    m = jnp.full((rows, 1), NEG_BIG, F32)
    l = jnp.zeros((rows, 1), F32)
    acc = jnp.zeros((rows, KV_LORA), F32)
    n_tiles = 1 + past // tk
    nxt = scores(0)
    for i in range(n_tiles):
        s, c_b = nxt
        if i + 1 < n_tiles:
            nxt = scores(i + 1)
        m_new = jnp.maximum(m, jnp.max(s, axis=1, keepdims=True))
        alpha = jnp.exp2(m - m_new)
        p = jnp.exp2(s - m_new)
        l = alpha * l + jnp.sum(p, axis=1, keepdims=True)
        acc = alpha * acc + _dot(p.astype(BF16), c_b)
        m = m_new
    o_lat = (acc / l).astype(BF16)
    out = _dot(o_lat[0:t_new], wv_ref[0])
    for h in range(1, N_B_HEADS):
        out = out + _dot(o_lat[h * t_new:(h + 1) * t_new], wv_ref[h])
    o_ref[...] = out.astype(BF16)


def _attn_sample(q, cache_lat, cache_kr, new_lat, new_kr, wkt, wv_pad, *, layer, n_seq, t_new, tk):
    past = cache_lat.shape[2]
    assert past % tk == 0
    row_map = lambda b: (b, 0)
    return pl.pallas_call(
        functools.partial(_attn_sample_kernel, t_new, tk),
        grid=(n_seq,),
        in_specs=[pl.BlockSpec((t_new, N_B_HEADS * HEAD_BLK), row_map),
                  pl.BlockSpec((None, None, past, KV_LORA), lambda b: (layer, b, 0, 0)),
                  pl.BlockSpec((None, None, ROPE_DIM, past), lambda b: (layer, b, 0, 0)),
                  pl.BlockSpec((None, t_new, KV_LORA), lambda b: (layer, b, 0)),
                  pl.BlockSpec((None, t_new, ROPE_DIM), lambda b: (layer, b, 0)),
                  _layer_spec(wkt, layer), _layer_spec(wv_pad, layer)],
        out_specs=pl.BlockSpec((t_new, D_B), row_map),
        out_shape=jax.ShapeDtypeStruct((n_seq * t_new, D_B), BF16),
        compiler_params=pltpu.CompilerParams(dimension_semantics=("arbitrary",),
                                             vmem_limit_bytes=VMEM_LIMIT),
        name="attn_sample",
    )(q, cache_lat, cache_kr, new_lat, new_kr, wkt, wv_pad)


def _mixer_out_kernel(nb, tt, n_sub,
                      x_ref, a_ref, b_ref, c_ref, wo_ref, g1_ref, b1_ref,
                      wu_ref, cw_ref, cb_ref, wd_ref, g2_ref, b2_ref, s_ref,
                      y_ref, o_ref,
                      x1_ref, xp_ref, x1b_ref, acc_ref, bufg0_ref, bufg1_ref, bufv0_ref, bufv1_ref,
                      car_ref, act0_ref, act1_ref):
    bufg_ref = (bufg0_ref, bufg1_ref)
    bufv_ref = (bufv0_ref, bufv1_ref)
    act_ref = (act0_ref, act1_ref)
    t_idx = pl.program_id(1)
    rows = nb * tt
    n_tiles = rows // SUBLANE
    halo = (FFN_CONV - 1) * SUBLANE

    @pl.when(t_idx == 0)
    def _():
        car_ref[...] = s_ref[...]

    n_items = n_sub * N_FF

    def prologue(t):
        rs = slice(t * rows, (t + 1) * rows)
        mix = (_dot(a_ref[rs, :], wo_ref[0:D_A, :]) + _dot(b_ref[rs, :], wo_ref[D_A:D_A + D_B, :])
               + _dot(c_ref[rs, :], wo_ref[D_A + D_B:, :]))
        x1 = _layer_norm(ALPHA * x_ref[rs, :] + mix, g1_ref[...], b1_ref[...])
        x1_ref[rs, :] = x1
        for c in range(D_MODEL // LANE):
            for k in range(n_tiles):
                xp_ref[c, pl.ds(_interleave_start(k, n_tiles), SUBLANE, stride=SUBLANE), :] = (
                    x1[k * SUBLANE:(k + 1) * SUBLANE, c * LANE:(c + 1) * LANE])
            x1b_ref[t, :, c * LANE:(c + 1) * LANE] = xp_ref[c].astype(BF16)

    def epilogue(t):
        rs = slice(t * rows, (t + 1) * rows)
        ffn = jnp.concatenate(
            [jnp.concatenate([acc_ref[t, c, pl.ds(_interleave_start(k, n_tiles), SUBLANE, stride=SUBLANE), :]
                              for k in range(n_tiles)], axis=0)
             for c in range(D_MODEL // LANE)], axis=1)
        y_ref[rs, :] = _layer_norm(ALPHA * x1_ref[rs, :] + ffn, g2_ref[...], b2_ref[...])

    def cols(j, value_half):
        c0 = (D_FF if value_half else 0) + j * FF_CHUNK
        return slice(c0, c0 + FF_CHUNK)

    first_sublane = lax.broadcasted_iota(jnp.int32, (SUBLANE, FF_CHUNK), 0) == 0

    def stage(up, buf, cs):
        buf[halo:halo + rows, :] = up
        for i in range(FFN_CONV - 1):
            carry_rows = slice(i * SUBLANE, (i + 1) * SUBLANE)
            last = up[rows - halo + i * SUBLANE:rows - halo + (i + 1) * SUBLANE]
            if nb == 1:
                last = pltpu.roll(last, 1, 0)
                prev = jnp.where(first_sublane, car_ref[carry_rows, cs], last)
            else:
                prev = car_ref[carry_rows, cs]
            buf[carry_rows, :] = prev
            car_ref[carry_rows, cs] = last
            o_ref[carry_rows, cs] = last

    def conv_rows(buf, cs, r0, nr):
        acc = cb_ref[:, cs]
        for i in range(FFN_CONV):
            acc = acc + buf[r0 + i * SUBLANE:r0 + i * SUBLANE + nr, :] * cw_ref[i:i + 1, cs]
        return acc

    def up_proj(i):
        t, j = divmod(i, N_FF)
        xb = x1b_ref[t]
        stage(_dot(xb, wu_ref[:, cols(j, False)]), bufg_ref[i % 2], cols(j, False))
        stage(_dot(xb, wu_ref[:, cols(j, True)]), bufv_ref[i % 2], cols(j, True))

    groups_per_tile = -(-N_FF // DOWN_GROUP)

    def act_slot(t, grp):
        return act_ref[(t * groups_per_tile + grp) % 2]

    def elementwise(i):
        t, j = divmod(i, N_FF)
        grp, pos = divmod(j, DOWN_GROUP)
        for r0 in range(0, rows, ROW_BLK):
            nr = min(ROW_BLK, rows - r0)
            gate = conv_rows(bufg_ref[i % 2], cols(j, False), r0, nr)
            val = conv_rows(bufv_ref[i % 2], cols(j, True), r0, nr)
            act_slot(t, grp)[r0:r0 + nr, pos * FF_CHUNK:(pos + 1) * FF_CHUNK] = (
                jax.nn.gelu(gate) * val).astype(BF16)

    def down_proj(t, grp):
        n_chunks = min(DOWN_GROUP, N_FF - grp * DOWN_GROUP)
        k0, width = grp * DOWN_GROUP * FF_CHUNK, n_chunks * FF_CHUNK
        part = _dot(act_slot(t, grp)[:, 0:width], wd_ref[k0:k0 + width, :])
        for c in range(D_MODEL // LANE):
            if grp == 0:
                acc_ref[t, c] = part[:, c * LANE:(c + 1) * LANE]
            else:
                acc_ref[t, c] += part[:, c * LANE:(c + 1) * LANE]

    prologue(0)
    up_proj(0)
    pending = None
    for i in range(n_items + 1):
        if i + 1 < n_items:
            up_proj(i + 1)
        if i % N_FF == NEXT_PROLOGUE_AT and i // N_FF + 1 < n_sub:
            prologue(i // N_FF + 1)
        if pending is not None:
            down_proj(*pending)
            if pending[1] == groups_per_tile - 1:
                epilogue(pending[0])
            pending = None
        if i < n_items:
            elementwise(i)
            t, j = divmod(i, N_FF)
            if (j + 1) % DOWN_GROUP == 0 or j == N_FF - 1:
                pending = (t, j // DOWN_GROUP)


def _interleave_start(k, n_tiles):
    group, part = divmod(k, n_tiles // SUBLANE)
    return part * SUBLANE * SUBLANE + group


FFN_STATE_ROWS = (FFN_CONV - 1) * SUBLANE


def _mixer_out(x2d, a, b, c, lw, layer, state, *, n_seq, seq_len, nb, tt, n_sub):
    nt = seq_len // (tt * n_sub)
    assert nb == 1 or (nt == 1 and n_sub == 1)
    rows = nb * tt
    blk_rows = rows * n_sub
    assert nb in (1, SUBLANE) and rows % (SUBLANE * SUBLANE) == 0
    n_rows = n_seq * seq_len
    row_map = lambda bi, t: (bi * nt + t, 0)
    weights_a = [lw['w_o'], lw['ln1_g'], lw['ln1_b'], lw['w_up'], lw['ffn_cw'], lw['ffn_cb'],
                 lw['w_down'], lw['ln2_g'], lw['ln2_b']]
    st_spec = pl.BlockSpec((None, FFN_STATE_ROWS, 2 * D_FF), lambda bi, t: (bi, 0, 0))
    in_specs = ([pl.BlockSpec((blk_rows, D_MODEL), row_map), pl.BlockSpec((blk_rows, D_A), row_map),
                 pl.BlockSpec((blk_rows, D_B), row_map), pl.BlockSpec((blk_rows, D_C), row_map)]
                + [_layer_spec(w, layer, single_buffer=True) for w in weights_a]
                + [st_spec])
    st_shape = jax.ShapeDtypeStruct((n_seq // nb, FFN_STATE_ROWS, 2 * D_FF), F32)
    buf_rows = FFN_STATE_ROWS + rows
    return pl.pallas_call(
        functools.partial(_mixer_out_kernel, nb, tt, n_sub),
        grid=(n_seq // nb, nt),
        in_specs=in_specs,
        out_specs=[pl.BlockSpec((blk_rows, D_MODEL), row_map), st_spec],
        out_shape=[jax.ShapeDtypeStruct((n_rows, D_MODEL), F32), st_shape],
        scratch_shapes=[pltpu.VMEM((blk_rows, D_MODEL), F32),
                        pltpu.VMEM((D_MODEL // LANE, rows, LANE), F32),
                        pltpu.VMEM((n_sub, rows, D_MODEL), BF16),
                        pltpu.VMEM((n_sub, D_MODEL // LANE, rows, LANE), F32),
                        pltpu.VMEM((buf_rows, FF_CHUNK), F32),
                        pltpu.VMEM((buf_rows, FF_CHUNK), F32),
                        pltpu.VMEM((buf_rows, FF_CHUNK), F32),
                        pltpu.VMEM((buf_rows, FF_CHUNK), F32),
                        pltpu.VMEM((FFN_STATE_ROWS, 2 * D_FF), F32),
                        pltpu.VMEM((rows, DOWN_GROUP * FF_CHUNK), BF16),
                        pltpu.VMEM((rows, DOWN_GROUP * FF_CHUNK), BF16)],
        compiler_params=pltpu.CompilerParams(dimension_semantics=("arbitrary", "arbitrary"),
                                             vmem_limit_bytes=VMEM_LIMIT),
        name="mixer_out",
    )(x2d, a, b, c, *weights_a, state)


def _rope_tables(pos, reps, feature_major_q):
    half = ROPE_DIM // 2
    inv = 1.0 / (ROPE_THETA ** (jnp.arange(0, ROPE_DIM, 2, dtype=F32) / ROPE_DIM))
    ang = pos.astype(F32)[:, None] * inv[None, :]
    cos, sin = jnp.cos(ang), jnp.sin(ang)
    n = pos.shape[0]
    zeros = jnp.zeros((n, HEAD_BLK - ROPE_DIM), F32)
    tkc = jnp.concatenate([cos, cos, zeros], axis=1)
    tks = jnp.concatenate([sin, sin, zeros], axis=1)
    tqc = jnp.concatenate([cos, cos, jnp.ones((n, NOPE_DIM), F32),
                           jnp.zeros((n, HEAD_BLK - ROPE_DIM - NOPE_DIM), F32)], axis=1)
    tqc = jnp.concatenate([tqc, tqc], axis=1) * (ATTN_SCALE * LOG2E)
    tqs = jnp.concatenate([tks, tks], axis=1) * (ATTN_SCALE * LOG2E)
    if feature_major_q:
        return ((cos * (ATTN_SCALE * LOG2E)).T, (sin * (ATTN_SCALE * LOG2E)).T, tkc, tks)
    return tuple(jnp.tile(t, (reps, 1)) for t in (tqc, tqs, tkc, tks))


def _block_diag(w):
    depth, n, d, e = w.shape
    return jnp.einsum('lnde,nm->lndme', w, jnp.eye(n, dtype=w.dtype)).reshape(depth, n * d, n * e)


def _prep_weights(p):
    half = ROPE_DIM // 2
    depth = p['w_in'].shape[0]
    bounds = [sum(IN_SPLITS[:i + 1]) for i in range(len(IN_SPLITS) - 1)]
    u, v, cq, ckv, kr, xc, gate = jnp.split(p['w_in'], bounds, axis=2)
    kr1, kr2 = kr[..., :half], kr[..., half:]
    z = jnp.zeros((depth, D_MODEL, HEAD_BLK - ROPE_DIM), F32)
    w_in = jnp.concatenate([u, v, cq, ckv, kr1, kr2, z, -kr2, kr1, z, xc, gate], axis=2)

    wq = p['mla_w_uq'].reshape(depth, Q_LORA, N_B_HEADS, NOPE_DIM + ROPE_DIM)
    nope, r1, r2 = wq[..., :NOPE_DIM], wq[..., NOPE_DIM:NOPE_DIM + half], wq[..., NOPE_DIM + half:]
    zq = lambda k: jnp.zeros((depth, Q_LORA, N_B_HEADS, k), F32)
    wq_blk = jnp.concatenate([r1, r2, nope, zq(HEAD_BLK - ROPE_DIM - NOPE_DIM)], axis=-1)
    wq_blk = wq_blk.reshape(depth, Q_LORA, -1).astype(BF16)
    wq_swp = jnp.concatenate([-r2, r1, zq(HEAD_BLK - ROPE_DIM)], axis=-1).reshape(depth, Q_LORA, -1).astype(BF16)

    w_uk, w_uv = p['mla_w_uk'], p['mla_w_uv']
    zk = jnp.zeros((depth, KV_LORA, N_B_HEADS, ROPE_DIM), F32)
    wk_pad = jnp.concatenate([zk, w_uk, zk], axis=-1).reshape(depth, KV_LORA, -1)
    zt = jnp.zeros((depth, N_B_HEADS, ROPE_DIM, KV_LORA), F32)
    wkt = jnp.concatenate([zt, jnp.transpose(w_uk, (0, 2, 3, 1)), zt], axis=2)
    wv_pad = jnp.einsum('lchd,hg->lhcgd', w_uv, jnp.eye(N_B_HEADS, dtype=F32)).reshape(
        depth, N_B_HEADS, KV_LORA, D_B)
    w_uv2 = w_uv.reshape(depth, KV_LORA, -1).astype(BF16)

    row = lambda a: a.reshape(depth, 1, -1)
    return dict(
        w_in=w_in.astype(BF16),
        wq_blk=wq_blk, wq_swp=wq_swp,
        wq_blk_t=jnp.swapaxes(wq_blk, 1, 2),
        wk_pad=wk_pad.astype(BF16), w_uv_t=jnp.swapaxes(w_uv2, 1, 2),
        wkt=wkt.astype(BF16), wv_pad=wv_pad.astype(BF16),
        w_r=_block_diag(p['lru_w_r']).astype(BF16), w_i=_block_diag(p['lru_w_i']).astype(BF16),
        q_g=row(p['mla_q_norm_g']), kv_g=row(p['mla_kv_norm_g']),
        conv_w=p['lru_conv_w'], conv_b=row(p['lru_conv_b']),
        b_r=row(p['lru_b_r']), b_i=row(p['lru_b_i']), lam=row(p['lru_lam']),
        w_o=p['w_o'].astype(BF16),
        ln1_g=row(p['ln1_g']), ln1_b=row(p['ln1_b']), ln2_g=row(p['ln2_g']), ln2_b=row(p['ln2_b']),
        w_up=p['ffn_w_up'].astype(BF16),
        ffn_cw=jnp.pad(p['ffn_conv_w'], ((0, 0), (0, SUBLANE - FFN_CONV), (0, 0))),
        ffn_cb=row(p['ffn_conv_b']),
        w_down=p['ffn_w_down'].astype(BF16),
    )


def _gmlp_params(p, ln):
    w_s = p['gmlp_w_s'][:, :, :ln, :ln]
    b_s = jnp.repeat(jnp.swapaxes(p['gmlp_b_s'][:, :, :ln], 1, 2), A_HEAD, axis=2)
    return w_s, b_s


def _ffn_state_in(st, nb):
    n, steps, width = st.shape
    if nb == 1:
        return jnp.pad(st[:, :, None, :], ((0, 0), (0, 0), (0, SUBLANE - 1), (0, 0))).reshape(n, -1, width)
    return jnp.swapaxes(st.reshape(n // nb, nb, steps, width), 1, 2).reshape(n // nb, -1, width)


def _ffn_state_out(st, nb):
    n, _, width = st.shape
    st = st.reshape(n, FFN_CONV - 1, SUBLANE, width)
    if nb == 1:
        return st[:, :, 0, :]
    return jnp.swapaxes(st, 1, 2).reshape(n * nb, FFN_CONV - 1, width)


PROMPT_TT_IN = 512
PROMPT_TT_OUT = 256
PROMPT_SUB_OUT = 2
PROMPT_TQ = 1024
PROMPT_TK = 1024
SAMPLE_NB = 8
SAMPLE_TK = 1024


def kernel(x_prompt, x_sample, cache_kv_latent, cache_k_rope, state_lru_h, state_lru_conv, state_ffn_conv,
           ln1_g, ln1_b, ln2_g, ln2_b, w_in, w_o, gmlp_w_s, gmlp_b_s, mla_q_norm_g, mla_w_uq,
           mla_kv_norm_g, mla_w_uk, mla_w_uv, lru_conv_w, lru_conv_b, lru_w_r, lru_b_r, lru_w_i, lru_b_i,
           lru_lam, ffn_w_up, ffn_conv_w, ffn_conv_b, ffn_w_down):
    p = dict(ln1_g=ln1_g, ln1_b=ln1_b, ln2_g=ln2_g, ln2_b=ln2_b, w_in=w_in, w_o=w_o, gmlp_w_s=gmlp_w_s,
             gmlp_b_s=gmlp_b_s, mla_q_norm_g=mla_q_norm_g, mla_w_uq=mla_w_uq, mla_kv_norm_g=mla_kv_norm_g,
             mla_w_uk=mla_w_uk, mla_w_uv=mla_w_uv, lru_conv_w=lru_conv_w, lru_conv_b=lru_conv_b,
             lru_w_r=lru_w_r, lru_b_r=lru_b_r, lru_w_i=lru_w_i, lru_b_i=lru_b_i, lru_lam=lru_lam,
             ffn_w_up=ffn_w_up, ffn_conv_w=ffn_conv_w, ffn_conv_b=ffn_conv_b, ffn_w_down=ffn_w_down)
    bp, s_len, _ = x_prompt.shape
    bd, t_len, _ = x_sample.shape
    past = cache_kv_latent.shape[2]
    depth = w_in.shape[0]
    ln_p, ln_d = min(s_len, GMLP_CHUNK), min(t_len, GMLP_CHUNK)

    tabs_p = _rope_tables(jnp.arange(s_len), 1, True)
    tabs_d = _rope_tables(past + jnp.arange(t_len), SAMPLE_NB, False)
    zero_rows = jnp.zeros((bp, SUBLANE, D_C), F32)
    zero_ffn = jnp.zeros((bp, FFN_STATE_ROWS, 2 * D_FF), F32)

    xp = x_prompt.reshape(bp * s_len, D_MODEL)
    xd = x_sample.reshape(bd * t_len, D_MODEL)
    cache_kr_t = jnp.swapaxes(cache_k_rope, 2, 3)
    lw = _prep_weights(p)
    lw_p = dict(lw, **dict(zip(('w_s', 'b_s'), _gmlp_params(p, ln_p))))
    lw_d = dict(lw, **dict(zip(('w_s', 'b_s'), _gmlp_params(p, ln_d))))
    outs = {i: [] for i in (2, 3, 4, 7, 8, 9, 10)}
    stacked_p = stacked_d = None
    for l in range(depth):
        a, c, q, lat_p, kr_p, hst, cst, k, v = _mixer_in(
            xp, lw_p, l, tabs_p, zero_rows, zero_rows, stacked_p,
            n_seq=bp, seq_len=s_len, nb=1, tt=PROMPT_TT_IN, ln=ln_p, prompt=True)
        stacked_p = (lat_p, kr_p)
        b = _attn_prompt(q, k, v, n_seq=bp, seq_len=s_len, tq=PROMPT_TQ, tk=PROMPT_TK)
        xp, ffn_st = _mixer_out(xp, a, b, c, lw, l, zero_ffn,
                                n_seq=bp, seq_len=s_len, nb=1, tt=PROMPT_TT_OUT, n_sub=PROMPT_SUB_OUT)
        outs[2].append(hst[:, 0, :])
        outs[3].append(cst[:, SUBLANE - (LRU_CONV - 1):, :])
        outs[4].append(_ffn_state_out(ffn_st, 1))

        h0 = jnp.broadcast_to(state_lru_h[l][:, None, :], (bd, SUBLANE, D_C))
        conv0 = jnp.pad(state_lru_conv[l], ((0, 0), (SUBLANE - (LRU_CONV - 1), 0), (0, 0)))
        ffn_st0 = _ffn_state_in(state_ffn_conv[l], SAMPLE_NB)
        a, c, q, lat_d, kr_d, hst, cst, vg = _mixer_in(
            xd, lw_d, l, tabs_d, h0, conv0, stacked_d,
            n_seq=bd, seq_len=t_len, nb=SAMPLE_NB, tt=t_len, ln=ln_d, prompt=False)
        stacked_d = (lat_d, kr_d)
        b = _attn_sample(q, cache_kv_latent, cache_kr_t, lat_d, kr_d, lw['wkt'], lw['wv_pad'],
                         layer=l, n_seq=bd, t_new=t_len, tk=SAMPLE_TK)
        xd, ffn_st = _mixer_out(xd, a, b, c, lw, l, ffn_st0,
                                n_seq=bd, seq_len=t_len, nb=SAMPLE_NB, tt=t_len, n_sub=1)
        outs[7].append(vg.reshape(bd, t_len, D_A))
        outs[8].append(hst[:, 0, :])
        outs[9].append(cst[:, SUBLANE - (LRU_CONV - 1):, :])
        outs[10].append(_ffn_state_out(ffn_st, SAMPLE_NB))

    st = {i: jnp.stack(o) for i, o in outs.items()}
    return (xp.reshape(bp, s_len, D_MODEL), xd.reshape(bd, t_len, D_MODEL),
            lat_p.reshape(depth, bp, s_len, KV_LORA), kr_p.reshape(depth, bp, s_len, ROPE_DIM),
            st[2], st[3], st[4],
            lat_d.reshape(depth, bd, t_len, KV_LORA), kr_d.reshape(depth, bd, t_len, ROPE_DIM),
            st[7], st[8], st[9], st[10])
```

```python
import functools
import math

import jax
import jax.numpy as jnp
from jax import lax
from jax.experimental import pallas as pl
from jax.experimental.pallas import tpu as pltpu

F32 = jnp.float32
BF16 = jnp.bfloat16

D_MODEL = 1024
DEPTH = 4
CHUNK = 64
GMLP_CHUNK = 128
D_A = 256
N_A_HEADS = 4
A_HEAD = 64
D_B = 512
N_B_HEADS = 8
V_DIM = 64
NOPE_DIM = 64
ROPE_DIM = 32
Q_LORA = 384
KV_LORA = 256
ROPE_THETA = 10000.0
ATTN_SCALE = (NOPE_DIM + ROPE_DIM) ** -0.5
D_C = 256
LRU_CONV = 4
LRU_C = 8.0
D_FF = 2816
FFN_CONV = 3
ALPHA = (2.0 * DEPTH) ** 0.25
LN_EPS = 1e-5
RMS_EPS = 1e-6

LANE = 128
SUBLANE = 8
HEAD_BLK = 128
VMEM_LIMIT = 56 * 1024 * 1024

C_U = 0
C_V = C_U + D_A
C_CQ = C_V + D_A
C_CKV = C_CQ + Q_LORA
C_KR = C_CKV + KV_LORA
C_XC = C_KR + 2 * HEAD_BLK
C_GATE = C_XC + D_C
C_END = C_GATE + D_C
IN_SPLITS = (D_A, D_A, Q_LORA, KV_LORA, ROPE_DIM, D_C, D_C)

FF_CHUNK = 256
ROW_BLK = 64
DOWN_GROUP = 3
NEXT_PROLOGUE_AT = 1
N_FF = D_FF // FF_CHUNK


def _dot(a, b):
    return jnp.dot(a, b, preferred_element_type=F32)


def _dot_nt(a, b):
    return lax.dot_general(a, b, (((1,), (1,)), ((), ())), preferred_element_type=F32)


def _rms_norm(x, g):
    ms = jnp.mean(x * x, axis=-1, keepdims=True)
    return x * lax.rsqrt(ms + RMS_EPS) * g


def _layer_norm(x, g, b):
    mu = jnp.mean(x, axis=-1, keepdims=True)
    xc = x - mu
    var = jnp.mean(xc * xc, axis=-1, keepdims=True)
    return xc * lax.rsqrt(var + LN_EPS) * g + b


def _mixer_in_kernel(nb, tt, ln, prompt, n_aliased, x_ref, win_ref, wqb_ref, *rest):
    if prompt:
        wkp_ref, wuv_ref, rest = rest[0], rest[1], rest[2:]
    else:
        wqs_ref, rest = rest[0], rest[1:]
    (wr_ref, wi_ref, ws_ref, bs_ref, qg_ref, kvg_ref, cw_ref, cb_ref, br_ref, bi_ref, lam_ref,
     tqc_ref, tqs_ref, tkc_ref, tks_ref, h0_ref, conv0_ref) = rest[:17]
    rest = rest[17 + n_aliased:]
    if prompt:
        (a_ref, c_ref, q_ref, lat_ref, kr_ref, hst_ref, cst_ref, k_ref, v_ref,
         xb_ref, cv_ref, hc_ref) = rest
        vg_ref = None
    else:
        (a_ref, c_ref, q_ref, lat_ref, kr_ref, hst_ref, cst_ref, vg_ref,
         xb_ref, cv_ref, hc_ref) = rest
        k_ref = v_ref = None
    rows = nb * tt
    t_idx = pl.program_id(1)

    @pl.when(t_idx == 0)
    def _():
        cv_ref[:, 0:SUBLANE, :] = conv0_ref[...]
        hc_ref[...] = h0_ref[...]

    xb_ref[...] = x_ref[...].astype(BF16)

    def proj(c0, c1):
        return _dot(xb_ref[...], win_ref[:, c0:c1])

    xc_in = proj(C_XC, C_GATE)
    z_gate = proj(C_GATE, C_END)
    z_u = proj(C_U, C_V)
    z_v = proj(C_V, C_CQ)

    xcs = []
    for s in range(nb):
        cv_ref[s, SUBLANE:SUBLANE + tt, :] = xc_in[s * tt:(s + 1) * tt]
        acc = cb_ref[...]
        for j in range(LRU_CONV):
            off = SUBLANE - (LRU_CONV - 1) + j
            acc = acc + cv_ref[s, off:off + tt, :] * cw_ref[j:j + 1, :]
        xcs.append(acc)
        tail = cv_ref[s, tt:tt + SUBLANE, :]
        cst_ref[s] = tail
        cv_ref[s, 0:SUBLANE, :] = tail
    xc = xcs[0] if nb == 1 else jnp.concatenate(xcs, axis=0)
    xcb = xc.astype(BF16)
    z_r = _dot(xcb, wr_ref[...])
    z_i = _dot(xcb, wi_ref[...])
    z_cq = proj(C_CQ, C_CKV)
    z_ckv = proj(C_CKV, C_KR)
    kr2 = proj(C_KR, C_XC)

    gpb = ln // SUBLANE
    sub = lax.broadcasted_iota(jnp.int32, (gpb, SUBLANE, D_C), 1)
    soft_lam = jax.nn.softplus(-lam_ref[...])
    chain = {}

    def lru_block(c):
        r0 = c * ln
        seq, first = divmod(r0, tt)
        r = jax.nn.sigmoid(z_r[r0:r0 + ln] + br_ref[...])
        ig = jax.nn.sigmoid(z_i[r0:r0 + ln] + bi_ref[...])
        log_a = (-LRU_C) * r * soft_lam
        a = jnp.exp(log_a)
        b_in = jnp.sqrt(-jnp.tanh(log_a) * (a * a + 1.0)) * (ig * xc[r0:r0 + ln])
        a3 = a.reshape(gpb, SUBLANE, D_C)
        b3 = b_in.reshape(gpb, SUBLANE, D_C)
        for k in (1, 2, 4):
            keep = sub >= k
            a_sh = jnp.where(keep, pltpu.roll(a3, k, 1), 1.0)
            b_sh = jnp.where(keep, pltpu.roll(b3, k, 1), 0.0)
            b3 = a3 * b_sh + b3
            a3 = a3 * a_sh
        hb = hc_ref[seq] if first == 0 else chain['h']
        h_rows = []
        for g in range(gpb):
            hr = a3[g] * hb + b3[g]
            h_rows.append(hr)
            hb = jnp.broadcast_to(hr[SUBLANE - 1:SUBLANE, :], (SUBLANE, D_C))
        chain['h'] = hb
        if first + ln == tt:
            hc_ref[seq] = hb
            hst_ref[seq] = hb
        c_ref[r0:r0 + ln, :] = (jnp.concatenate(h_rows, axis=0)
                                * jax.nn.gelu(z_gate[r0:r0 + ln])).astype(BF16)

    cqn = _rms_norm(z_cq, qg_ref[...]).astype(BF16)
    tqc = tqc_ref[...]
    tqs = tqs_ref[...]
    ckvn = _rms_norm(z_ckv, kvg_ref[...])
    def put_layer(ref, val):
        if n_aliased:
            ref[...] = val
        else:
            ref[0] = val
            if ref.shape[0] > 1:
                ref[1:] = jnp.zeros((ref.shape[0] - 1,) + val.shape, val.dtype)

    put_layer(lat_ref, ckvn)
    ckvn_b = ckvn.astype(BF16)
    kr_rot = kr2[:, :HEAD_BLK] * tkc_ref[...] + kr2[:, HEAD_BLK:] * tks_ref[...]
    put_layer(kr_ref, kr_rot[:, :ROPE_DIM])

    v = jax.nn.gelu(z_v)
    if vg_ref is not None:
        vg_ref[...] = v
    vb = v.astype(BF16)
    row_i = lax.broadcasted_iota(jnp.int32, (ln, ln), 0)
    col_i = lax.broadcasted_iota(jnp.int32, (ln, ln), 1)
    w_tril = [jnp.where(row_i >= col_i, ws_ref[h], 0.0).astype(BF16) for h in range(N_A_HEADS)]
    head_of_lane = lax.broadcasted_iota(jnp.int32, (ln, D_A), 1) // A_HEAD

    def gmlp_chunk(c):
        r0 = c * ln
        vc = vb[r0:r0 + ln]
        gate = bs_ref[...]
        for h in range(N_A_HEADS):
            gate = gate + _dot(w_tril[h], jnp.where(head_of_lane == h, vc, jnp.zeros_like(vc)))
        a_ref[r0:r0 + ln, :] = (jax.nn.gelu(z_u[r0:r0 + ln]) * gate).astype(BF16)

    def q_pair(j):
        c0, c1 = 2 * j * HEAD_BLK, 2 * (j + 1) * HEAD_BLK
        if prompt:
            qb = _dot_nt(wqb_ref[c0:c1, :], cqn)
            half = ROPE_DIM // 2
            for i in range(2):
                r0 = i * HEAD_BLK
                r1, r2 = qb[r0:r0 + half], qb[r0 + half:r0 + ROPE_DIM]
                rest = qb[r0 + ROPE_DIM:r0 + HEAD_BLK] * (ATTN_SCALE * LOG2E)
                blk = jnp.concatenate([r1 * tqc - r2 * tqs, r1 * tqs + r2 * tqc, rest], axis=0)
                q_ref[c0 + r0:c0 + r0 + HEAD_BLK, :] = blk.astype(BF16)
        else:
            q_ref[:, c0:c1] = (_dot(cqn, wqb_ref[:, c0:c1]) * tqc
                               + _dot(cqn, wqs_ref[:, c0:c1]) * tqs).astype(BF16)

    def k_pair(j):
        c0 = 2 * j * HEAD_BLK
        kp = _dot(ckvn_b, wkp_ref[:, c0:c0 + 2 * HEAD_BLK])
        k_ref[:, c0:c0 + HEAD_BLK] = (kp[:, :HEAD_BLK] + kr_rot).astype(BF16)
        k_ref[:, c0 + HEAD_BLK:c0 + 2 * HEAD_BLK] = (kp[:, HEAD_BLK:] + kr_rot).astype(BF16)

    def v_all():
        v_t = _dot_nt(wuv_ref[...], ckvn_b).astype(BF16)
        for h in range(N_B_HEADS):
            v_ref[h * V_AUG:h * V_AUG + V_DIM, :] = v_t[h * V_DIM:(h + 1) * V_DIM]
            v_ref[h * V_AUG + V_DIM:(h + 1) * V_AUG, :] = jnp.ones((V_ONES, rows), BF16)

    n_blocks = rows // ln
    mla = [functools.partial(q_pair, j) for j in range(N_B_HEADS // 2)]
    if prompt:
        mla += [functools.partial(k_pair, j) for j in range(N_B_HEADS // 2)] + [v_all]
    per_block = -(-len(mla) // n_blocks)
    for c in range(n_blocks):
        for piece in mla[c * per_block:(c + 1) * per_block]:
            piece()
        gmlp_chunk(c)
        lru_block(c)


def _layer_spec(w, layer, single_buffer=False):
    tail = (0,) * (w.ndim - 1)
    mode = dict(pipeline_mode=pl.Buffered(1)) if single_buffer else {}
    return pl.BlockSpec((None,) + w.shape[1:], lambda *_: (layer,) + tail, **mode)


def _mixer_in(x2d, lw, layer, tabs, h0, conv0, stacked, *, n_seq, seq_len, nb, tt, ln, prompt):
    depth = lw['w_in'].shape[0]
    nt = seq_len // tt
    assert nb == 1 or nt == 1
    rows = nb * tt
    n_rows = n_seq * seq_len
    grid = (n_seq // nb, nt)
    row_map = lambda b, t: (b * nt + t, 0)
    stacked = tuple(stacked or ())

    def layer_out_spec(width):
        if stacked:
            return pl.BlockSpec((None, rows, width), lambda b, t: (layer, b * nt + t, 0))
        assert layer == 0
        return pl.BlockSpec((depth, rows, width), lambda b, t: (0, b * nt + t, 0))

    tab_map = lambda b, t: (t, 0)
    seq_map = lambda b, t: (b, 0, 0)

    mode_weights = [lw['wq_blk_t'], lw['wk_pad'], lw['w_uv_t']] if prompt else [lw['wq_blk'], lw['wq_swp']]
    weights = [lw['w_in']] + mode_weights + [
        lw['w_r'], lw['w_i'], lw['w_s'], lw['b_s'], lw['q_g'], lw['kv_g'], lw['conv_w'], lw['conv_b'],
        lw['b_r'], lw['b_i'], lw['lam']]
    col_map = lambda b, t: (0, b * nt + t)
    if prompt:
        q_tab_spec = pl.BlockSpec((ROPE_DIM // 2, rows), lambda b, t: (0, t))
        q_shape = jax.ShapeDtypeStruct((N_B_HEADS * HEAD_BLK, n_rows), BF16)
        q_spec = pl.BlockSpec((N_B_HEADS * HEAD_BLK, rows), col_map)
    else:
        q_tab_spec = pl.BlockSpec((rows, 2 * HEAD_BLK), tab_map)
        q_shape = jax.ShapeDtypeStruct((n_rows, N_B_HEADS * HEAD_BLK), BF16)
        q_spec = pl.BlockSpec((rows, N_B_HEADS * HEAD_BLK), row_map)
    in_specs = ([pl.BlockSpec((rows, D_MODEL), row_map)]
                + [_layer_spec(w, layer) for w in weights]
                + [q_tab_spec, q_tab_spec,
                   pl.BlockSpec((rows, HEAD_BLK), tab_map), pl.BlockSpec((rows, HEAD_BLK), tab_map),
                   pl.BlockSpec((nb, SUBLANE, D_C), seq_map), pl.BlockSpec((nb, SUBLANE, D_C), seq_map)])
    out_shape = [jax.ShapeDtypeStruct((n_rows, D_A), BF16),
                 jax.ShapeDtypeStruct((n_rows, D_C), BF16),
                 q_shape,
                 jax.ShapeDtypeStruct((depth, n_rows, KV_LORA), F32),
                 jax.ShapeDtypeStruct((depth, n_rows, ROPE_DIM), F32),
                 jax.ShapeDtypeStruct((n_seq, SUBLANE, D_C), F32),
                 jax.ShapeDtypeStruct((n_seq, SUBLANE, D_C), F32)]
    out_specs = [pl.BlockSpec((rows, D_A), row_map), pl.BlockSpec((rows, D_C), row_map),
                 q_spec, layer_out_spec(KV_LORA), layer_out_spec(ROPE_DIM),
                 pl.BlockSpec((nb, SUBLANE, D_C), seq_map), pl.BlockSpec((nb, SUBLANE, D_C), seq_map)]
    if prompt:
        out_shape += [jax.ShapeDtypeStruct((n_rows, N_B_HEADS * HEAD_BLK), BF16),
                      jax.ShapeDtypeStruct((N_B_HEADS * V_AUG, n_rows), BF16)]
        out_specs += [pl.BlockSpec((rows, N_B_HEADS * HEAD_BLK), row_map),
                      pl.BlockSpec((N_B_HEADS * V_AUG, rows), col_map)]
    else:
        out_shape += [jax.ShapeDtypeStruct((n_rows, D_A), F32)]
        out_specs += [pl.BlockSpec((rows, D_A), row_map)]
    scratch = [pltpu.VMEM((rows, D_MODEL), BF16),
               pltpu.VMEM((nb, tt + SUBLANE, D_C), F32),
               pltpu.VMEM((nb, SUBLANE, D_C), F32)]
    n_in = len(in_specs)
    in_specs += [pl.BlockSpec(memory_space=pl.ANY)] * len(stacked)
    aliases = {n_in + i: 3 + i for i in range(len(stacked))}
    return pl.pallas_call(
        functools.partial(_mixer_in_kernel, nb, tt, ln, prompt, len(stacked)),
        grid=grid, in_specs=in_specs, out_specs=out_specs, out_shape=out_shape,
        scratch_shapes=scratch, input_output_aliases=aliases,
        compiler_params=pltpu.CompilerParams(dimension_semantics=("arbitrary", "arbitrary"),
                                             vmem_limit_bytes=VMEM_LIMIT),
        name="mixer_in_prompt" if prompt else "mixer_in_sample",
    )(x2d, *weights, *tabs, h0, conv0, *stacked)


NEG_BIG = -1e30
LOG2E = math.log2(math.e)
V_ONES = 16
V_AUG = V_DIM + V_ONES
ATTN_HEAD_GROUP = 2
ATTN_DIAG_PARTS = 4


def _last_kv_tile(qi, tq, tk):
    return ((qi + 1) * tq - 1) // tk


def _kv_steps(nq, tq, tk):
    pairs = [(i, j) for i in range(nq) for j in range(_last_kv_tile(i, tq, tk) + 1)]
    return (jnp.asarray([p[0] for p in pairs], jnp.int32), jnp.asarray([p[1] for p in pairs], jnp.int32))


def _attn_prompt_kernel(tq, tk, qi_tab, ki_tab, q_ref, k_ref, v_ref, o_ref, m_ref, acc_ref):
    step_idx = pl.program_id(1)
    qi = qi_tab[step_idx]
    ki = ki_tab[step_idx]

    @pl.when(ki == 0)
    def _():
        m_ref[...] = jnp.full(m_ref.shape, NEG_BIG, F32)
        acc_ref[...] = jnp.zeros(acc_ref.shape, F32)

    pw = tq // ATTN_DIAG_PARTS

    def step(diag):
        if diag:
            local = (lax.broadcasted_iota(jnp.int32, (pw, pw), 0) // CHUNK
                     <= lax.broadcasted_iota(jnp.int32, (pw, pw), 1) // CHUNK)
            parts = tuple((c * pw, (c + 1) * pw, (c + 1) * pw) for c in range(ATTN_DIAG_PARTS))
        else:
            parts = ((0, tq, tk),)

        def scores(h):
            blk = slice(h * HEAD_BLK, (h + 1) * HEAD_BLK)
            return [_dot(k_ref[0:nk, blk], q_ref[blk, c0:c1]) for (c0, c1, nk) in parts]

        def softmax(h, s_parts):
            out = []
            for (c0, c1, nk), s in zip(parts, s_parts):
                if diag:
                    low = jnp.where(local, s[nk - pw:], NEG_BIG)
                    s = low if nk == pw else jnp.concatenate([s[:nk - pw], low], axis=0)
                m_prev = m_ref[h, :, c0:c1]
                m_new = jnp.maximum(m_prev, jnp.max(s, axis=0, keepdims=True))
                m_ref[h, :, c0:c1] = m_new
                out.append((jnp.exp2(m_prev - m_new), jnp.exp2(s - m_new).astype(BF16)))
            return out

        group = ATTN_HEAD_GROUP
        s_next = [scores(h) for h in range(group)]
        for h0 in range(0, N_B_HEADS, group):
            s_cur = s_next
            if h0 + group < N_B_HEADS:
                s_next = [scores(h) for h in range(h0 + group, h0 + 2 * group)]
            probs = [softmax(h0 + i, s_cur[i]) for i in range(group)]
            for i, prob_parts in enumerate(probs):
                rows = slice((h0 + i) * V_AUG, (h0 + i + 1) * V_AUG)
                for (c0, c1, nk), (alpha, p) in zip(parts, prob_parts):
                    acc_ref[rows, c0:c1] = alpha * acc_ref[rows, c0:c1] + _dot(v_ref[rows, 0:nk], p)

    def finalize():
        for j in range(N_B_HEADS // 2):
            halves = []
            for h in (2 * j, 2 * j + 1):
                denom = acc_ref[h * V_AUG + V_DIM:h * V_AUG + V_DIM + 1, :]
                halves.append(acc_ref[h * V_AUG:h * V_AUG + V_DIM, :] / denom)
            o_ref[:, j * LANE:(j + 1) * LANE] = jnp.concatenate(halves, axis=0).T.astype(BF16)

    @pl.when(ki == qi)
    def _():
        step(True)
        finalize()

    @pl.when(ki != qi)
    def _():
        step(False)


def _attn_prompt(q_t, k, v_t, *, n_seq, seq_len, tq, tk):
    assert tq == tk and (tq // ATTN_DIAG_PARTS) % max(CHUNK, LANE) == 0
    nq, nk = seq_len // tq, seq_len // tk
    qi_tab, ki_tab = _kv_steps(nq, tq, tk)
    grid_spec = pltpu.PrefetchScalarGridSpec(
        num_scalar_prefetch=2,
        grid=(n_seq, qi_tab.shape[0]),
        in_specs=[pl.BlockSpec((N_B_HEADS * HEAD_BLK, tq), lambda b, s, qt, kt: (0, b * nq + qt[s])),
                  pl.BlockSpec((tk, N_B_HEADS * HEAD_BLK), lambda b, s, qt, kt: (b * nk + kt[s], 0)),
                  pl.BlockSpec((N_B_HEADS * V_AUG, tk), lambda b, s, qt, kt: (0, b * nk + kt[s]))],
        out_specs=pl.BlockSpec((tq, D_B), lambda b, s, qt, kt: (b * nq + qt[s], 0)),
        scratch_shapes=[pltpu.VMEM((N_B_HEADS, 1, tq), F32),
                        pltpu.VMEM((N_B_HEADS * V_AUG, tq), F32)])
    return pl.pallas_call(
        functools.partial(_attn_prompt_kernel, tq, tk),
        grid_spec=grid_spec,
        out_shape=jax.ShapeDtypeStruct((n_seq * seq_len, D_B), BF16),
        compiler_params=pltpu.CompilerParams(dimension_semantics=("arbitrary", "arbitrary"),
                                             vmem_limit_bytes=VMEM_LIMIT),
        name="attn_prompt",
    )(qi_tab, ki_tab, q_t, k, v_t)


def _attn_sample_kernel(t_new, tk, q_ref, clat_ref, ckr_ref, nlat_ref, nkr_ref, wkt_ref, wv_ref, o_ref):
    past = clat_ref.shape[0]
    ql, qr = [], []
    for h in range(N_B_HEADS):
        q_h = q_ref[:, h * HEAD_BLK:(h + 1) * HEAD_BLK]
        ql.append(_dot(q_h, wkt_ref[h]).astype(BF16))
        qr.append(q_h[:, :ROPE_DIM])
    ql = jnp.concatenate(ql, axis=0)
    qr = jnp.concatenate(qr, axis=0)

    def scores(i):
        if i == 0:
            c_b = nlat_ref[...].astype(BF16)
            return _dot_nt(ql, c_b) + _dot_nt(qr, nkr_ref[...].astype(BF16)), c_b
        r0 = (i - 1) * tk
        c_b = clat_ref[r0:r0 + tk, :].astype(BF16)
        return _dot_nt(ql, c_b) + _dot(qr, ckr_ref[:, r0:r0 + tk].astype(BF16)), c_b

    rows = N_B_HEADS * t_new
    m = jnp.full((rows, 1), NEG_BIG, F32)
    l = jnp.zeros((rows, 1), F32)
    acc = jnp.zeros((rows, KV_LORA), F32)
    n_tiles = 1 + past // tk
    nxt = scores(0)
    for i in range(n_tiles):
        s, c_b = nxt
        if i + 1 < n_tiles:
            nxt = scores(i + 1)
        m_new = jnp.maximum(m, jnp.max(s, axis=1, keepdims=True))
        alpha = jnp.exp2(m - m_new)
        p = jnp.exp2(s - m_new)
        l = alpha * l + jnp.sum(p, axis=1, keepdims=True)
        acc = alpha * acc + _dot(p.astype(BF16), c_b)
        m = m_new
    o_lat = (acc / l).astype(BF16)
    out = _dot(o_lat[0:t_new], wv_ref[0])
    for h in range(1, N_B_HEADS):
        out = out + _dot(o_lat[h * t_new:(h + 1) * t_new], wv_ref[h])
    o_ref[...] = out.astype(BF16)


def _attn_sample(q, cache_lat, cache_kr, new_lat, new_kr, wkt, wv_pad, *, layer, n_seq, t_new, tk):
    past = cache_lat.shape[2]
    assert past % tk == 0
    row_map = lambda b: (b, 0)
    return pl.pallas_call(
        functools.partial(_attn_sample_kernel, t_new, tk),
        grid=(n_seq,),
        in_specs=[pl.BlockSpec((t_new, N_B_HEADS * HEAD_BLK), row_map),
                  pl.BlockSpec((None, None, past, KV_LORA), lambda b: (layer, b, 0, 0)),
                  pl.BlockSpec((None, None, ROPE_DIM, past), lambda b: (layer, b, 0, 0)),
                  pl.BlockSpec((None, t_new, KV_LORA), lambda b: (layer, b, 0)),
                  pl.BlockSpec((None, t_new, ROPE_DIM), lambda b: (layer, b, 0)),
                  _layer_spec(wkt, layer), _layer_spec(wv_pad, layer)],
        out_specs=pl.BlockSpec((t_new, D_B), row_map),
        out_shape=jax.ShapeDtypeStruct((n_seq * t_new, D_B), BF16),
        compiler_params=pltpu.CompilerParams(dimension_semantics=("arbitrary",),
                                             vmem_limit_bytes=VMEM_LIMIT),
        name="attn_sample",
    )(q, cache_lat, cache_kr, new_lat, new_kr, wkt, wv_pad)


def _mixer_out_kernel(nb, tt, n_sub,
                      x_ref, a_ref, b_ref, c_ref, wo_ref, g1_ref, b1_ref,
                      wu_ref, cw_ref, cb_ref, wd_ref, g2_ref, b2_ref, s_ref,
                      y_ref, o_ref,
                      x1_ref, xp_ref, x1b_ref, acc_ref, bufg0_ref, bufg1_ref, bufv0_ref, bufv1_ref,
                      car_ref, act0_ref, act1_ref):
    bufg_ref = (bufg0_ref, bufg1_ref)
    bufv_ref = (bufv0_ref, bufv1_ref)
    act_ref = (act0_ref, act1_ref)
    t_idx = pl.program_id(1)
    rows = nb * tt
    n_tiles = rows // SUBLANE
    halo = (FFN_CONV - 1) * SUBLANE

    @pl.when(t_idx == 0)
    def _():
        car_ref[...] = s_ref[...]

    n_items = n_sub * N_FF

    def prologue(t):
        rs = slice(t * rows, (t + 1) * rows)
        mix = (_dot(a_ref[rs, :], wo_ref[0:D_A, :]) + _dot(b_ref[rs, :], wo_ref[D_A:D_A + D_B, :])
               + _dot(c_ref[rs, :], wo_ref[D_A + D_B:, :]))
        x1 = _layer_norm(ALPHA * x_ref[rs, :] + mix, g1_ref[...], b1_ref[...])
        x1_ref[rs, :] = x1
        for c in range(D_MODEL // LANE):
            for k in range(n_tiles):
                xp_ref[c, pl.ds(_interleave_start(k, n_tiles), SUBLANE, stride=SUBLANE), :] = (
                    x1[k * SUBLANE:(k + 1) * SUBLANE, c * LANE:(c + 1) * LANE])
            x1b_ref[t, :, c * LANE:(c + 1) * LANE] = xp_ref[c].astype(BF16)

    def epilogue(t):
        rs = slice(t * rows, (t + 1) * rows)
        ffn = jnp.concatenate(
            [jnp.concatenate([acc_ref[t, c, pl.ds(_interleave_start(k, n_tiles), SUBLANE, stride=SUBLANE), :]
                              for k in range(n_tiles)], axis=0)
             for c in range(D_MODEL // LANE)], axis=1)
        y_ref[rs, :] = _layer_norm(ALPHA * x1_ref[rs, :] + ffn, g2_ref[...], b2_ref[...])

    def cols(j, value_half):
        c0 = (D_FF if value_half else 0) + j * FF_CHUNK
        return slice(c0, c0 + FF_CHUNK)

    first_sublane = lax.broadcasted_iota(jnp.int32, (SUBLANE, FF_CHUNK), 0) == 0

    def stage(up, buf, cs):
        buf[halo:halo + rows, :] = up
        for i in range(FFN_CONV - 1):
            carry_rows = slice(i * SUBLANE, (i + 1) * SUBLANE)
            last = up[rows - halo + i * SUBLANE:rows - halo + (i + 1) * SUBLANE]
            if nb == 1:
                last = pltpu.roll(last, 1, 0)
                prev = jnp.where(first_sublane, car_ref[carry_rows, cs], last)
            else:
                prev = car_ref[carry_rows, cs]
            buf[carry_rows, :] = prev
            car_ref[carry_rows, cs] = last
            o_ref[carry_rows, cs] = last

    def conv_rows(buf, cs, r0, nr):
        acc = cb_ref[:, cs]
        for i in range(FFN_CONV):
            acc = acc + buf[r0 + i * SUBLANE:r0 + i * SUBLANE + nr, :] * cw_ref[i:i + 1, cs]
        return acc

    def up_proj(i):
        t, j = divmod(i, N_FF)
        xb = x1b_ref[t]
        stage(_dot(xb, wu_ref[:, cols(j, False)]), bufg_ref[i % 2], cols(j, False))
        stage(_dot(xb, wu_ref[:, cols(j, True)]), bufv_ref[i % 2], cols(j, True))

    groups_per_tile = -(-N_FF // DOWN_GROUP)

    def act_slot(t, grp):
        return act_ref[(t * groups_per_tile + grp) % 2]

    def elementwise(i):
        t, j = divmod(i, N_FF)
        grp, pos = divmod(j, DOWN_GROUP)
        for r0 in range(0, rows, ROW_BLK):
            nr = min(ROW_BLK, rows - r0)
            gate = conv_rows(bufg_ref[i % 2], cols(j, False), r0, nr)
            val = conv_rows(bufv_ref[i % 2], cols(j, True), r0, nr)
            act_slot(t, grp)[r0:r0 + nr, pos * FF_CHUNK:(pos + 1) * FF_CHUNK] = (
                jax.nn.gelu(gate) * val).astype(BF16)

    def down_proj(t, grp):
        n_chunks = min(DOWN_GROUP, N_FF - grp * DOWN_GROUP)
        k0, width = grp * DOWN_GROUP * FF_CHUNK, n_chunks * FF_CHUNK
        part = _dot(act_slot(t, grp)[:, 0:width], wd_ref[k0:k0 + width, :])
        for c in range(D_MODEL // LANE):
            if grp == 0:
                acc_ref[t, c] = part[:, c * LANE:(c + 1) * LANE]
            else:
                acc_ref[t, c] += part[:, c * LANE:(c + 1) * LANE]

    prologue(0)
    up_proj(0)
    pending = None
    for i in range(n_items + 1):
        if i + 1 < n_items:
            up_proj(i + 1)
        if i % N_FF == NEXT_PROLOGUE_AT and i // N_FF + 1 < n_sub:
            prologue(i // N_FF + 1)
        if pending is not None:
            down_proj(*pending)
            if pending[1] == groups_per_tile - 1:
                epilogue(pending[0])
            pending = None
        if i < n_items:
            elementwise(i)
            t, j = divmod(i, N_FF)
            if (j + 1) % DOWN_GROUP == 0 or j == N_FF - 1:
                pending = (t, j // DOWN_GROUP)


def _interleave_start(k, n_tiles):
    group, part = divmod(k, n_tiles // SUBLANE)
    return part * SUBLANE * SUBLANE + group


FFN_STATE_ROWS = (FFN_CONV - 1) * SUBLANE


def _mixer_out(x2d, a, b, c, lw, layer, state, *, n_seq, seq_len, nb, tt, n_sub):
    nt = seq_len // (tt * n_sub)
    assert nb == 1 or (nt == 1 and n_sub == 1)
    rows = nb * tt
    blk_rows = rows * n_sub
    assert nb in (1, SUBLANE) and rows % (SUBLANE * SUBLANE) == 0
    n_rows = n_seq * seq_len
    row_map = lambda bi, t: (bi * nt + t, 0)
    weights_a = [lw['w_o'], lw['ln1_g'], lw['ln1_b'], lw['w_up'], lw['ffn_cw'], lw['ffn_cb'],
                 lw['w_down'], lw['ln2_g'], lw['ln2_b']]
    st_spec = pl.BlockSpec((None, FFN_STATE_ROWS, 2 * D_FF), lambda bi, t: (bi, 0, 0))
    in_specs = ([pl.BlockSpec((blk_rows, D_MODEL), row_map), pl.BlockSpec((blk_rows, D_A), row_map),
                 pl.BlockSpec((blk_rows, D_B), row_map), pl.BlockSpec((blk_rows, D_C), row_map)]
                + [_layer_spec(w, layer, single_buffer=True) for w in weights_a]
                + [st_spec])
    st_shape = jax.ShapeDtypeStruct((n_seq // nb, FFN_STATE_ROWS, 2 * D_FF), F32)
    buf_rows = FFN_STATE_ROWS + rows
    return pl.pallas_call(
        functools.partial(_mixer_out_kernel, nb, tt, n_sub),
        grid=(n_seq // nb, nt),
        in_specs=in_specs,
        out_specs=[pl.BlockSpec((blk_rows, D_MODEL), row_map), st_spec],
        out_shape=[jax.ShapeDtypeStruct((n_rows, D_MODEL), F32), st_shape],
        scratch_shapes=[pltpu.VMEM((blk_rows, D_MODEL), F32),
                        pltpu.VMEM((D_MODEL // LANE, rows, LANE), F32),
                        pltpu.VMEM((n_sub, rows, D_MODEL), BF16),
                        pltpu.VMEM((n_sub, D_MODEL // LANE, rows, LANE), F32),
                        pltpu.VMEM((buf_rows, FF_CHUNK), F32),
                        pltpu.VMEM((buf_rows, FF_CHUNK), F32),
                        pltpu.VMEM((buf_rows, FF_CHUNK), F32),
                        pltpu.VMEM((buf_rows, FF_CHUNK), F32),
                        pltpu.VMEM((FFN_STATE_ROWS, 2 * D_FF), F32),
                        pltpu.VMEM((rows, DOWN_GROUP * FF_CHUNK), BF16),
                        pltpu.VMEM((rows, DOWN_GROUP * FF_CHUNK), BF16)],
        compiler_params=pltpu.CompilerParams(dimension_semantics=("arbitrary", "arbitrary"),
                                             vmem_limit_bytes=VMEM_LIMIT),
        name="mixer_out",
    )(x2d, a, b, c, *weights_a, state)


def _rope_tables(pos, reps, feature_major_q):
    half = ROPE_DIM // 2
    inv = 1.0 / (ROPE_THETA ** (jnp.arange(0, ROPE_DIM, 2, dtype=F32) / ROPE_DIM))
    ang = pos.astype(F32)[:, None] * inv[None, :]
    cos, sin = jnp.cos(ang), jnp.sin(ang)
    n = pos.shape[0]
    zeros = jnp.zeros((n, HEAD_BLK - ROPE_DIM), F32)
    tkc = jnp.concatenate([cos, cos, zeros], axis=1)
    tks = jnp.concatenate([sin, sin, zeros], axis=1)
    tqc = jnp.concatenate([cos, cos, jnp.ones((n, NOPE_DIM), F32),
                           jnp.zeros((n, HEAD_BLK - ROPE_DIM - NOPE_DIM), F32)], axis=1)
    tqc = jnp.concatenate([tqc, tqc], axis=1) * (ATTN_SCALE * LOG2E)
    tqs = jnp.concatenate([tks, tks], axis=1) * (ATTN_SCALE * LOG2E)
    if feature_major_q:
        return ((cos * (ATTN_SCALE * LOG2E)).T, (sin * (ATTN_SCALE * LOG2E)).T, tkc, tks)
    return tuple(jnp.tile(t, (reps, 1)) for t in (tqc, tqs, tkc, tks))


def _block_diag(w):
    depth, n, d, e = w.shape
    return jnp.einsum('lnde,nm->lndme', w, jnp.eye(n, dtype=w.dtype)).reshape(depth, n * d, n * e)


def _prep_weights(p):
    half = ROPE_DIM // 2
    depth = p['w_in'].shape[0]
    bounds = [sum(IN_SPLITS[:i + 1]) for i in range(len(IN_SPLITS) - 1)]
    u, v, cq, ckv, kr, xc, gate = jnp.split(p['w_in'], bounds, axis=2)
    kr1, kr2 = kr[..., :half], kr[..., half:]
    z = jnp.zeros((depth, D_MODEL, HEAD_BLK - ROPE_DIM), F32)
    w_in = jnp.concatenate([u, v, cq, ckv, kr1, kr2, z, -kr2, kr1, z, xc, gate], axis=2)

    wq = p['mla_w_uq'].reshape(depth, Q_LORA, N_B_HEADS, NOPE_DIM + ROPE_DIM)
    nope, r1, r2 = wq[..., :NOPE_DIM], wq[..., NOPE_DIM:NOPE_DIM + half], wq[..., NOPE_DIM + half:]
    zq = lambda k: jnp.zeros((depth, Q_LORA, N_B_HEADS, k), F32)
    wq_blk = jnp.concatenate([r1, r2, nope, zq(HEAD_BLK - ROPE_DIM - NOPE_DIM)], axis=-1)
    wq_blk = wq_blk.reshape(depth, Q_LORA, -1).astype(BF16)
    wq_swp = jnp.concatenate([-r2, r1, zq(HEAD_BLK - ROPE_DIM)], axis=-1).reshape(depth, Q_LORA, -1).astype(BF16)

    w_uk, w_uv = p['mla_w_uk'], p['mla_w_uv']
    zk = jnp.zeros((depth, KV_LORA, N_B_HEADS, ROPE_DIM), F32)
    wk_pad = jnp.concatenate([zk, w_uk, zk], axis=-1).reshape(depth, KV_LORA, -1)
    zt = jnp.zeros((depth, N_B_HEADS, ROPE_DIM, KV_LORA), F32)
    wkt = jnp.concatenate([zt, jnp.transpose(w_uk, (0, 2, 3, 1)), zt], axis=2)
    wv_pad = jnp.einsum('lchd,hg->lhcgd', w_uv, jnp.eye(N_B_HEADS, dtype=F32)).reshape(
        depth, N_B_HEADS, KV_LORA, D_B)
    w_uv2 = w_uv.reshape(depth, KV_LORA, -1).astype(BF16)

    row = lambda a: a.reshape(depth, 1, -1)
    return dict(
        w_in=w_in.astype(BF16),
        wq_blk=wq_blk, wq_swp=wq_swp,
        wq_blk_t=jnp.swapaxes(wq_blk, 1, 2),
        wk_pad=wk_pad.astype(BF16), w_uv_t=jnp.swapaxes(w_uv2, 1, 2),
        wkt=wkt.astype(BF16), wv_pad=wv_pad.astype(BF16),
        w_r=_block_diag(p['lru_w_r']).astype(BF16), w_i=_block_diag(p['lru_w_i']).astype(BF16),
        q_g=row(p['mla_q_norm_g']), kv_g=row(p['mla_kv_norm_g']),
        conv_w=p['lru_conv_w'], conv_b=row(p['lru_conv_b']),
        b_r=row(p['lru_b_r']), b_i=row(p['lru_b_i']), lam=row(p['lru_lam']),
        w_o=p['w_o'].astype(BF16),
        ln1_g=row(p['ln1_g']), ln1_b=row(p['ln1_b']), ln2_g=row(p['ln2_g']), ln2_b=row(p['ln2_b']),
        w_up=p['ffn_w_up'].astype(BF16),
        ffn_cw=jnp.pad(p['ffn_conv_w'], ((0, 0), (0, SUBLANE - FFN_CONV), (0, 0))),
        ffn_cb=row(p['ffn_conv_b']),
        w_down=p['ffn_w_down'].astype(BF16),
    )


def _gmlp_params(p, ln):
    w_s = p['gmlp_w_s'][:, :, :ln, :ln]
    b_s = jnp.repeat(jnp.swapaxes(p['gmlp_b_s'][:, :, :ln], 1, 2), A_HEAD, axis=2)
    return w_s, b_s


def _ffn_state_in(st, nb):
    n, steps, width = st.shape
    if nb == 1:
        return jnp.pad(st[:, :, None, :], ((0, 0), (0, 0), (0, SUBLANE - 1), (0, 0))).reshape(n, -1, width)
    return jnp.swapaxes(st.reshape(n // nb, nb, steps, width), 1, 2).reshape(n // nb, -1, width)


def _ffn_state_out(st, nb):
    n, _, width = st.shape
    st = st.reshape(n, FFN_CONV - 1, SUBLANE, width)
    if nb == 1:
        return st[:, :, 0, :]
    return jnp.swapaxes(st, 1, 2).reshape(n * nb, FFN_CONV - 1, width)


PROMPT_TT_IN = 1024
PROMPT_TT_OUT = 256
PROMPT_SUB_OUT = 2
PROMPT_TQ = 1024
PROMPT_TK = 1024
SAMPLE_NB = 8
SAMPLE_TK = 1024


def kernel(x_prompt, x_sample, cache_kv_latent, cache_k_rope, state_lru_h, state_lru_conv, state_ffn_conv,
           ln1_g, ln1_b, ln2_g, ln2_b, w_in, w_o, gmlp_w_s, gmlp_b_s, mla_q_norm_g, mla_w_uq,
           mla_kv_norm_g, mla_w_uk, mla_w_uv, lru_conv_w, lru_conv_b, lru_w_r, lru_b_r, lru_w_i, lru_b_i,
           lru_lam, ffn_w_up, ffn_conv_w, ffn_conv_b, ffn_w_down):
    p = dict(ln1_g=ln1_g, ln1_b=ln1_b, ln2_g=ln2_g, ln2_b=ln2_b, w_in=w_in, w_o=w_o, gmlp_w_s=gmlp_w_s,
             gmlp_b_s=gmlp_b_s, mla_q_norm_g=mla_q_norm_g, mla_w_uq=mla_w_uq, mla_kv_norm_g=mla_kv_norm_g,
             mla_w_uk=mla_w_uk, mla_w_uv=mla_w_uv, lru_conv_w=lru_conv_w, lru_conv_b=lru_conv_b,
             lru_w_r=lru_w_r, lru_b_r=lru_b_r, lru_w_i=lru_w_i, lru_b_i=lru_b_i, lru_lam=lru_lam,
             ffn_w_up=ffn_w_up, ffn_conv_w=ffn_conv_w, ffn_conv_b=ffn_conv_b, ffn_w_down=ffn_w_down)
    bp, s_len, _ = x_prompt.shape
    bd, t_len, _ = x_sample.shape
    past = cache_kv_latent.shape[2]
    depth = w_in.shape[0]
    ln_p, ln_d = min(s_len, GMLP_CHUNK), min(t_len, GMLP_CHUNK)

    tabs_p = _rope_tables(jnp.arange(s_len), 1, True)
    tabs_d = _rope_tables(past + jnp.arange(t_len), SAMPLE_NB, False)
    zero_rows = jnp.zeros((bp, SUBLANE, D_C), F32)
    zero_ffn = jnp.zeros((bp, FFN_STATE_ROWS, 2 * D_FF), F32)

    xp = x_prompt.reshape(bp * s_len, D_MODEL)
    xd = x_sample.reshape(bd * t_len, D_MODEL)
    cache_kr_t = jnp.swapaxes(cache_k_rope, 2, 3)
    lw = _prep_weights(p)
    lw_p = dict(lw, **dict(zip(('w_s', 'b_s'), _gmlp_params(p, ln_p))))
    lw_d = dict(lw, **dict(zip(('w_s', 'b_s'), _gmlp_params(p, ln_d))))
    outs = {i: [] for i in (2, 3, 4, 7, 8, 9, 10)}
    stacked_p = stacked_d = None
    for l in range(depth):
        a, c, q, lat_p, kr_p, hst, cst, k, v = _mixer_in(
            xp, lw_p, l, tabs_p, zero_rows, zero_rows, stacked_p,
            n_seq=bp, seq_len=s_len, nb=1, tt=PROMPT_TT_IN, ln=ln_p, prompt=True)
        stacked_p = (lat_p, kr_p)
        b = _attn_prompt(q, k, v, n_seq=bp, seq_len=s_len, tq=PROMPT_TQ, tk=PROMPT_TK)
        xp, ffn_st = _mixer_out(xp, a, b, c, lw, l, zero_ffn,
                                n_seq=bp, seq_len=s_len, nb=1, tt=PROMPT_TT_OUT, n_sub=PROMPT_SUB_OUT)
        outs[2].append(hst[:, 0, :])
        outs[3].append(cst[:, SUBLANE - (LRU_CONV - 1):, :])
        outs[4].append(_ffn_state_out(ffn_st, 1))

        h0 = jnp.broadcast_to(state_lru_h[l][:, None, :], (bd, SUBLANE, D_C))
        conv0 = jnp.pad(state_lru_conv[l], ((0, 0), (SUBLANE - (LRU_CONV - 1), 0), (0, 0)))
        ffn_st0 = _ffn_state_in(state_ffn_conv[l], SAMPLE_NB)
        a, c, q, lat_d, kr_d, hst, cst, vg = _mixer_in(
            xd, lw_d, l, tabs_d, h0, conv0, stacked_d,
            n_seq=bd, seq_len=t_len, nb=SAMPLE_NB, tt=t_len, ln=ln_d, prompt=False)
        stacked_d = (lat_d, kr_d)
        b = _attn_sample(q, cache_kv_latent, cache_kr_t, lat_d, kr_d, lw['wkt'], lw['wv_pad'],
                         layer=l, n_seq=bd, t_new=t_len, tk=SAMPLE_TK)
        xd, ffn_st = _mixer_out(xd, a, b, c, lw, l, ffn_st0,
                                n_seq=bd, seq_len=t_len, nb=SAMPLE_NB, tt=t_len, n_sub=1)
        outs[7].append(vg.reshape(bd, t_len, D_A))
        outs[8].append(hst[:, 0, :])
        outs[9].append(cst[:, SUBLANE - (LRU_CONV - 1):, :])
        outs[10].append(_ffn_state_out(ffn_st, SAMPLE_NB))

    st = {i: jnp.stack(o) for i, o in outs.items()}
    return (xp.reshape(bp, s_len, D_MODEL), xd.reshape(bd, t_len, D_MODEL),
            lat_p.reshape(depth, bp, s_len, KV_LORA), kr_p.reshape(depth, bp, s_len, ROPE_DIM),
            st[2], st[3], st[4],
            lat_d.reshape(depth, bd, t_len, KV_LORA), kr_d.reshape(depth, bd, t_len, ROPE_DIM),
            st[7], st[8], st[9], st[10])
```

```python
import functools
import math

import jax
import jax.numpy as jnp
from jax import lax
from jax.experimental import pallas as pl
from jax.experimental.pallas import tpu as pltpu

F32 = jnp.float32
BF16 = jnp.bfloat16

D_MODEL = 1024
DEPTH = 4
CHUNK = 64
GMLP_CHUNK = 128
D_A = 256
N_A_HEADS = 4
A_HEAD = 64
D_B = 512
N_B_HEADS = 8
V_DIM = 64
NOPE_DIM = 64
ROPE_DIM = 32
Q_LORA = 384
KV_LORA = 256
ROPE_THETA = 10000.0
ATTN_SCALE = (NOPE_DIM + ROPE_DIM) ** -0.5
D_C = 256
LRU_CONV = 4
LRU_C = 8.0
D_FF = 2816
FFN_CONV = 3
ALPHA = (2.0 * DEPTH) ** 0.25
LN_EPS = 1e-5
RMS_EPS = 1e-6

LANE = 128
SUBLANE = 8
HEAD_BLK = 128
VMEM_LIMIT = 56 * 1024 * 1024

C_U = 0
C_V = C_U + D_A
C_CQ = C_V + D_A
C_CKV = C_CQ + Q_LORA
C_KR = C_CKV + KV_LORA
C_XC = C_KR + 2 * HEAD_BLK
C_GATE = C_XC + D_C
C_END = C_GATE + D_C
IN_SPLITS = (D_A, D_A, Q_LORA, KV_LORA, ROPE_DIM, D_C, D_C)

FF_CHUNK = 256
ROW_BLK = 64
DOWN_GROUP = 11
NEXT_PROLOGUE_AT = 1
N_FF = D_FF // FF_CHUNK


def _dot(a, b):
    return jnp.dot(a, b, preferred_element_type=F32)


def _dot_nt(a, b):
    return lax.dot_general(a, b, (((1,), (1,)), ((), ())), preferred_element_type=F32)


def _rms_norm(x, g):
    ms = jnp.mean(x * x, axis=-1, keepdims=True)
    return x * lax.rsqrt(ms + RMS_EPS) * g


def _layer_norm(x, g, b):
    mu = jnp.mean(x, axis=-1, keepdims=True)
    xc = x - mu
    var = jnp.mean(xc * xc, axis=-1, keepdims=True)
    return xc * lax.rsqrt(var + LN_EPS) * g + b


def _mixer_in_kernel(nb, tt, ln, prompt, n_aliased, x_ref, win_ref, wqb_ref, *rest):
    if prompt:
        wkp_ref, wuv_ref, rest = rest[0], rest[1], rest[2:]
    else:
        wqs_ref, rest = rest[0], rest[1:]
    (wr_ref, wi_ref, ws_ref, bs_ref, qg_ref, kvg_ref, cw_ref, cb_ref, br_ref, bi_ref, lam_ref,
     tqc_ref, tqs_ref, tkc_ref, tks_ref, h0_ref, conv0_ref) = rest[:17]
    rest = rest[17 + n_aliased:]
    if prompt:
        (a_ref, c_ref, q_ref, lat_ref, kr_ref, hst_ref, cst_ref, k_ref, v_ref,
         xb_ref, cv_ref, hc_ref) = rest
        vg_ref = None
    else:
        (a_ref, c_ref, q_ref, lat_ref, kr_ref, hst_ref, cst_ref, vg_ref,
         xb_ref, cv_ref, hc_ref) = rest
        k_ref = v_ref = None
    rows = nb * tt
    t_idx = pl.program_id(1)

    @pl.when(t_idx == 0)
    def _():
        cv_ref[:, 0:SUBLANE, :] = conv0_ref[...]
        hc_ref[...] = h0_ref[...]

    xb_ref[...] = x_ref[...].astype(BF16)

    def proj(c0, c1):
        return _dot(xb_ref[...], win_ref[:, c0:c1])

    xc_in = proj(C_XC, C_GATE)
    z_gate = proj(C_GATE, C_END)
    z_u = proj(C_U, C_V)
    z_v = proj(C_V, C_CQ)

    xcs = []
    for s in range(nb):
        cv_ref[s, SUBLANE:SUBLANE + tt, :] = xc_in[s * tt:(s + 1) * tt]
        acc = cb_ref[...]
        for j in range(LRU_CONV):
            off = SUBLANE - (LRU_CONV - 1) + j
            acc = acc + cv_ref[s, off:off + tt, :] * cw_ref[j:j + 1, :]
        xcs.append(acc)
        tail = cv_ref[s, tt:tt + SUBLANE, :]
        cst_ref[s] = tail
        cv_ref[s, 0:SUBLANE, :] = tail
    xc = xcs[0] if nb == 1 else jnp.concatenate(xcs, axis=0)
    xcb = xc.astype(BF16)
    z_r = _dot(xcb, wr_ref[...])
    z_i = _dot(xcb, wi_ref[...])
    z_cq = proj(C_CQ, C_CKV)
    z_ckv = proj(C_CKV, C_KR)
    kr2 = proj(C_KR, C_XC)

    gpb = ln // SUBLANE
    sub = lax.broadcasted_iota(jnp.int32, (gpb, SUBLANE, D_C), 1)
    soft_lam = jax.nn.softplus(-lam_ref[...])
    chain = {}

    def lru_block(c):
        r0 = c * ln
        seq, first = divmod(r0, tt)
        r = jax.nn.sigmoid(z_r[r0:r0 + ln] + br_ref[...])
        ig = jax.nn.sigmoid(z_i[r0:r0 + ln] + bi_ref[...])
        log_a = (-LRU_C) * r * soft_lam
        a = jnp.exp(log_a)
        b_in = jnp.sqrt(-jnp.tanh(log_a) * (a * a + 1.0)) * (ig * xc[r0:r0 + ln])
        a3 = a.reshape(gpb, SUBLANE, D_C)
        b3 = b_in.reshape(gpb, SUBLANE, D_C)
        for k in (1, 2, 4):
            keep = sub >= k
            a_sh = jnp.where(keep, pltpu.roll(a3, k, 1), 1.0)
            b_sh = jnp.where(keep, pltpu.roll(b3, k, 1), 0.0)
            b3 = a3 * b_sh + b3
            a3 = a3 * a_sh
        hb = hc_ref[seq] if first == 0 else chain['h']
        h_rows = []
        for g in range(gpb):
            hr = a3[g] * hb + b3[g]
            h_rows.append(hr)
            hb = jnp.broadcast_to(hr[SUBLANE - 1:SUBLANE, :], (SUBLANE, D_C))
        chain['h'] = hb
        if first + ln == tt:
            hc_ref[seq] = hb
            hst_ref[seq] = hb
        c_ref[r0:r0 + ln, :] = (jnp.concatenate(h_rows, axis=0)
                                * jax.nn.gelu(z_gate[r0:r0 + ln])).astype(BF16)

    cqn = _rms_norm(z_cq, qg_ref[...]).astype(BF16)
    tqc = tqc_ref[...]
    tqs = tqs_ref[...]
    ckvn = _rms_norm(z_ckv, kvg_ref[...])
    def put_layer(ref, val):
        if n_aliased:
            ref[...] = val
        else:
            ref[0] = val
            if ref.shape[0] > 1:
                ref[1:] = jnp.zeros((ref.shape[0] - 1,) + val.shape, val.dtype)

    put_layer(lat_ref, ckvn)
    ckvn_b = ckvn.astype(BF16)
    kr_rot = kr2[:, :HEAD_BLK] * tkc_ref[...] + kr2[:, HEAD_BLK:] * tks_ref[...]
    put_layer(kr_ref, kr_rot[:, :ROPE_DIM])

    v = jax.nn.gelu(z_v)
    if vg_ref is not None:
        vg_ref[...] = v
    vb = v.astype(BF16)
    row_i = lax.broadcasted_iota(jnp.int32, (ln, ln), 0)
    col_i = lax.broadcasted_iota(jnp.int32, (ln, ln), 1)
    w_tril = [jnp.where(row_i >= col_i, ws_ref[h], 0.0).astype(BF16) for h in range(N_A_HEADS)]
    head_of_lane = lax.broadcasted_iota(jnp.int32, (ln, D_A), 1) // A_HEAD

    def gmlp_chunk(c):
        r0 = c * ln
        vc = vb[r0:r0 + ln]
        gate = bs_ref[...]
        for h in range(N_A_HEADS):
            gate = gate + _dot(w_tril[h], jnp.where(head_of_lane == h, vc, jnp.zeros_like(vc)))
        a_ref[r0:r0 + ln, :] = (jax.nn.gelu(z_u[r0:r0 + ln]) * gate).astype(BF16)

    def q_pair(j):
        c0, c1 = 2 * j * HEAD_BLK, 2 * (j + 1) * HEAD_BLK
        if prompt:
            qb = _dot_nt(wqb_ref[c0:c1, :], cqn)
            half = ROPE_DIM // 2
            for i in range(2):
                r0 = i * HEAD_BLK
                r1, r2 = qb[r0:r0 + half], qb[r0 + half:r0 + ROPE_DIM]
                rest = qb[r0 + ROPE_DIM:r0 + HEAD_BLK] * (ATTN_SCALE * LOG2E)
                blk = jnp.concatenate([r1 * tqc - r2 * tqs, r1 * tqs + r2 * tqc, rest], axis=0)
                q_ref[c0 + r0:c0 + r0 + HEAD_BLK, :] = blk.astype(BF16)
        else:
            q_ref[:, c0:c1] = (_dot(cqn, wqb_ref[:, c0:c1]) * tqc
                               + _dot(cqn, wqs_ref[:, c0:c1]) * tqs).astype(BF16)

    def k_pair(j):
        c0 = 2 * j * HEAD_BLK
        kp = _dot(ckvn_b, wkp_ref[:, c0:c0 + 2 * HEAD_BLK])
        k_ref[:, c0:c0 + HEAD_BLK] = (kp[:, :HEAD_BLK] + kr_rot).astype(BF16)
        k_ref[:, c0 + HEAD_BLK:c0 + 2 * HEAD_BLK] = (kp[:, HEAD_BLK:] + kr_rot).astype(BF16)

    def v_all():
        v_t = _dot_nt(wuv_ref[...], ckvn_b).astype(BF16)
        for h in range(N_B_HEADS):
            v_ref[h * V_AUG:h * V_AUG + V_DIM, :] = v_t[h * V_DIM:(h + 1) * V_DIM]
            v_ref[h * V_AUG + V_DIM:(h + 1) * V_AUG, :] = jnp.ones((V_ONES, rows), BF16)

    n_blocks = rows // ln
    mla = [functools.partial(q_pair, j) for j in range(N_B_HEADS // 2)]
    if prompt:
        mla += [functools.partial(k_pair, j) for j in range(N_B_HEADS // 2)] + [v_all]
    per_block = -(-len(mla) // n_blocks)
    for c in range(n_blocks):
        for piece in mla[c * per_block:(c + 1) * per_block]:
            piece()
        gmlp_chunk(c)
        lru_block(c)


def _layer_spec(w, layer, single_buffer=False):
    tail = (0,) * (w.ndim - 1)
    mode = dict(pipeline_mode=pl.Buffered(1)) if single_buffer else {}
    return pl.BlockSpec((None,) + w.shape[1:], lambda *_: (layer,) + tail, **mode)


def _mixer_in(x2d, lw, layer, tabs, h0, conv0, stacked, *, n_seq, seq_len, nb, tt, ln, prompt):
    depth = lw['w_in'].shape[0]
    nt = seq_len // tt
    assert nb == 1 or nt == 1
    rows = nb * tt
    n_rows = n_seq * seq_len
    grid = (n_seq // nb, nt)
    row_map = lambda b, t: (b * nt + t, 0)
    stacked = tuple(stacked or ())

    def layer_out_spec(width):
        if stacked:
            return pl.BlockSpec((None, rows, width), lambda b, t: (layer, b * nt + t, 0))
        assert layer == 0
        return pl.BlockSpec((depth, rows, width), lambda b, t: (0, b * nt + t, 0))

    tab_map = lambda b, t: (t, 0)
    seq_map = lambda b, t: (b, 0, 0)

    mode_weights = [lw['wq_blk_t'], lw['wk_pad'], lw['w_uv_t']] if prompt else [lw['wq_blk'], lw['wq_swp']]
    weights = [lw['w_in']] + mode_weights + [
        lw['w_r'], lw['w_i'], lw['w_s'], lw['b_s'], lw['q_g'], lw['kv_g'], lw['conv_w'], lw['conv_b'],
        lw['b_r'], lw['b_i'], lw['lam']]
    col_map = lambda b, t: (0, b * nt + t)
    if prompt:
        q_tab_spec = pl.BlockSpec((ROPE_DIM // 2, rows), lambda b, t: (0, t))
        q_shape = jax.ShapeDtypeStruct((N_B_HEADS * HEAD_BLK, n_rows), BF16)
        q_spec = pl.BlockSpec((N_B_HEADS * HEAD_BLK, rows), col_map)
    else:
        q_tab_spec = pl.BlockSpec((rows, 2 * HEAD_BLK), tab_map)
        q_shape = jax.ShapeDtypeStruct((n_rows, N_B_HEADS * HEAD_BLK), BF16)
        q_spec = pl.BlockSpec((rows, N_B_HEADS * HEAD_BLK), row_map)
    in_specs = ([pl.BlockSpec((rows, D_MODEL), row_map)]
                + [_layer_spec(w, layer) for w in weights]
                + [q_tab_spec, q_tab_spec,
                   pl.BlockSpec((rows, HEAD_BLK), tab_map), pl.BlockSpec((rows, HEAD_BLK), tab_map),
                   pl.BlockSpec((nb, SUBLANE, D_C), seq_map), pl.BlockSpec((nb, SUBLANE, D_C), seq_map)])
    out_shape = [jax.ShapeDtypeStruct((n_rows, D_A), BF16),
                 jax.ShapeDtypeStruct((n_rows, D_C), BF16),
                 q_shape,
                 jax.ShapeDtypeStruct((depth, n_rows, KV_LORA), F32),
                 jax.ShapeDtypeStruct((depth, n_rows, ROPE_DIM), F32),
                 jax.ShapeDtypeStruct((n_seq, SUBLANE, D_C), F32),
                 jax.ShapeDtypeStruct((n_seq, SUBLANE, D_C), F32)]
    out_specs = [pl.BlockSpec((rows, D_A), row_map), pl.BlockSpec((rows, D_C), row_map),
                 q_spec, layer_out_spec(KV_LORA), layer_out_spec(ROPE_DIM),
                 pl.BlockSpec((nb, SUBLANE, D_C), seq_map), pl.BlockSpec((nb, SUBLANE, D_C), seq_map)]
    if prompt:
        out_shape += [jax.ShapeDtypeStruct((n_rows, N_B_HEADS * HEAD_BLK), BF16),
                      jax.ShapeDtypeStruct((N_B_HEADS * V_AUG, n_rows), BF16)]
        out_specs += [pl.BlockSpec((rows, N_B_HEADS * HEAD_BLK), row_map),
                      pl.BlockSpec((N_B_HEADS * V_AUG, rows), col_map)]
    else:
        out_shape += [jax.ShapeDtypeStruct((n_rows, D_A), F32)]
        out_specs += [pl.BlockSpec((rows, D_A), row_map)]
    scratch = [pltpu.VMEM((rows, D_MODEL), BF16),
               pltpu.VMEM((nb, tt + SUBLANE, D_C), F32),
               pltpu.VMEM((nb, SUBLANE, D_C), F32)]
    n_in = len(in_specs)
    in_specs += [pl.BlockSpec(memory_space=pl.ANY)] * len(stacked)
    aliases = {n_in + i: 3 + i for i in range(len(stacked))}
    return pl.pallas_call(
        functools.partial(_mixer_in_kernel, nb, tt, ln, prompt, len(stacked)),
        grid=grid, in_specs=in_specs, out_specs=out_specs, out_shape=out_shape,
        scratch_shapes=scratch, input_output_aliases=aliases,
        compiler_params=pltpu.CompilerParams(dimension_semantics=("arbitrary", "arbitrary"),
                                             vmem_limit_bytes=VMEM_LIMIT),
        name="mixer_in_prompt" if prompt else "mixer_in_sample",
    )(x2d, *weights, *tabs, h0, conv0, *stacked)


NEG_BIG = -1e30
LOG2E = math.log2(math.e)
V_ONES = 16
V_AUG = V_DIM + V_ONES
ATTN_HEAD_GROUP = 2
ATTN_DIAG_PARTS = 4


def _last_kv_tile(qi, tq, tk):
    return ((qi + 1) * tq - 1) // tk


def _kv_steps(nq, tq, tk):
    pairs = [(i, j) for i in range(nq) for j in range(_last_kv_tile(i, tq, tk) + 1)]
    return (jnp.asarray([p[0] for p in pairs], jnp.int32), jnp.asarray([p[1] for p in pairs], jnp.int32))


def _attn_prompt_kernel(tq, tk, qi_tab, ki_tab, q_ref, k_ref, v_ref, o_ref, m_ref, acc_ref):
    step_idx = pl.program_id(1)
    qi = qi_tab[step_idx]
    ki = ki_tab[step_idx]

    @pl.when(ki == 0)
    def _():
        m_ref[...] = jnp.full(m_ref.shape, NEG_BIG, F32)
        acc_ref[...] = jnp.zeros(acc_ref.shape, F32)

    pw = tq // ATTN_DIAG_PARTS

    def step(diag):
        if diag:
            local = (lax.broadcasted_iota(jnp.int32, (pw, pw), 0) // CHUNK
                     <= lax.broadcasted_iota(jnp.int32, (pw, pw), 1) // CHUNK)
            parts = tuple((c * pw, (c + 1) * pw, (c + 1) * pw) for c in range(ATTN_DIAG_PARTS))
        else:
            parts = ((0, tq, tk),)

        def scores(h):
            blk = slice(h * HEAD_BLK, (h + 1) * HEAD_BLK)
            return [_dot(k_ref[0:nk, blk], q_ref[blk, c0:c1]) for (c0, c1, nk) in parts]

        def softmax(h, s_parts):
            out = []
            for (c0, c1, nk), s in zip(parts, s_parts):
                if diag:
                    low = jnp.where(local, s[nk - pw:], NEG_BIG)
                    s = low if nk == pw else jnp.concatenate([s[:nk - pw], low], axis=0)
                m_prev = m_ref[h, :, c0:c1]
                m_new = jnp.maximum(m_prev, jnp.max(s, axis=0, keepdims=True))
                m_ref[h, :, c0:c1] = m_new
                out.append((jnp.exp2(m_prev - m_new), jnp.exp2(s - m_new).astype(BF16)))
            return out

        group = ATTN_HEAD_GROUP
        s_next = [scores(h) for h in range(group)]
        for h0 in range(0, N_B_HEADS, group):
            s_cur = s_next
            if h0 + group < N_B_HEADS:
                s_next = [scores(h) for h in range(h0 + group, h0 + 2 * group)]
            probs = [softmax(h0 + i, s_cur[i]) for i in range(group)]
            for i, prob_parts in enumerate(probs):
                rows = slice((h0 + i) * V_AUG, (h0 + i + 1) * V_AUG)
                for (c0, c1, nk), (alpha, p) in zip(parts, prob_parts):
                    acc_ref[rows, c0:c1] = alpha * acc_ref[rows, c0:c1] + _dot(v_ref[rows, 0:nk], p)

    def finalize():
        for j in range(N_B_HEADS // 2):
            halves = []
            for h in (2 * j, 2 * j + 1):
                denom = acc_ref[h * V_AUG + V_DIM:h * V_AUG + V_DIM + 1, :]
                halves.append(acc_ref[h * V_AUG:h * V_AUG + V_DIM, :] / denom)
            o_ref[:, j * LANE:(j + 1) * LANE] = jnp.concatenate(halves, axis=0).T.astype(BF16)

    @pl.when(ki == qi)
    def _():
        step(True)
        finalize()

    @pl.when(ki != qi)
    def _():
        step(False)


def _attn_prompt(q_t, k, v_t, *, n_seq, seq_len, tq, tk):
    assert tq == tk and (tq // ATTN_DIAG_PARTS) % max(CHUNK, LANE) == 0
    nq, nk = seq_len // tq, seq_len // tk
    qi_tab, ki_tab = _kv_steps(nq, tq, tk)
    grid_spec = pltpu.PrefetchScalarGridSpec(
        num_scalar_prefetch=2,
        grid=(n_seq, qi_tab.shape[0]),
        in_specs=[pl.BlockSpec((N_B_HEADS * HEAD_BLK, tq), lambda b, s, qt, kt: (0, b * nq + qt[s])),
                  pl.BlockSpec((tk, N_B_HEADS * HEAD_BLK), lambda b, s, qt, kt: (b * nk + kt[s], 0)),
                  pl.BlockSpec((N_B_HEADS * V_AUG, tk), lambda b, s, qt, kt: (0, b * nk + kt[s]))],
        out_specs=pl.BlockSpec((tq, D_B), lambda b, s, qt, kt: (b * nq + qt[s], 0)),
        scratch_shapes=[pltpu.VMEM((N_B_HEADS, 1, tq), F32),
                        pltpu.VMEM((N_B_HEADS * V_AUG, tq), F32)])
    return pl.pallas_call(
        functools.partial(_attn_prompt_kernel, tq, tk),
        grid_spec=grid_spec,
        out_shape=jax.ShapeDtypeStruct((n_seq * seq_len, D_B), BF16),
        compiler_params=pltpu.CompilerParams(dimension_semantics=("arbitrary", "arbitrary"),
                                             vmem_limit_bytes=VMEM_LIMIT),
        name="attn_prompt",
    )(qi_tab, ki_tab, q_t, k, v_t)


def _attn_sample_kernel(t_new, tk, q_ref, clat_ref, ckr_ref, nlat_ref, nkr_ref, wkt_ref, wv_ref, o_ref):
    past = clat_ref.shape[0]
    ql, qr = [], []
    for h in range(N_B_HEADS):
        q_h = q_ref[:, h * HEAD_BLK:(h + 1) * HEAD_BLK]
        ql.append(_dot(q_h, wkt_ref[h]).astype(BF16))
        qr.append(q_h[:, :ROPE_DIM])
    ql = jnp.concatenate(ql, axis=0)
    qr = jnp.concatenate(qr, axis=0)

    def scores(i):
        if i == 0:
            c_b = nlat_ref[...].astype(BF16)
            return _dot_nt(ql, c_b) + _dot_nt(qr, nkr_ref[...].astype(BF16)), c_b
        r0 = (i - 1) * tk
        c_b = clat_ref[r0:r0 + tk, :].astype(BF16)
        return _dot_nt(ql, c_b) + _dot(qr, ckr_ref[:, r0:r0 + tk].astype(BF16)), c_b

    rows = N_B_HEADS * t_new
    m = jnp.full((rows, 1), NEG_BIG, F32)
    l = jnp.zeros((rows, 1), F32)
    acc = jnp.zeros((rows, KV_LORA), F32)
    n_tiles = 1 + past // tk
    nxt = scores(0)
    for i in range(n_tiles):
        s, c_b = nxt
        if i + 1 < n_tiles:
            nxt = scores(i + 1)
        m_new = jnp.maximum(m, jnp.max(s, axis=1, keepdims=True))
        alpha = jnp.exp2(m - m_new)
        p = jnp.exp2(s - m_new)
        l = alpha * l + jnp.sum(p, axis=1, keepdims=True)
        acc = alpha * acc + _dot(p.astype(BF16), c_b)
        m = m_new
    o_lat = (acc / l).astype(BF16)
    out = _dot(o_lat[0:t_new], wv_ref[0])
    for h in range(1, N_B_HEADS):
        out = out + _dot(o_lat[h * t_new:(h + 1) * t_new], wv_ref[h])
    o_ref[...] = out.astype(BF16)


def _attn_sample(q, cache_lat, cache_kr, new_lat, new_kr, wkt, wv_pad, *, layer, n_seq, t_new, tk):
    past = cache_lat.shape[2]
    assert past % tk == 0
    row_map = lambda b: (b, 0)
    return pl.pallas_call(
        functools.partial(_attn_sample_kernel, t_new, tk),
        grid=(n_seq,),
        in_specs=[pl.BlockSpec((t_new, N_B_HEADS * HEAD_BLK), row_map),
                  pl.BlockSpec((None, None, past, KV_LORA), lambda b: (layer, b, 0, 0)),
                  pl.BlockSpec((None, None, ROPE_DIM, past), lambda b: (layer, b, 0, 0)),
                  pl.BlockSpec((None, t_new, KV_LORA), lambda b: (layer, b, 0)),
                  pl.BlockSpec((None, t_new, ROPE_DIM), lambda b: (layer, b, 0)),
                  _layer_spec(wkt, layer), _layer_spec(wv_pad, layer)],
        out_specs=pl.BlockSpec((t_new, D_B), row_map),
        out_shape=jax.ShapeDtypeStruct((n_seq * t_new, D_B), BF16),
        compiler_params=pltpu.CompilerParams(dimension_semantics=("arbitrary",),
                                             vmem_limit_bytes=VMEM_LIMIT),
        name="attn_sample",
    )(q, cache_lat, cache_kr, new_lat, new_kr, wkt, wv_pad)


def _mixer_out_kernel(nb, tt, n_sub,
                      x_ref, a_ref, b_ref, c_ref, wo_ref, g1_ref, b1_ref,
                      wu_ref, cw_ref, cb_ref, wd_ref, g2_ref, b2_ref, s_ref,
                      y_ref, o_ref,
                      x1_ref, xp_ref, x1b_ref, acc_ref, bufg0_ref, bufg1_ref, bufv0_ref, bufv1_ref,
                      car_ref, act0_ref, act1_ref):
    bufg_ref = (bufg0_ref, bufg1_ref)
    bufv_ref = (bufv0_ref, bufv1_ref)
    act_ref = (act0_ref, act1_ref)
    t_idx = pl.program_id(1)
    rows = nb * tt
    n_tiles = rows // SUBLANE
    halo = (FFN_CONV - 1) * SUBLANE

    @pl.when(t_idx == 0)
    def _():
        car_ref[...] = s_ref[...]

    n_items = n_sub * N_FF

    def prologue(t):
        rs = slice(t * rows, (t + 1) * rows)
        mix = (_dot(a_ref[rs, :], wo_ref[0:D_A, :]) + _dot(b_ref[rs, :], wo_ref[D_A:D_A + D_B, :])
               + _dot(c_ref[rs, :], wo_ref[D_A + D_B:, :]))
        x1 = _layer_norm(ALPHA * x_ref[rs, :] + mix, g1_ref[...], b1_ref[...])
        x1_ref[rs, :] = x1
        for c in range(D_MODEL // LANE):
            for k in range(n_tiles):
                xp_ref[c, pl.ds(_interleave_start(k, n_tiles), SUBLANE, stride=SUBLANE), :] = (
                    x1[k * SUBLANE:(k + 1) * SUBLANE, c * LANE:(c + 1) * LANE])
            x1b_ref[t, :, c * LANE:(c + 1) * LANE] = xp_ref[c].astype(BF16)

    def epilogue(t):
        rs = slice(t * rows, (t + 1) * rows)
        ffn = jnp.concatenate(
            [jnp.concatenate([acc_ref[t, c, pl.ds(_interleave_start(k, n_tiles), SUBLANE, stride=SUBLANE), :]
                              for k in range(n_tiles)], axis=0)
             for c in range(D_MODEL // LANE)], axis=1)
        y_ref[rs, :] = _layer_norm(ALPHA * x1_ref[rs, :] + ffn, g2_ref[...], b2_ref[...])

    def cols(j, value_half):
        c0 = (D_FF if value_half else 0) + j * FF_CHUNK
        return slice(c0, c0 + FF_CHUNK)

    first_sublane = lax.broadcasted_iota(jnp.int32, (SUBLANE, FF_CHUNK), 0) == 0

    def stage(up, buf, cs):
        buf[halo:halo + rows, :] = up
        for i in range(FFN_CONV - 1):
            carry_rows = slice(i * SUBLANE, (i + 1) * SUBLANE)
            last = up[rows - halo + i * SUBLANE:rows - halo + (i + 1) * SUBLANE]
            if nb == 1:
                last = pltpu.roll(last, 1, 0)
                prev = jnp.where(first_sublane, car_ref[carry_rows, cs], last)
            else:
                prev = car_ref[carry_rows, cs]
            buf[carry_rows, :] = prev
            car_ref[carry_rows, cs] = last
            o_ref[carry_rows, cs] = last

    def conv_rows(buf, cs, r0, nr):
        acc = cb_ref[:, cs]
        for i in range(FFN_CONV):
            acc = acc + buf[r0 + i * SUBLANE:r0 + i * SUBLANE + nr, :] * cw_ref[i:i + 1, cs]
        return acc

    def up_proj(i):
        t, j = divmod(i, N_FF)
        xb = x1b_ref[t]
        stage(_dot(xb, wu_ref[:, cols(j, False)]), bufg_ref[i % 2], cols(j, False))
        stage(_dot(xb, wu_ref[:, cols(j, True)]), bufv_ref[i % 2], cols(j, True))

    groups_per_tile = -(-N_FF // DOWN_GROUP)

    def act_slot(t, grp):
        return act_ref[(t * groups_per_tile + grp) % 2]

    def elementwise(i):
        t, j = divmod(i, N_FF)
        grp, pos = divmod(j, DOWN_GROUP)
        for r0 in range(0, rows, ROW_BLK):
            nr = min(ROW_BLK, rows - r0)
            gate = conv_rows(bufg_ref[i % 2], cols(j, False), r0, nr)
            val = conv_rows(bufv_ref[i % 2], cols(j, True), r0, nr)
            act_slot(t, grp)[r0:r0 + nr, pos * FF_CHUNK:(pos + 1) * FF_CHUNK] = (
                jax.nn.gelu(gate) * val).astype(BF16)

    def down_proj(t, grp):
        n_chunks = min(DOWN_GROUP, N_FF - grp * DOWN_GROUP)
        k0, width = grp * DOWN_GROUP * FF_CHUNK, n_chunks * FF_CHUNK
        part = _dot(act_slot(t, grp)[:, 0:width], wd_ref[k0:k0 + width, :])
        for c in range(D_MODEL // LANE):
            if grp == 0:
                acc_ref[t, c] = part[:, c * LANE:(c + 1) * LANE]
            else:
                acc_ref[t, c] += part[:, c * LANE:(c + 1) * LANE]

    prologue(0)
    up_proj(0)
    pending = None
    for i in range(n_items + 1):
        if i + 1 < n_items:
            up_proj(i + 1)
        if i % N_FF == NEXT_PROLOGUE_AT and i // N_FF + 1 < n_sub:
            prologue(i // N_FF + 1)
        if pending is not None:
            down_proj(*pending)
            if pending[1] == groups_per_tile - 1:
                epilogue(pending[0])
            pending = None
        if i < n_items:
            elementwise(i)
            t, j = divmod(i, N_FF)
            if (j + 1) % DOWN_GROUP == 0 or j == N_FF - 1:
                pending = (t, j // DOWN_GROUP)


def _interleave_start(k, n_tiles):
    group, part = divmod(k, n_tiles // SUBLANE)
    return part * SUBLANE * SUBLANE + group


FFN_STATE_ROWS = (FFN_CONV - 1) * SUBLANE


def _mixer_out(x2d, a, b, c, lw, layer, state, *, n_seq, seq_len, nb, tt, n_sub):
    nt = seq_len // (tt * n_sub)
    assert nb == 1 or (nt == 1 and n_sub == 1)
    rows = nb * tt
    blk_rows = rows * n_sub
    assert nb in (1, SUBLANE) and rows % (SUBLANE * SUBLANE) == 0
    n_rows = n_seq * seq_len
    row_map = lambda bi, t: (bi * nt + t, 0)
    weights_a = [lw['w_o'], lw['ln1_g'], lw['ln1_b'], lw['w_up'], lw['ffn_cw'], lw['ffn_cb'],
                 lw['w_down'], lw['ln2_g'], lw['ln2_b']]
    st_spec = pl.BlockSpec((None, FFN_STATE_ROWS, 2 * D_FF), lambda bi, t: (bi, 0, 0))
    in_specs = ([pl.BlockSpec((blk_rows, D_MODEL), row_map), pl.BlockSpec((blk_rows, D_A), row_map),
                 pl.BlockSpec((blk_rows, D_B), row_map), pl.BlockSpec((blk_rows, D_C), row_map)]
                + [_layer_spec(w, layer, single_buffer=True) for w in weights_a]
                + [st_spec])
    st_shape = jax.ShapeDtypeStruct((n_seq // nb, FFN_STATE_ROWS, 2 * D_FF), F32)
    buf_rows = FFN_STATE_ROWS + rows
    return pl.pallas_call(
        functools.partial(_mixer_out_kernel, nb, tt, n_sub),
        grid=(n_seq // nb, nt),
        in_specs=in_specs,
        out_specs=[pl.BlockSpec((blk_rows, D_MODEL), row_map), st_spec],
        out_shape=[jax.ShapeDtypeStruct((n_rows, D_MODEL), F32), st_shape],
        scratch_shapes=[pltpu.VMEM((blk_rows, D_MODEL), F32),
                        pltpu.VMEM((D_MODEL // LANE, rows, LANE), F32),
                        pltpu.VMEM((n_sub, rows, D_MODEL), BF16),
                        pltpu.VMEM((n_sub, D_MODEL // LANE, rows, LANE), F32),
                        pltpu.VMEM((buf_rows, FF_CHUNK), F32),
                        pltpu.VMEM((buf_rows, FF_CHUNK), F32),
                        pltpu.VMEM((buf_rows, FF_CHUNK), F32),
                        pltpu.VMEM((buf_rows, FF_CHUNK), F32),
                        pltpu.VMEM((FFN_STATE_ROWS, 2 * D_FF), F32),
                        pltpu.VMEM((rows, DOWN_GROUP * FF_CHUNK), BF16),
                        pltpu.VMEM((rows, DOWN_GROUP * FF_CHUNK), BF16)],
        compiler_params=pltpu.CompilerParams(dimension_semantics=("arbitrary", "arbitrary"),
                                             vmem_limit_bytes=VMEM_LIMIT),
        name="mixer_out",
    )(x2d, a, b, c, *weights_a, state)


def _rope_tables(pos, reps, feature_major_q):
    half = ROPE_DIM // 2
    inv = 1.0 / (ROPE_THETA ** (jnp.arange(0, ROPE_DIM, 2, dtype=F32) / ROPE_DIM))
    ang = pos.astype(F32)[:, None] * inv[None, :]
    cos, sin = jnp.cos(ang), jnp.sin(ang)
    n = pos.shape[0]
    zeros = jnp.zeros((n, HEAD_BLK - ROPE_DIM), F32)
    tkc = jnp.concatenate([cos, cos, zeros], axis=1)
    tks = jnp.concatenate([sin, sin, zeros], axis=1)
    tqc = jnp.concatenate([cos, cos, jnp.ones((n, NOPE_DIM), F32),
                           jnp.zeros((n, HEAD_BLK - ROPE_DIM - NOPE_DIM), F32)], axis=1)
    tqc = jnp.concatenate([tqc, tqc], axis=1) * (ATTN_SCALE * LOG2E)
    tqs = jnp.concatenate([tks, tks], axis=1) * (ATTN_SCALE * LOG2E)
    if feature_major_q:
        return ((cos * (ATTN_SCALE * LOG2E)).T, (sin * (ATTN_SCALE * LOG2E)).T, tkc, tks)
    return tuple(jnp.tile(t, (reps, 1)) for t in (tqc, tqs, tkc, tks))


def _block_diag(w):
    depth, n, d, e = w.shape
    return jnp.einsum('lnde,nm->lndme', w, jnp.eye(n, dtype=w.dtype)).reshape(depth, n * d, n * e)


def _prep_weights(p):
    half = ROPE_DIM // 2
    depth = p['w_in'].shape[0]
    bounds = [sum(IN_SPLITS[:i + 1]) for i in range(len(IN_SPLITS) - 1)]
    u, v, cq, ckv, kr, xc, gate = jnp.split(p['w_in'], bounds, axis=2)
    kr1, kr2 = kr[..., :half], kr[..., half:]
    z = jnp.zeros((depth, D_MODEL, HEAD_BLK - ROPE_DIM), F32)
    w_in = jnp.concatenate([u, v, cq, ckv, kr1, kr2, z, -kr2, kr1, z, xc, gate], axis=2)

    wq = p['mla_w_uq'].reshape(depth, Q_LORA, N_B_HEADS, NOPE_DIM + ROPE_DIM)
    nope, r1, r2 = wq[..., :NOPE_DIM], wq[..., NOPE_DIM:NOPE_DIM + half], wq[..., NOPE_DIM + half:]
    zq = lambda k: jnp.zeros((depth, Q_LORA, N_B_HEADS, k), F32)
    wq_blk = jnp.concatenate([r1, r2, nope, zq(HEAD_BLK - ROPE_DIM - NOPE_DIM)], axis=-1)
    wq_blk = wq_blk.reshape(depth, Q_LORA, -1).astype(BF16)
    wq_swp = jnp.concatenate([-r2, r1, zq(HEAD_BLK - ROPE_DIM)], axis=-1).reshape(depth, Q_LORA, -1).astype(BF16)

    w_uk, w_uv = p['mla_w_uk'], p['mla_w_uv']
    zk = jnp.zeros((depth, KV_LORA, N_B_HEADS, ROPE_DIM), F32)
    wk_pad = jnp.concatenate([zk, w_uk, zk], axis=-1).reshape(depth, KV_LORA, -1)
    zt = jnp.zeros((depth, N_B_HEADS, ROPE_DIM, KV_LORA), F32)
    wkt = jnp.concatenate([zt, jnp.transpose(w_uk, (0, 2, 3, 1)), zt], axis=2)
    wv_pad = jnp.einsum('lchd,hg->lhcgd', w_uv, jnp.eye(N_B_HEADS, dtype=F32)).reshape(
        depth, N_B_HEADS, KV_LORA, D_B)
    w_uv2 = w_uv.reshape(depth, KV_LORA, -1).astype(BF16)

    row = lambda a: a.reshape(depth, 1, -1)
    return dict(
        w_in=w_in.astype(BF16),
        wq_blk=wq_blk, wq_swp=wq_swp,
        wq_blk_t=jnp.swapaxes(wq_blk, 1, 2),
        wk_pad=wk_pad.astype(BF16), w_uv_t=jnp.swapaxes(w_uv2, 1, 2),
        wkt=wkt.astype(BF16), wv_pad=wv_pad.astype(BF16),
        w_r=_block_diag(p['lru_w_r']).astype(BF16), w_i=_block_diag(p['lru_w_i']).astype(BF16),
        q_g=row(p['mla_q_norm_g']), kv_g=row(p['mla_kv_norm_g']),
        conv_w=p['lru_conv_w'], conv_b=row(p['lru_conv_b']),
        b_r=row(p['lru_b_r']), b_i=row(p['lru_b_i']), lam=row(p['lru_lam']),
        w_o=p['w_o'].astype(BF16),
        ln1_g=row(p['ln1_g']), ln1_b=row(p['ln1_b']), ln2_g=row(p['ln2_g']), ln2_b=row(p['ln2_b']),
        w_up=p['ffn_w_up'].astype(BF16),
        ffn_cw=jnp.pad(p['ffn_conv_w'], ((0, 0), (0, SUBLANE - FFN_CONV), (0, 0))),
        ffn_cb=row(p['ffn_conv_b']),
        w_down=p['ffn_w_down'].astype(BF16),
    )


def _gmlp_params(p, ln):
    w_s = p['gmlp_w_s'][:, :, :ln, :ln]
    b_s = jnp.repeat(jnp.swapaxes(p['gmlp_b_s'][:, :, :ln], 1, 2), A_HEAD, axis=2)
    return w_s, b_s


def _ffn_state_in(st, nb):
    n, steps, width = st.shape
    if nb == 1:
        return jnp.pad(st[:, :, None, :], ((0, 0), (0, 0), (0, SUBLANE - 1), (0, 0))).reshape(n, -1, width)
    return jnp.swapaxes(st.reshape(n // nb, nb, steps, width), 1, 2).reshape(n // nb, -1, width)


def _ffn_state_out(st, nb):
    n, _, width = st.shape
    st = st.reshape(n, FFN_CONV - 1, SUBLANE, width)
    if nb == 1:
        return st[:, :, 0, :]
    return jnp.swapaxes(st, 1, 2).reshape(n * nb, FFN_CONV - 1, width)


PROMPT_TT_IN = 1024
PROMPT_TT_OUT = 256
PROMPT_SUB_OUT = 2
PROMPT_TQ = 1024
PROMPT_TK = 1024
SAMPLE_NB = 8
SAMPLE_TK = 1024


def kernel(x_prompt, x_sample, cache_kv_latent, cache_k_rope, state_lru_h, state_lru_conv, state_ffn_conv,
           ln1_g, ln1_b, ln2_g, ln2_b, w_in, w_o, gmlp_w_s, gmlp_b_s, mla_q_norm_g, mla_w_uq,
           mla_kv_norm_g, mla_w_uk, mla_w_uv, lru_conv_w, lru_conv_b, lru_w_r, lru_b_r, lru_w_i, lru_b_i,
           lru_lam, ffn_w_up, ffn_conv_w, ffn_conv_b, ffn_w_down):
    p = dict(ln1_g=ln1_g, ln1_b=ln1_b, ln2_g=ln2_g, ln2_b=ln2_b, w_in=w_in, w_o=w_o, gmlp_w_s=gmlp_w_s,
             gmlp_b_s=gmlp_b_s, mla_q_norm_g=mla_q_norm_g, mla_w_uq=mla_w_uq, mla_kv_norm_g=mla_kv_norm_g,
             mla_w_uk=mla_w_uk, mla_w_uv=mla_w_uv, lru_conv_w=lru_conv_w, lru_conv_b=lru_conv_b,
             lru_w_r=lru_w_r, lru_b_r=lru_b_r, lru_w_i=lru_w_i, lru_b_i=lru_b_i, lru_lam=lru_lam,
             ffn_w_up=ffn_w_up, ffn_conv_w=ffn_conv_w, ffn_conv_b=ffn_conv_b, ffn_w_down=ffn_w_down)
    bp, s_len, _ = x_prompt.shape
    bd, t_len, _ = x_sample.shape
    past = cache_kv_latent.shape[2]
    depth = w_in.shape[0]
    ln_p, ln_d = min(s_len, GMLP_CHUNK), min(t_len, GMLP_CHUNK)

    tabs_p = _rope_tables(jnp.arange(s_len), 1, True)
    tabs_d = _rope_tables(past + jnp.arange(t_len), SAMPLE_NB, False)
    zero_rows = jnp.zeros((bp, SUBLANE, D_C), F32)
    zero_ffn = jnp.zeros((bp, FFN_STATE_ROWS, 2 * D_FF), F32)

    xp = x_prompt.reshape(bp * s_len, D_MODEL)
    xd = x_sample.reshape(bd * t_len, D_MODEL)
    cache_kr_t = jnp.swapaxes(cache_k_rope, 2, 3)
    lw = _prep_weights(p)
    lw_p = dict(lw, **dict(zip(('w_s', 'b_s'), _gmlp_params(p, ln_p))))
    lw_d = dict(lw, **dict(zip(('w_s', 'b_s'), _gmlp_params(p, ln_d))))
    outs = {i: [] for i in (2, 3, 4, 7, 8, 9, 10)}
    stacked_p = stacked_d = None
    for l in range(depth):
        a, c, q, lat_p, kr_p, hst, cst, k, v = _mixer_in(
            xp, lw_p, l, tabs_p, zero_rows, zero_rows, stacked_p,
            n_seq=bp, seq_len=s_len, nb=1, tt=PROMPT_TT_IN, ln=ln_p, prompt=True)
        stacked_p = (lat_p, kr_p)
        b = _attn_prompt(q, k, v, n_seq=bp, seq_len=s_len, tq=PROMPT_TQ, tk=PROMPT_TK)
        xp, ffn_st = _mixer_out(xp, a, b, c, lw, l, zero_ffn,
                                n_seq=bp, seq_len=s_len, nb=1, tt=PROMPT_TT_OUT, n_sub=PROMPT_SUB_OUT)
        outs[2].append(hst[:, 0, :])
        outs[3].append(cst[:, SUBLANE - (LRU_CONV - 1):, :])
        outs[4].append(_ffn_state_out(ffn_st, 1))

        h0 = jnp.broadcast_to(state_lru_h[l][:, None, :], (bd, SUBLANE, D_C))
        conv0 = jnp.pad(state_lru_conv[l], ((0, 0), (SUBLANE - (LRU_CONV - 1), 0), (0, 0)))
        ffn_st0 = _ffn_state_in(state_ffn_conv[l], SAMPLE_NB)
        a, c, q, lat_d, kr_d, hst, cst, vg = _mixer_in(
            xd, lw_d, l, tabs_d, h0, conv0, stacked_d,
            n_seq=bd, seq_len=t_len, nb=SAMPLE_NB, tt=t_len, ln=ln_d, prompt=False)
        stacked_d = (lat_d, kr_d)
        b = _attn_sample(q, cache_kv_latent, cache_kr_t, lat_d, kr_d, lw['wkt'], lw['wv_pad'],
                         layer=l, n_seq=bd, t_new=t_len, tk=SAMPLE_TK)
        xd, ffn_st = _mixer_out(xd, a, b, c, lw, l, ffn_st0,
                                n_seq=bd, seq_len=t_len, nb=SAMPLE_NB, tt=t_len, n_sub=1)
        outs[7].append(vg.reshape(bd, t_len, D_A))
        outs[8].append(hst[:, 0, :])
        outs[9].append(cst[:, SUBLANE - (LRU_CONV - 1):, :])
        outs[10].append(_ffn_state_out(ffn_st, SAMPLE_NB))

    st = {i: jnp.stack(o) for i, o in outs.items()}
    return (xp.reshape(bp, s_len, D_MODEL), xd.reshape(bd, t_len, D_MODEL),
            lat_p.reshape(depth, bp, s_len, KV_LORA), kr_p.reshape(depth, bp, s_len, ROPE_DIM),
            st[2], st[3], st[4],
            lat_d.reshape(depth, bd, t_len, KV_LORA), kr_d.reshape(depth, bd, t_len, ROPE_DIM),
            st[7], st[8], st[9], st[10])
```
